```python
import math
import jax
import jax.numpy as jnp
from jax import lax
import numpy as np

D_MODEL = 2048
BATCH = 2
SEQ = 16384
DEPTH = 4

ROPE_THETA = 10000.0
RMS_EPS = 1e-6
DIFF_SUBLN_EPS = 1e-5
A_HEAD_DIM = 128
A_HEADS = D_MODEL // 256
A_WIDTH = A_HEADS * A_HEAD_DIM
MOBA_BLOCK = 256
MOBA_TOPK = 3
MOBA_Q_CHUNK = 64
B_HEAD_DIM = 128
B_HEADS = D_MODEL // 512
B_WIDTH = 2 * B_HEADS * B_HEAD_DIM
DENSE_Q_BLOCK = 128
EVEN_IN = 3 * A_WIDTH + 3 * B_WIDTH
EVEN_OUT = A_WIDTH + B_WIDTH
C_HEAD_DIM = 64
C_Q_HEADS = D_MODEL // C_HEAD_DIM
C_KV_HEADS = C_Q_HEADS // 8
C_WINDOW = 128
C_Q_WIDTH = C_Q_HEADS * C_HEAD_DIM
C_KV_WIDTH = C_KV_HEADS * C_HEAD_DIM
ODD_IN = C_Q_WIDTH + 2 * C_KV_WIDTH
FFN_HIDDEN = -(-8 * D_MODEL // (3 * 256)) * 256
N_EVEN = (DEPTH + 1) // 2
N_ODD = DEPTH // 2
N_MOD = 6

kernel_name = "hybrid_moba_diff_swa_adaln_trunk"


def rms_norm(x, w=None, eps=RMS_EPS):
    xf = x.astype(jnp.float32)
    y = xf * lax.rsqrt(jnp.mean(xf * xf, axis=-1, keepdims=True) + eps)
    if w is not None:
        y = y * w.astype(jnp.float32)
    return y.astype(x.dtype)


def rope_tables(seq, dim):
    pos = jnp.arange(seq, dtype=jnp.float32)
    inv_freq = ROPE_THETA ** (-jnp.arange(0, dim, 2, dtype=jnp.float32) / dim)
    ang = pos[:, None] * inv_freq[None, :]
    return jnp.cos(ang), jnp.sin(ang)


def apply_rope(x, cos, sin):
    xf = x.astype(jnp.float32)
    half = xf.shape[-1] // 2
    x1, x2 = xf[..., :half], xf[..., half:]
    out = jnp.concatenate([x1 * cos - x2 * sin, x2 * cos + x1 * sin], axis=-1)
    return out.astype(x.dtype)


def split_heads(t, n_heads, head_dim):
    bsz, seq, _ = t.shape
    return t.reshape(bsz, seq, n_heads, head_dim).transpose(0, 2, 1, 3)


def merge_heads(t):
    bsz, nh, seq, dh = t.shape
    return t.transpose(0, 2, 1, 3).reshape(bsz, seq, nh * dh)


def lambda_init_fn(layer):
    return 0.8 - 0.6 * math.exp(-0.3 * layer)


def moba_attention(q, k, v):
    bsz, nh, seq, dh = q.shape
    nb = -(-seq // MOBA_BLOCK)
    padded = nb * MOBA_BLOCK
    if padded != seq:
        pad = ((0, 0), (0, 0), (0, padded - seq), (0, 0))
        q, k, v = jnp.pad(q, pad), jnp.pad(k, pad), jnp.pad(v, pad)
    topk = min(MOBA_TOPK, nb)
    n_chunks = padded // MOBA_Q_CHUNK
    scale = dh ** -0.5
    kb = k.reshape(bsz, nh, nb, MOBA_BLOCK, dh)
    vb = v.reshape(bsz, nh, nb, MOBA_BLOCK, dh)
    k_mean = jnp.mean(kb.astype(jnp.float32), axis=3)
    q_chunks = jnp.moveaxis(q.reshape(bsz, nh, n_chunks, MOBA_Q_CHUNK, dh), 2, 0)
    b_idx = jnp.arange(bsz)[:, None, None, None]
    h_idx = jnp.arange(nh)[None, :, None, None]
    blk_ids = jnp.arange(nb)
    q_offs = jnp.arange(MOBA_Q_CHUNK)
    k_offs = jnp.arange(MOBA_BLOCK)

    def one_chunk(args):
        qi, ci = args
        q_pos = ci * MOBA_Q_CHUNK + q_offs
        own = (ci * MOBA_Q_CHUNK) // MOBA_BLOCK
        gate = jnp.einsum('bhqd,bhnd->bhqn', qi.astype(jnp.float32), k_mean)
        gate = jnp.where(blk_ids < own, gate, -jnp.inf)
        _, sel = lax.top_k(gate, topk)
        sel_valid = sel < own
        k_sel = kb[b_idx, h_idx, sel]
        v_sel = vb[b_idx, h_idx, sel]
        s_sel = jnp.einsum('bhqd,bhqjkd->bhqjk', qi, k_sel).astype(jnp.float32) * scale
        s_sel = jnp.where(sel_valid[..., None], s_sel, -jnp.inf)
        k_own = lax.dynamic_index_in_dim(kb, own, axis=2, keepdims=False)
        v_own = lax.dynamic_index_in_dim(vb, own, axis=2, keepdims=False)
        s_own = jnp.einsum('bhqd,bhkd->bhqk', qi, k_own).astype(jnp.float32) * scale
        k_pos = own * MOBA_BLOCK + k_offs
        s_own = jnp.where(k_pos[None, :] <= q_pos[:, None], s_own, -jnp.inf)
        logits = jnp.concatenate(
            [s_sel.reshape(bsz, nh, MOBA_Q_CHUNK, topk * MOBA_BLOCK), s_own], axis=-1)
        p = jax.nn.softmax(logits, axis=-1).astype(v.dtype)
        p_sel = p[..., :topk * MOBA_BLOCK].reshape(bsz, nh, MOBA_Q_CHUNK, topk, MOBA_BLOCK)
        p_own = p[..., topk * MOBA_BLOCK:]
        return (jnp.einsum('bhqjk,bhqjkd->bhqd', p_sel, v_sel)
                + jnp.einsum('bhqk,bhkd->bhqd', p_own, v_own))

    out = lax.map(one_chunk, (q_chunks, jnp.arange(n_chunks)))
    out = jnp.moveaxis(out, 0, 2).reshape(bsz, nh, padded, dh)
    return out[:, :, :seq]


def diff_attention(q, k, v, lam):
    bsz, n2, seq, dh = q.shape
    nh = n2 // 2
    n_blocks = seq // DENSE_Q_BLOCK
    scale = dh ** -0.5
    k_pos = jnp.arange(seq)
    q_offs = jnp.arange(DENSE_Q_BLOCK)
    q_blocks = jnp.moveaxis(q.reshape(bsz, n2, n_blocks, DENSE_Q_BLOCK, dh), 2, 0)

    def one_block(args):
        qi, bi = args
        q_pos = bi * DENSE_Q_BLOCK + q_offs
        s = jnp.einsum('bhqd,bhkd->bhqk', qi, k).astype(jnp.float32) * scale
        s = jnp.where(k_pos[None, :] <= q_pos[:, None], s, -jnp.inf)
        p = jax.nn.softmax(s, axis=-1).reshape(bsz, nh, 2, DENSE_Q_BLOCK, seq)
        w = (p[:, :, 0] - lam * p[:, :, 1]).astype(v.dtype)
        return jnp.einsum('bhqk,bhkd->bhqd', w, v)

    out = lax.map(one_block, (q_blocks, jnp.arange(n_blocks)))
    return jnp.moveaxis(out, 0, 2).reshape(bsz, nh, seq, v.shape[-1])


def sliding_window_sink_attention(q, k, v, sinks):
    bsz, n_q, seq, dh = q.shape
    n_kv = k.shape[1]
    grp = n_q // n_kv
    n_blocks = seq // C_WINDOW
    scale = dh ** -0.5
    pad = ((0, 0), (0, 0), (C_WINDOW, 0), (0, 0))
    kp, vp = jnp.pad(k, pad), jnp.pad(v, pad)
    q_blocks = jnp.moveaxis(q.reshape(bsz, n_kv, grp, n_blocks, C_WINDOW, dh), 3, 0)
    sink_logits = sinks.astype(jnp.float32).reshape(1, n_kv, grp, 1, 1)
    q_offs = jnp.arange(C_WINDOW)
    k_offs = jnp.arange(2 * C_WINDOW)

    def one_block(args):
        qi, bi = args
        kw = lax.dynamic_slice_in_dim(kp, bi * C_WINDOW, 2 * C_WINDOW, axis=2)
        vw = lax.dynamic_slice_in_dim(vp, bi * C_WINDOW, 2 * C_WINDOW, axis=2)
        q_pos = bi * C_WINDOW + q_offs
        k_pos = (bi - 1) * C_WINDOW + k_offs
        rel = q_pos[:, None] - k_pos[None, :]
        mask = (rel >= 0) & (rel < C_WINDOW) & (k_pos[None, :] >= 0)
        s = jnp.einsum('bkgqd,bksd->bkgqs', qi, kw).astype(jnp.float32) * scale
        s = jnp.where(mask, s, -jnp.inf)
        sink = jnp.broadcast_to(sink_logits, s.shape[:-1] + (1,))
        p = jax.nn.softmax(jnp.concatenate([s, sink], axis=-1), axis=-1)[..., :-1]
        return jnp.einsum('bkgqs,bksd->bkgqd', p.astype(v.dtype), vw)

    out = lax.map(one_block, (q_blocks, jnp.arange(n_blocks)))
    return jnp.moveaxis(out, 0, 3).reshape(bsz, n_q, seq, dh)


def even_mixer(h, w_in, w_out, lam_vecs, subln_w, lambda_init, cos_r, sin_r):
    proj = h @ w_in
    splits = [A_WIDTH, 2 * A_WIDTH, 3 * A_WIDTH,
              3 * A_WIDTH + B_WIDTH, 3 * A_WIDTH + 2 * B_WIDTH]
    aq, ak, av, bq, bk, bv = jnp.split(proj, splits, axis=-1)
    aq = apply_rope(split_heads(aq, A_HEADS, A_HEAD_DIM), cos_r, sin_r)
    ak = apply_rope(split_heads(ak, A_HEADS, A_HEAD_DIM), cos_r, sin_r)
    av = split_heads(av, A_HEADS, A_HEAD_DIM)
    ya = moba_attention(aq, ak, av)
    bq = apply_rope(split_heads(bq, 2 * B_HEADS, B_HEAD_DIM), cos_r, sin_r)
    bk = apply_rope(split_heads(bk, 2 * B_HEADS, B_HEAD_DIM), cos_r, sin_r)
    bv = split_heads(bv, B_HEADS, 2 * B_HEAD_DIM)
    lv = lam_vecs.astype(jnp.float32)
    lam = (jnp.exp(jnp.sum(lv[0] * lv[1])) - jnp.exp(jnp.sum(lv[2] * lv[3]))
           + lambda_init)
    yb = diff_attention(bq, bk, bv, lam)
    yb = rms_norm(yb, subln_w, DIFF_SUBLN_EPS) * (1.0 - lambda_init)
    y = jnp.concatenate([merge_heads(ya), merge_heads(yb)], axis=-1)
    return y @ w_out


def odd_mixer(h, w_in, b_in, w_out, b_out, sinks, cos_r, sin_r):
    proj = h @ w_in + b_in
    q, k, v = jnp.split(proj, [C_Q_WIDTH, C_Q_WIDTH + C_KV_WIDTH], axis=-1)
    q = apply_rope(split_heads(q, C_Q_HEADS, C_HEAD_DIM), cos_r, sin_r)
    k = apply_rope(split_heads(k, C_KV_HEADS, C_HEAD_DIM), cos_r, sin_r)
    v = split_heads(v, C_KV_HEADS, C_HEAD_DIM)
    y = sliding_window_sink_attention(q, k, v, sinks)
    return merge_heads(y) @ w_out + b_out


def swiglu(h, w_in, w_out):
    gate, up = jnp.split(h @ w_in, 2, axis=-1)
    return (jax.nn.silu(gate) * up) @ w_out


def setup_inputs(seed: int = 0) -> dict:
    key = jax.random.key(seed)
    ks = jax.random.split(key, 17)

    def nrm(k, shape, s):
        return jax.random.normal(k, shape, jnp.float32) * s

    return {
        "x": nrm(ks[0], (BATCH, SEQ, D_MODEL), 1.0),
        "c": nrm(ks[1], (BATCH, D_MODEL), 1.0),
        "ada_w": nrm(ks[2], (DEPTH, D_MODEL, N_MOD * D_MODEL), 0.5 * D_MODEL ** -0.5),
        "ada_b": nrm(ks[3], (DEPTH, N_MOD * D_MODEL), 0.02),
        "ab_w_in": nrm(ks[4], (N_EVEN, D_MODEL, EVEN_IN), D_MODEL ** -0.5),
        "ab_w_out": nrm(ks[5], (N_EVEN, EVEN_OUT, D_MODEL), EVEN_OUT ** -0.5),
        "diff_lambda": nrm(ks[6], (N_EVEN, 4, B_HEAD_DIM), 0.1),
        "diff_subln": 1.0 + nrm(ks[7], (N_EVEN, 2 * B_HEAD_DIM), 0.02),
        "swa_w_in": nrm(ks[8], (N_ODD, D_MODEL, ODD_IN), D_MODEL ** -0.5),
        "swa_b_in": nrm(ks[9], (N_ODD, ODD_IN), 0.02),
        "swa_w_out": nrm(ks[10], (N_ODD, C_Q_WIDTH, D_MODEL), C_Q_WIDTH ** -0.5),
        "swa_b_out": nrm(ks[11], (N_ODD, D_MODEL), 0.02),
        "swa_sinks": nrm(ks[12], (N_ODD, C_Q_HEADS), 0.5),
        "ffn_w_in": nrm(ks[13], (DEPTH, D_MODEL, 2 * FFN_HIDDEN), D_MODEL ** -0.5),
        "ffn_w_out": nrm(ks[14], (DEPTH, FFN_HIDDEN, D_MODEL), FFN_HIDDEN ** -0.5),
        "final_norm": 1.0 + nrm(ks[15], (D_MODEL,), 0.02),
    }


def reference(x, c, ada_w, ada_b, ab_w_in, ab_w_out, diff_lambda, diff_subln,
              swa_w_in, swa_b_in, swa_w_out, swa_b_out, swa_sinks,
              ffn_w_in, ffn_w_out, final_norm):
    seq = x.shape[1]
    cos_ab, sin_ab = rope_tables(seq, A_HEAD_DIM)
    cos_c, sin_c = rope_tables(seq, C_HEAD_DIM)
    cond = jax.nn.silu(c.astype(jnp.float32)).astype(x.dtype)
    for layer in range(DEPTH):
        mod = cond @ ada_w[layer] + ada_b[layer]
        sh1, sc1, g1, sh2, sc2, g2 = jnp.split(mod[:, None, :], N_MOD, axis=-1)
        h = rms_norm(x) * (1.0 + sc1) + sh1
        i = layer // 2
        if layer % 2 == 0:
            y = even_mixer(h, ab_w_in[i], ab_w_out[i], diff_lambda[i], diff_subln[i],
                           lambda_init_fn(layer), cos_ab, sin_ab)
        else:
            y = odd_mixer(h, swa_w_in[i], swa_b_in[i], swa_w_out[i], swa_b_out[i],
                          swa_sinks[i], cos_c, sin_c)
        x = x + g1 * y
        h = rms_norm(x) * (1.0 + sc2) + sh2
        x = x + g2 * swiglu(h, ffn_w_in[layer], ffn_w_out[layer])
    return rms_norm(x, final_norm)
```

```python
import functools
import math

import jax
import jax.numpy as jnp
from jax import lax
from jax.experimental import pallas as pl
from jax.experimental.pallas import tpu as pltpu

F32 = jnp.float32
MXU_DTYPE = jnp.bfloat16

ROPE_THETA = 10000.0
RMS_EPS = 1e-6
DIFF_SUBLN_EPS = 1e-5
A_HEAD_DIM = 128
MOBA_BLOCK = 256
MOBA_BLOCK_LOG2 = 8
MOBA_TOPK = 3
B_HEAD_DIM = 128
C_HEAD_DIM = 64
C_GROUP = 8
C_WINDOW = 128
N_MOD = 6

LANES = 128
MASKED = -1e30
LOG2E = 1.4426950408889634

ATTN_TILE = 512
ROW_TILE = 512
FFN_HIDDEN_TILE = 512
MOD_COL_TILE = 1024
MIB = 2 ** 20

NT_DIMS = (((1,), (1,)), ((), ()))


def _params(semantics, vmem_mib):
    return pltpu.CompilerParams(dimension_semantics=semantics,
                                vmem_limit_bytes=vmem_mib * MIB)


def _col_tile(n):
    for t in (512, 256, 128):
        if n % t == 0:
            return t
    raise ValueError(f"column count {n} is not a multiple of {LANES}")


def _norm_mod(x, sc, sh):
    ms = jnp.mean(x * x, axis=-1, keepdims=True)
    return (x * lax.rsqrt(ms + RMS_EPS)) * (1.0 + sc) + sh


def _mod_kernel(c_ref, w_ref, b_ref, o_ref):
    c = c_ref[...]
    cond = (c * jax.nn.sigmoid(c)).astype(MXU_DTYPE)
    o_ref[...] = jnp.dot(cond, w_ref[...].astype(MXU_DTYPE),
                         preferred_element_type=F32) + b_ref[...]


def _modulation(c, ada_w, ada_b):
    depth, d, n = ada_w.shape
    bsz = c.shape[0]
    rows = -(-bsz // 8) * 8
    c_pad = jnp.pad(c, ((0, rows - bsz), (0, 0)))
    tn = MOD_COL_TILE if n % MOD_COL_TILE == 0 else _col_tile(n)
    out = pl.pallas_call(
        _mod_kernel,
        grid=(depth, n // tn),
        in_specs=[pl.BlockSpec((rows, d), lambda l, j: (0, 0)),
                  pl.BlockSpec((None, d, tn), lambda l, j: (l, 0, j)),
                  pl.BlockSpec((None, 1, tn), lambda l, j: (l, 0, j))],
        out_specs=pl.BlockSpec((None, rows, tn), lambda l, j: (l, 0, j)),
        out_shape=jax.ShapeDtypeStruct((depth, rows, n), F32),
        compiler_params=_params(("parallel", "parallel"), 40),
    )(c_pad, ada_w, ada_b.reshape(depth, 1, n))
    return out[:, :bsz]


def _rope_partner(a, rope_dim):
    if rope_dim == LANES:
        return pltpu.roll(a, LANES // 2, 1)
    lane = lax.broadcasted_iota(jnp.int32, a.shape, 1)
    half = rope_dim // 2
    first_half = (lane & (rope_dim - 1)) < half
    return jnp.where(first_half, pltpu.roll(a, LANES - half, 1), pltpu.roll(a, half, 1))


def _proj_kernel(x_ref, sc_ref, sh_ref, w_ref, *rest, rope_dim, has_bias, transpose_out):
    rest = list(rest)
    b_ref = rest.pop(0) if has_bias else None
    cos_ref, sin_ref = (rest.pop(0), rest.pop(0)) if rope_dim else (None, None)
    o_ref, h_ref = rest

    @pl.when(pl.program_id(2) == 0)
    def _():
        h_ref[...] = _norm_mod(x_ref[...], sc_ref[...], sh_ref[...]).astype(h_ref.dtype)

    acc = jnp.dot(h_ref[...], w_ref[...], preferred_element_type=F32)
    if has_bias:
        acc = acc + b_ref[...]
    if rope_dim:
        cos, sin = cos_ref[...], sin_ref[...]
        for t in range(acc.shape[1] // LANES):
            a = acc[:, t * LANES:(t + 1) * LANES]
            o_ref[:, t * LANES:(t + 1) * LANES] = (
                a * cos + _rope_partner(a, rope_dim) * sin).astype(o_ref.dtype)
    elif transpose_out:
        o_ref[...] = acc.T.astype(o_ref.dtype)
    else:
        o_ref[...] = acc.astype(o_ref.dtype)


def _project(x, sc, sh, w, bias=None, rope=None, rope_dim=0, transpose_out=False):
    bsz, seq, d = x.shape
    n = w.shape[1]
    tm = min(ROW_TILE, seq)
    tn = _col_tile(n)
    in_specs = [pl.BlockSpec((None, tm, d), lambda b, i, j: (b, i, 0)),
                pl.BlockSpec((None, 1, d), lambda b, i, j: (b, 0, 0)),
                pl.BlockSpec((None, 1, d), lambda b, i, j: (b, 0, 0)),
                pl.BlockSpec((d, tn), lambda b, i, j: (0, j))]
    args = [x, sc, sh, w]
    if bias is not None:
        in_specs.append(pl.BlockSpec((1, tn), lambda b, i, j: (0, j)))
        args.append(bias.reshape(1, n).astype(F32))
    if rope_dim:
        in_specs += [pl.BlockSpec((tm, LANES), lambda b, i, j: (i, 0))] * 2
        args += list(rope)
    if transpose_out:
        out_shape = jax.ShapeDtypeStruct((bsz, seq // tm, n, tm), MXU_DTYPE)
        out_spec = pl.BlockSpec((None, None, tn, tm), lambda b, i, j: (b, i, j, 0))
    else:
        out_shape = jax.ShapeDtypeStruct((bsz, seq, n), MXU_DTYPE)
        out_spec = pl.BlockSpec((None, tm, tn), lambda b, i, j: (b, i, j))
    return pl.pallas_call(
        functools.partial(_proj_kernel, rope_dim=rope_dim, has_bias=bias is not None,
                          transpose_out=transpose_out),
        grid=(bsz, seq // tm, n // tn),
        in_specs=in_specs,
        out_specs=out_spec,
        out_shape=out_shape,
        scratch_shapes=[pltpu.VMEM((tm, d), MXU_DTYPE)],
        compiler_params=_params(("parallel", "parallel", "arbitrary"), 40),
    )(*args)


def _flash_step(k_aug, q_aug, vt, m_ref, l_ref, acc_ref, scale2, first):
    s = lax.dot_general(k_aug, q_aug, NT_DIMS, preferred_element_type=F32) * scale2
    if first:
        krow = lax.broadcasted_iota(jnp.int32, s.shape, 0)
        qcol = lax.broadcasted_iota(jnp.int32, s.shape, 1)
        s = jnp.where(krow <= qcol, s, MASKED)
        m_new = jnp.max(s, axis=0, keepdims=True)
        p = jnp.exp2(s - m_new)
        l_ref[...] = jnp.sum(p, axis=0, keepdims=True)
        acc_ref[...] = jnp.dot(vt, p.astype(MXU_DTYPE), preferred_element_type=F32)
    else:
        m_prev = m_ref[...]
        m_new = jnp.maximum(m_prev, jnp.max(s, axis=0, keepdims=True))
        alpha = jnp.exp2(m_prev - m_new)
        p = jnp.exp2(s - m_new)
        l_ref[...] = alpha * l_ref[...] + jnp.sum(p, axis=0, keepdims=True)
        acc_ref[...] = alpha * acc_ref[...] + jnp.dot(
            vt, p.astype(MXU_DTYPE), preferred_element_type=F32)
    m_ref[...] = m_new


def _moba_kernel(q_ref, k_ref, vt_ref, o_ref, kmean_ref, qaug_ref, m_ref, l_ref, acc_ref,
                 *, n_blocks, scale2):
    i = pl.program_id(2)
    tq = q_ref.shape[0]

    @pl.when(i == 0)
    def _():
        kmean_ref[...] = jnp.zeros_like(kmean_ref)

        def mean_body(n, carry):
            kb = k_ref[pl.ds(pl.multiple_of(n * MOBA_BLOCK, MOBA_BLOCK), MOBA_BLOCK), :]
            kmean_ref[pl.ds(n, 1), :] = (
                jnp.sum(kb.astype(F32), axis=0, keepdims=True) * (1.0 / MOBA_BLOCK))
            return carry

        lax.fori_loop(0, n_blocks, mean_body, 0)

    q = q_ref[...]
    km = kmean_ref[...]
    k_hi = km.astype(MXU_DTYPE)
    r1 = km - k_hi.astype(F32)
    k_mid = r1.astype(MXU_DTYPE)
    k_lo = (r1 - k_mid.astype(F32)).astype(MXU_DTYPE)
    gate = (lax.dot_general(q, k_lo, NT_DIMS, preferred_element_type=F32)
            + lax.dot_general(q, k_mid, NT_DIMS, preferred_element_type=F32)
            + lax.dot_general(q, k_hi, NT_DIMS, preferred_element_type=F32))
    col = lax.broadcasted_iota(jnp.int32, gate.shape, 1)
    row = lax.broadcasted_iota(jnp.int32, gate.shape, 0)
    own = jnp.right_shift(i * tq + row, MOBA_BLOCK_LOG2)
    past = col < own
    colf = col.astype(F32)
    g = jnp.where(past, gate, -jnp.inf)
    chosen = jnp.zeros(gate.shape, jnp.bool_)
    for _ in range(min(MOBA_TOPK, n_blocks)):
        best = jnp.max(g, axis=1, keepdims=True)
        first_best = jnp.min(jnp.where(g == best, colf, float(LANES)), axis=1, keepdims=True)
        pick = colf == first_best
        chosen = jnp.logical_or(chosen, pick)
        g = jnp.where(pick, -jnp.inf, g)
    visible = jnp.logical_or(jnp.logical_and(chosen, past), col == own)
    qaug_ref[:, :A_HEAD_DIM] = q
    qaug_ref[:, A_HEAD_DIM:] = jnp.where(visible, 0.0, MASKED).astype(qaug_ref.dtype)

    def step(j, first):
        tk = tq
        kt = k_ref[pl.ds(pl.multiple_of(j * tk, tk), tk), :]
        kblk = jnp.right_shift(j * tk + lax.broadcasted_iota(jnp.int32, (tk, LANES), 0),
                               MOBA_BLOCK_LOG2)
        onehot = jnp.where(kblk == lax.broadcasted_iota(jnp.int32, (tk, LANES), 1), 1.0, 0.0)
        k_aug = jnp.concatenate([kt, onehot.astype(kt.dtype)], axis=1)
        _flash_step(k_aug, qaug_ref[...], vt_ref[j], m_ref, l_ref, acc_ref, scale2, first)

    step(i, True)

    def past_body(j, carry):
        step(j, False)
        return carry

    lax.fori_loop(0, i, past_body, 0)
    o_ref[...] = (acc_ref[...] / l_ref[...]).T.astype(o_ref.dtype)


def _moba_attention(qk, vt, n_heads, q_col, k_col, v_row):
    bsz, seq, _ = qk.shape
    t = min(ATTN_TILE, seq)
    n_tiles = seq // t
    n_blocks = seq // MOBA_BLOCK
    assert seq % t == 0 and t % MOBA_BLOCK == 0 and n_blocks <= LANES
    hd = A_HEAD_DIM
    scale2 = hd ** -0.5 * LOG2E
    return pl.pallas_call(
        functools.partial(_moba_kernel, n_blocks=n_blocks, scale2=scale2),
        grid=(bsz, n_heads, n_tiles),
        in_specs=[pl.BlockSpec((None, t, hd), lambda b, h, i: (b, i, q_col + h)),
                  pl.BlockSpec((None, seq, hd), lambda b, h, i: (b, 0, k_col + h)),
                  pl.BlockSpec((None, n_tiles, hd, t), lambda b, h, i: (b, 0, v_row + h, 0))],
        out_specs=pl.BlockSpec((None, t, hd), lambda b, h, i: (b, i, h)),
        out_shape=jax.ShapeDtypeStruct((bsz, seq, n_heads * hd), MXU_DTYPE),
        scratch_shapes=[pltpu.VMEM((LANES, hd), F32),
                        pltpu.VMEM((t, 2 * hd), MXU_DTYPE),
                        pltpu.VMEM((1, t), F32),
                        pltpu.VMEM((1, t), F32),
                        pltpu.VMEM((hd, t), F32)],
        compiler_params=_params(("parallel", "parallel", "arbitrary"), 48),
    )(qk, qk, vt)


def _diff_kernel(q1_ref, q2_ref, k1_ref, k2_ref, vt_ref, lam_ref, w_ref, o_ref,
                 m1_ref, l1_ref, acc1_ref, m2_ref, l2_ref, acc2_ref, *, scale2, lambda_init):
    i = pl.program_id(2)
    tk = q1_ref.shape[0]

    def step(j, first):
        rows = pl.ds(pl.multiple_of(j * tk, tk), tk)
        vt = vt_ref[j]
        _flash_step(k1_ref[rows, :], q1_ref[...], vt, m1_ref, l1_ref, acc1_ref, scale2, first)
        _flash_step(k2_ref[rows, :], q2_ref[...], vt, m2_ref, l2_ref, acc2_ref, scale2, first)

    step(i, True)

    def past_body(j, carry):
        step(j, False)
        return carry

    lax.fori_loop(0, i, past_body, 0)

    lv = lam_ref[...]
    lam = (jnp.exp(jnp.sum(lv[0:1] * lv[1:2], axis=1, keepdims=True))
           - jnp.exp(jnp.sum(lv[2:3] * lv[3:4], axis=1, keepdims=True)) + lambda_init)
    y = acc1_ref[...] / l1_ref[...] - lam * (acc2_ref[...] / l2_ref[...])
    ms = jnp.mean(y * y, axis=0, keepdims=True)
    y = (y * lax.rsqrt(ms + DIFF_SUBLN_EPS)) * w_ref[...]
    o_ref[...] = (y * (1.0 - lambda_init)).T.astype(o_ref.dtype)


def _diff_attention(qk, vt, lam_vecs, subln_w, lambda_init, n_heads, q_col, k_col, v_row):
    bsz, seq, _ = qk.shape
    t = min(ATTN_TILE, seq)
    n_tiles = seq // t
    hd = B_HEAD_DIM
    scale2 = hd ** -0.5 * LOG2E
    stat = pltpu.VMEM((1, t), F32)
    acc = pltpu.VMEM((2 * hd, t), F32)
    return pl.pallas_call(
        functools.partial(_diff_kernel, scale2=scale2, lambda_init=lambda_init),
        grid=(bsz, n_heads, n_tiles),
        in_specs=[pl.BlockSpec((None, t, hd), lambda b, h, i: (b, i, q_col + 2 * h)),
                  pl.BlockSpec((None, t, hd), lambda b, h, i: (b, i, q_col + 2 * h + 1)),
                  pl.BlockSpec((None, seq, hd), lambda b, h, i: (b, 0, k_col + 2 * h)),
                  pl.BlockSpec((None, seq, hd), lambda b, h, i: (b, 0, k_col + 2 * h + 1)),
                  pl.BlockSpec((None, n_tiles, 2 * hd, t), lambda b, h, i: (b, 0, v_row + h, 0)),
                  pl.BlockSpec((4, hd), lambda b, h, i: (0, 0)),
                  pl.BlockSpec((2 * hd, 1), lambda b, h, i: (0, 0))],
        out_specs=pl.BlockSpec((None, t, 2 * hd), lambda b, h, i: (b, i, h)),
        out_shape=jax.ShapeDtypeStruct((bsz, seq, n_heads * 2 * hd), MXU_DTYPE),
        scratch_shapes=[stat, stat, acc, stat, stat, acc],
        compiler_params=_params(("parallel", "parallel", "arbitrary"), 56),
    )(qk, qk, qk, qk, vt, lam_vecs.astype(F32), subln_w.astype(F32).reshape(2 * hd, 1))


def _swa_kernel(q_ref, kc_ref, kp_ref, vc_ref, vp_ref, sink_ref, o_ref, *, scale):
    i = pl.program_id(1)
    w = C_WINDOW
    hd = C_HEAD_DIM
    kk = jnp.concatenate([kp_ref[...], kc_ref[...]], axis=0)
    vv = jnp.concatenate([vp_ref[...], vc_ref[...]], axis=0)
    lane = lax.broadcasted_iota(jnp.int32, kk.shape, 1)
    lo = jnp.where(lane < hd, 1.0, 0.0).astype(kk.dtype)
    hi = jnp.where(lane >= hd, 1.0, 0.0).astype(kk.dtype)
    k_bd = jnp.concatenate([kk * lo, kk * hi], axis=0)
    v_bd = jnp.concatenate([vv * lo, vv * hi], axis=0)
    r = lax.broadcasted_iota(jnp.int32, (w, 2 * w), 0)
    cc = lax.broadcasted_iota(jnp.int32, (w, 2 * w), 1)
    rel = w + r - cc
    valid = (rel >= 0) & (rel < w) & ((i - 1) * w + cc >= 0)
    out_lane = lax.broadcasted_iota(jnp.int32, (w, LANES), 1)
    for p in range(q_ref.shape[1] // LANES):
        q2 = q_ref[:, p * LANES:(p + 1) * LANES]
        s = lax.dot_general(q2, k_bd, NT_DIMS, preferred_element_type=F32) * scale
        probs, inv = [], []
        for half in range(2):
            sh = jnp.where(valid, s[:, half * 2 * w:(half + 1) * 2 * w], MASKED)
            sink = sink_ref[p:p + 1, half * 2 * w:half * 2 * w + 1]
            m = jnp.maximum(jnp.max(sh, axis=1, keepdims=True), sink)
            e = jnp.exp(sh - m)
            denom = jnp.sum(e, axis=1, keepdims=True) + jnp.exp(sink - m)
            probs.append(e.astype(q2.dtype))
            inv.append(1.0 / denom)
        o = jnp.dot(jnp.concatenate(probs, axis=1), v_bd, preferred_element_type=F32)
        o = o * jnp.where(out_lane < hd, inv[0], inv[1])
        o_ref[:, p * LANES:(p + 1) * LANES] = o.astype(o_ref.dtype)


def _swa_attention(qk, v_dup, sinks, n_kv, k_col):
    bsz, seq, _ = qk.shape
    w = C_WINDOW
    gw = C_GROUP * C_HEAD_DIM
    pairs = C_GROUP // 2
    sink_cols = jnp.repeat(sinks.astype(F32).reshape(n_kv, pairs, 2), 2 * w, axis=-1)
    prev = lambda i: jnp.maximum(i - 1, 0)
    return pl.pallas_call(
        functools.partial(_swa_kernel, scale=C_HEAD_DIM ** -0.5),
        grid=(bsz, seq // w, n_kv),
        in_specs=[pl.BlockSpec((None, w, gw), lambda b, i, g: (b, i, g)),
                  pl.BlockSpec((None, w, LANES), lambda b, i, g: (b, i, k_col + g)),
                  pl.BlockSpec((None, w, LANES), lambda b, i, g: (b, prev(i), k_col + g)),
                  pl.BlockSpec((None, w, LANES), lambda b, i, g: (b, i, g)),
                  pl.BlockSpec((None, w, LANES), lambda b, i, g: (b, prev(i), g)),
                  pl.BlockSpec((None, pairs, 4 * w), lambda b, i, g: (g, 0, 0))],
        out_specs=pl.BlockSpec((None, w, gw), lambda b, i, g: (b, i, g)),
        out_shape=jax.ShapeDtypeStruct((bsz, seq, n_kv * gw), MXU_DTYPE),
        compiler_params=_params(("parallel", "parallel", "parallel"), 32),
    )(qk, qk, qk, v_dup, v_dup, sink_cols)


def _outproj_kernel(*refs, n_parts, has_bias):
    y_refs, w_refs = refs[:n_parts], refs[n_parts:2 * n_parts]
    rest = list(refs[2 * n_parts:])
    b_ref = rest.pop(0) if has_bias else None
    g_ref, x_ref, o_ref = rest
    acc = jnp.dot(y_refs[0][...], w_refs[0][...], preferred_element_type=F32)
    for y_ref, w_ref in zip(y_refs[1:], w_refs[1:]):
        acc = acc + jnp.dot(y_ref[...], w_ref[...], preferred_element_type=F32)
    if has_bias:
        acc = acc + b_ref[...]
    o_ref[...] = x_ref[...] + g_ref[...] * acc


def _out_project(x, gate, ys, ws, bias=None):
    bsz, seq, d = x.shape
    tm = min(ROW_TILE, seq)
    in_specs = [pl.BlockSpec((None, tm, y.shape[-1]), lambda b, i: (b, i, 0)) for y in ys]
    in_specs += [pl.BlockSpec(w.shape, lambda b, i: (0, 0)) for w in ws]
    args = list(ys) + list(ws)
    if bias is not None:
        in_specs.append(pl.BlockSpec((1, d), lambda b, i: (0, 0)))
        args.append(bias.reshape(1, d).astype(F32))
    in_specs += [pl.BlockSpec((None, 1, d), lambda b, i: (b, 0, 0)),
                 pl.BlockSpec((None, tm, d), lambda b, i: (b, i, 0))]
    args += [gate, x]
    return pl.pallas_call(
        functools.partial(_outproj_kernel, n_parts=len(ys), has_bias=bias is not None),
        grid=(bsz, seq // tm),
        in_specs=in_specs,
        out_specs=pl.BlockSpec((None, tm, d), lambda b, i: (b, i, 0)),
        out_shape=jax.ShapeDtypeStruct(x.shape, F32),
        input_output_aliases={len(args) - 1: 0},
        compiler_params=_params(("parallel", "parallel"), 48),
    )(*args)


def _ffn_kernel(x_ref, sc_ref, sh_ref, g_ref, wg_ref, wu_ref, wo_ref, *rest, final):
    rest = list(rest)
    fw_ref = rest.pop(0) if final else None
    o_ref, h_ref, acc_ref = rest
    k = pl.program_id(2)

    @pl.when(k == 0)
    def _():
        h_ref[...] = _norm_mod(x_ref[...], sc_ref[...], sh_ref[...]).astype(h_ref.dtype)
        acc_ref[...] = jnp.zeros_like(acc_ref)

    h = h_ref[...]
    gt = jnp.dot(h, wg_ref[...], preferred_element_type=F32)
    up = jnp.dot(h, wu_ref[...], preferred_element_type=F32)
    act = ((gt * jax.nn.sigmoid(gt)) * up).astype(h.dtype)
    acc_ref[...] += jnp.dot(act, wo_ref[...], preferred_element_type=F32)

    @pl.when(k == pl.num_programs(2) - 1)
    def _():
        xn = x_ref[...] + g_ref[...] * acc_ref[...]
        if final:
            ms = jnp.mean(xn * xn, axis=-1, keepdims=True)
            xn = (xn * lax.rsqrt(ms + RMS_EPS)) * fw_ref[...]
        o_ref[...] = xn


def _ffn(x, sc, sh, gate, w_in, w_out, final_w=None):
    bsz, seq, d = x.shape
    hidden = w_out.shape[0]
    tm = min(ROW_TILE, seq)
    th = FFN_HIDDEN_TILE
    assert hidden % th == 0
    n_h = hidden // th
    mod_spec = pl.BlockSpec((None, 1, d), lambda b, i, k: (b, 0, 0))
    in_specs = [pl.BlockSpec((None, tm, d), lambda b, i, k: (b, i, 0)),
                mod_spec, mod_spec, mod_spec,
                pl.BlockSpec((d, th), lambda b, i, k: (0, k)),
                pl.BlockSpec((d, th), lambda b, i, k: (0, n_h + k)),
                pl.BlockSpec((th, d), lambda b, i, k: (k, 0))]
    args = [x, sc, sh, gate, w_in, w_in, w_out]
    if final_w is not None:
        in_specs.append(pl.BlockSpec((1, d), lambda b, i, k: (0, 0)))
        args.append(final_w.reshape(1, d).astype(F32))
    return pl.pallas_call(
        functools.partial(_ffn_kernel, final=final_w is not None),
        grid=(bsz, seq // tm, n_h),
        in_specs=in_specs,
        out_specs=pl.BlockSpec((None, tm, d), lambda b, i, k: (b, i, 0)),
        out_shape=jax.ShapeDtypeStruct(x.shape, F32),
        scratch_shapes=[pltpu.VMEM((tm, d), MXU_DTYPE), pltpu.VMEM((tm, d), F32)],
        input_output_aliases={0: 0},
        compiler_params=_params(("parallel", "parallel", "arbitrary"), 48),
    )(*args)


def _rope_tables(seq, dim):
    pos = jnp.arange(seq, dtype=F32)
    inv_freq = ROPE_THETA ** (-jnp.arange(0, dim, 2, dtype=F32) / dim)
    ang = pos[:, None] * inv_freq[None, :]
    cos, sin = jnp.cos(ang), jnp.sin(ang)
    reps = LANES // dim
    return (jnp.tile(jnp.concatenate([cos, cos], axis=1), (1, reps)),
            jnp.tile(jnp.concatenate([-sin, sin], axis=1), (1, reps)))


def _lambda_init(layer):
    return 0.8 - 0.6 * math.exp(-0.3 * layer)


def _dup_heads(w, n_heads, head_dim):
    lead = w.shape[:-1]
    w = w.reshape(lead + (n_heads, 1, head_dim))
    return jnp.broadcast_to(w, lead + (n_heads, 2, head_dim)).reshape(lead + (2 * n_heads * head_dim,))


def kernel(x, c, ada_w, ada_b, ab_w_in, ab_w_out, diff_lambda, diff_subln, swa_w_in, swa_b_in,
           swa_w_out, swa_b_out, swa_sinks, ffn_w_in, ffn_w_out, final_norm):
    bsz, seq, d = x.shape
    depth = ada_w.shape[0]
    a_heads, b_heads = d // 256, d // 512
    a_width, b_width = a_heads * A_HEAD_DIM, 2 * b_heads * B_HEAD_DIM
    c_q_heads = d // C_HEAD_DIM
    c_kv_heads = c_q_heads // C_GROUP
    c_q_width, c_kv_width = c_q_heads * C_HEAD_DIM, c_kv_heads * C_HEAD_DIM

    rope_ab = _rope_tables(seq, A_HEAD_DIM)
    rope_c = _rope_tables(seq, C_HEAD_DIM)
    mod = _modulation(c, ada_w, ada_b)

    for layer in range(depth):
        sh1, sc1, g1, sh2, sc2, g2 = [
            mod[layer, :, None, m * d:(m + 1) * d] for m in range(N_MOD)]
        li = layer // 2
        if layer % 2 == 0:
            w_in = ab_w_in[li].astype(MXU_DTYPE)
            aq, ak, av, bq, bk, bv = jnp.split(
                w_in, [a_width, 2 * a_width, 3 * a_width,
                       3 * a_width + b_width, 3 * a_width + 2 * b_width], axis=1)
            qk = _project(x, sc1, sh1, jnp.concatenate([aq, ak, bq, bk], axis=1),
                          rope=rope_ab, rope_dim=A_HEAD_DIM)
            vt = _project(x, sc1, sh1, jnp.concatenate([av, bv], axis=1), transpose_out=True)
            ya = _moba_attention(qk, vt, a_heads, q_col=0, k_col=a_heads, v_row=0)
            yb = _diff_attention(qk, vt, diff_lambda[li], diff_subln[li], _lambda_init(layer),
                                 b_heads, q_col=2 * a_heads, k_col=2 * a_heads + 2 * b_heads,
                                 v_row=a_width // (2 * B_HEAD_DIM))
            w_out = ab_w_out[li].astype(MXU_DTYPE)
            x = _out_project(x, g1, [ya, yb], [w_out[:a_width], w_out[a_width:]])
        else:
            w_in, b_in = swa_w_in[li], swa_b_in[li]
            wq, wk, wv = jnp.split(w_in, [c_q_width, c_q_width + c_kv_width], axis=1)
            bq_, bk_, bv_ = jnp.split(b_in, [c_q_width, c_q_width + c_kv_width])
            w_qk = jnp.concatenate([wq, _dup_heads(wk, c_kv_heads, C_HEAD_DIM)], axis=1)
            b_qk = jnp.concatenate([bq_, _dup_heads(bk_, c_kv_heads, C_HEAD_DIM)])
            qk = _project(x, sc1, sh1, w_qk.astype(MXU_DTYPE), bias=b_qk,
                          rope=rope_c, rope_dim=C_HEAD_DIM)
            v_dup = _project(x, sc1, sh1, _dup_heads(wv, c_kv_heads, C_HEAD_DIM).astype(MXU_DTYPE),
                             bias=_dup_heads(bv_, c_kv_heads, C_HEAD_DIM))
            y = _swa_attention(qk, v_dup, swa_sinks[li], c_kv_heads, k_col=c_q_width // LANES)
            x = _out_project(x, g1, [y], [swa_w_out[li].astype(MXU_DTYPE)], bias=swa_b_out[li])
        x = _ffn(x, sc2, sh2, g2, ffn_w_in[layer].astype(MXU_DTYPE),
                 ffn_w_out[layer].astype(MXU_DTYPE),
                 final_w=final_norm if layer == depth - 1 else None)
    return x
```

```python
import functools
import math

import jax
import jax.numpy as jnp
from jax import lax
from jax.experimental import pallas as pl
from jax.experimental.pallas import tpu as pltpu

F32 = jnp.float32
MXU_DTYPE = jnp.bfloat16

ROPE_THETA = 10000.0
RMS_EPS = 1e-6
DIFF_SUBLN_EPS = 1e-5
A_HEAD_DIM = 128
MOBA_BLOCK = 256
MOBA_BLOCK_LOG2 = 8
MOBA_TOPK = 3
B_HEAD_DIM = 128
C_HEAD_DIM = 64
C_GROUP = 8
C_WINDOW = 128
N_MOD = 6

LANES = 128
MASKED = -1e30
LOG2E = 1.4426950408889634

ATTN_TILE = 512
ROW_TILE = 512
MOD_COL_TILE = 1024
MIB = 2 ** 20

NT_DIMS = (((1,), (1,)), ((), ()))


def _params(semantics, vmem_mib):
    return pltpu.CompilerParams(dimension_semantics=semantics,
                                vmem_limit_bytes=vmem_mib * MIB)


def _col_tile(n):
    for t in (512, 256, 128):
        if n % t == 0:
            return t
    raise ValueError(f"column count {n} is not a multiple of {LANES}")


def _norm_mod(x, sc, sh):
    ms = jnp.mean(x * x, axis=-1, keepdims=True)
    return (x * lax.rsqrt(ms + RMS_EPS)) * (1.0 + sc) + sh


def _mod_kernel(c_ref, w_ref, b_ref, o_ref):
    c = c_ref[...]
    cond = (c * jax.nn.sigmoid(c)).astype(MXU_DTYPE)
    o_ref[...] = jnp.dot(cond, w_ref[...].astype(MXU_DTYPE),
                         preferred_element_type=F32) + b_ref[...]


def _modulation(c, ada_w, ada_b):
    depth, d, n = ada_w.shape
    bsz = c.shape[0]
    rows = -(-bsz // 8) * 8
    c_pad = jnp.pad(c, ((0, rows - bsz), (0, 0)))
    tn = MOD_COL_TILE if n % MOD_COL_TILE == 0 else _col_tile(n)
    out = pl.pallas_call(
        _mod_kernel,
        grid=(depth, n // tn),
        in_specs=[pl.BlockSpec((rows, d), lambda l, j: (0, 0)),
                  pl.BlockSpec((None, d, tn), lambda l, j: (l, 0, j)),
                  pl.BlockSpec((None, 1, tn), lambda l, j: (l, 0, j))],
        out_specs=pl.BlockSpec((None, rows, tn), lambda l, j: (l, 0, j)),
        out_shape=jax.ShapeDtypeStruct((depth, rows, n), F32),
        compiler_params=_params(("parallel", "parallel"), 40),
    )(c_pad, ada_w, ada_b.reshape(depth, 1, n))
    return out[:, :bsz]


def _rope_partner(a, rope_dim):
    if rope_dim == LANES:
        return pltpu.roll(a, LANES // 2, 1)
    lane = lax.broadcasted_iota(jnp.int32, a.shape, 1)
    half = rope_dim // 2
    first_half = (lane & (rope_dim - 1)) < half
    return jnp.where(first_half, pltpu.roll(a, LANES - half, 1), pltpu.roll(a, half, 1))


def _proj_kernel(x_ref, sc_ref, sh_ref, w_ref, *rest, rope_dim, has_bias, transpose_out):
    rest = list(rest)
    b_ref = rest.pop(0) if has_bias else None
    cos_ref, sin_ref = (rest.pop(0), rest.pop(0)) if rope_dim else (None, None)
    o_ref, h_ref = rest

    @pl.when(pl.program_id(2) == 0)
    def _():
        h_ref[...] = _norm_mod(x_ref[...], sc_ref[...], sh_ref[...]).astype(h_ref.dtype)

    acc = jnp.dot(h_ref[...], w_ref[...], preferred_element_type=F32)
    if has_bias:
        acc = acc + b_ref[...]
    if rope_dim:
        cos, sin = cos_ref[...], sin_ref[...]
        for t in range(acc.shape[1] // LANES):
            a = acc[:, t * LANES:(t + 1) * LANES]
            o_ref[:, t * LANES:(t + 1) * LANES] = (
                a * cos + _rope_partner(a, rope_dim) * sin).astype(o_ref.dtype)
    elif transpose_out:
        o_ref[...] = acc.T.astype(o_ref.dtype)
    else:
        o_ref[...] = acc.astype(o_ref.dtype)


def _project(x, sc, sh, w, bias=None, rope=None, rope_dim=0, is_query_col=None,
             transpose_out=False):
    bsz, seq, d = x.shape
    n = w.shape[1]
    tm = min(ROW_TILE, seq)
    tn = _col_tile(n)
    in_specs = [pl.BlockSpec((None, tm, d), lambda b, i, j: (b, i, 0)),
                pl.BlockSpec((None, 1, d), lambda b, i, j: (b, 0, 0)),
                pl.BlockSpec((None, 1, d), lambda b, i, j: (b, 0, 0)),
                pl.BlockSpec((d, tn), lambda b, i, j: (0, j))]
    args = [x, sc, sh, w]
    if bias is not None:
        in_specs.append(pl.BlockSpec((1, tn), lambda b, i, j: (0, j)))
        args.append(bias.reshape(1, n).astype(F32))
    if rope_dim:
        table = lambda b, i, j: (jnp.where(is_query_col(j * tn), 0, 1), i, 0)
        in_specs += [pl.BlockSpec((None, tm, LANES), table)] * 2
        args += list(rope)
    if transpose_out:
        out_shape = jax.ShapeDtypeStruct((bsz, seq // tm, n, tm), MXU_DTYPE)
        out_spec = pl.BlockSpec((None, None, tn, tm), lambda b, i, j: (b, i, j, 0))
    else:
        out_shape = jax.ShapeDtypeStruct((bsz, seq, n), MXU_DTYPE)
        out_spec = pl.BlockSpec((None, tm, tn), lambda b, i, j: (b, i, j))
    return pl.pallas_call(
        functools.partial(_proj_kernel, rope_dim=rope_dim, has_bias=bias is not None,
                          transpose_out=transpose_out),
        grid=(bsz, seq // tm, n // tn),
        in_specs=in_specs,
        out_specs=out_spec,
        out_shape=out_shape,
        scratch_shapes=[pltpu.VMEM((tm, d), MXU_DTYPE)],
        compiler_params=_params(("parallel", "parallel", "arbitrary"), 40),
    )(*args)


N_FLASH_REFS = 10
SLAB_ROWS = 64
ONES_ROWS = 16


def _flash_scratch(t, v_dim):
    stat = pltpu.VMEM((1, t), F32)
    scores = pltpu.VMEM((t, t), F32)
    probs = pltpu.VMEM((t, t), MXU_DTYPE)
    return [stat, pltpu.VMEM((v_dim + ONES_ROWS, t), F32),
            scores, scores, stat, stat, probs, probs, stat, stat]


class _FlashStream:
    def __init__(self, refs):
        self.m, self.acc = refs[:2]
        self.s, self.s_max, self.p, self.alpha = refs[2:4], refs[4:6], refs[6:8], refs[8:10]

    def result(self):
        v_dim = self.acc.shape[0] - ONES_ROWS
        return self.acc[:v_dim, :] / self.acc[v_dim:v_dim + 1, :]


def _col_max(x):
    slab = x[:SLAB_ROWS]
    for r in range(SLAB_ROWS, x.shape[0], SLAB_ROWS):
        slab = jnp.maximum(slab, x[r:r + SLAB_ROWS])
    return jnp.max(slab, axis=0, keepdims=True)


def _causal_flash(i, streams, scores_of, values_of):
    for st in streams:
        st.m[...] = jnp.full_like(st.m, MASKED)
        st.acc[...] = jnp.zeros_like(st.acc)
        st.alpha[1][...] = jnp.ones_like(st.alpha[1])
        st.p[1][...] = jnp.zeros_like(st.p[1])

    def put_scores(st, slot, s):
        st.s[slot][...] = s
        st.s_max[slot][...] = _col_max(s)

    for idx, st in enumerate(streams):
        s = scores_of(idx, i)
        krow = lax.broadcasted_iota(jnp.int32, s.shape, 0)
        qcol = lax.broadcasted_iota(jnp.int32, s.shape, 1)
        put_scores(st, 0, jnp.where(krow <= qcol, s, MASKED))
    n_steps = i + 1
    last_past = jnp.maximum(i - 1, 0)

    def values_step(k, slot):
        tile = jnp.where(k <= 0, i, k - 1)
        for idx, st in enumerate(streams):
            vt = values_of(idx, tile)
            vt = jnp.concatenate([vt, jnp.ones((ONES_ROWS, vt.shape[1]), vt.dtype)], axis=0)
            st.acc[...] = st.alpha[slot][...] * st.acc[...] + jnp.dot(
                vt, st.p[slot][...], preferred_element_type=F32)

    def step(k, cur):
        nxt = 1 - cur
        ahead = jnp.minimum(k, last_past)
        for st in streams:
            m_prev = st.m[...]
            m_new = jnp.maximum(m_prev, st.s_max[cur][...])
            for r in range(0, st.s[cur].shape[0], SLAB_ROWS):
                p = jnp.exp2(st.s[cur][r:r + SLAB_ROWS, :] - m_new)
                st.p[cur][r:r + SLAB_ROWS, :] = p.astype(st.p[cur].dtype)
            st.m[...] = m_new
            st.alpha[cur][...] = jnp.exp2(m_prev - m_new)
        for idx, st in enumerate(streams):
            put_scores(st, nxt, scores_of(idx, ahead))
        values_step(k - 1, nxt)

    def pair_body(kk, carry):
        step(2 * kk, 0)
        step(2 * kk + 1, 1)
        return carry

    lax.fori_loop(0, n_steps // 2, pair_body, 0)
    odd = n_steps % 2 == 1

    @pl.when(odd)
    def _():
        step(n_steps - 1, 0)
        values_step(n_steps - 1, 0)

    @pl.when(jnp.logical_not(odd))
    def _():
        values_step(n_steps - 1, 1)


MOBA_HEADS_PER_STEP = 2


def _moba_kernel(q_ref, k_ref, vt_ref, o_ref, kmean_ref, qaug_ref, *state, n_blocks):
    i = pl.program_id(2)
    tq = q_ref.shape[0]
    tk = tq
    hd = A_HEAD_DIM
    heads = [slice(e * hd, (e + 1) * hd) for e in range(MOBA_HEADS_PER_STEP)]

    @pl.when(i == 0)
    def _():
        kmean_ref[...] = jnp.zeros_like(kmean_ref)

        def mean_body(n, carry):
            kb = k_ref[pl.ds(pl.multiple_of(n * MOBA_BLOCK, MOBA_BLOCK), MOBA_BLOCK), :]
            mean = jnp.sum(kb.astype(F32), axis=0, keepdims=True) * (1.0 / MOBA_BLOCK)
            for e, cols in enumerate(heads):
                kmean_ref[e, pl.ds(n, 1), :] = mean[:, cols]
            return carry

        lax.fori_loop(0, n_blocks, mean_body, 0)

    col = lax.broadcasted_iota(jnp.int32, (tq, LANES), 1)
    row = lax.broadcasted_iota(jnp.int32, (tq, LANES), 0)
    own = jnp.right_shift(i * tq + row, MOBA_BLOCK_LOG2)
    past = col < own
    colf = col.astype(F32)
    for e, cols in enumerate(heads):
        q = q_ref[:, cols]
        km = kmean_ref[e]
        k_hi = km.astype(MXU_DTYPE)
        r1 = km - k_hi.astype(F32)
        k_mid = r1.astype(MXU_DTYPE)
        k_lo = (r1 - k_mid.astype(F32)).astype(MXU_DTYPE)
        gate = (lax.dot_general(q, k_lo, NT_DIMS, preferred_element_type=F32)
                + lax.dot_general(q, k_mid, NT_DIMS, preferred_element_type=F32)
                + lax.dot_general(q, k_hi, NT_DIMS, preferred_element_type=F32))
        g = jnp.where(past, gate, -jnp.inf)
        chosen = jnp.zeros(gate.shape, jnp.bool_)
        for _ in range(min(MOBA_TOPK, n_blocks)):
            best = jnp.max(g, axis=1, keepdims=True)
            first_best = jnp.min(jnp.where(g == best, colf, float(LANES)), axis=1, keepdims=True)
            pick = colf == first_best
            chosen = jnp.logical_or(chosen, pick)
            g = jnp.where(pick, -jnp.inf, g)
        visible = jnp.logical_or(jnp.logical_and(chosen, past), col == own)
        qaug_ref[e, :, :hd] = q
        qaug_ref[e, :, hd:] = jnp.where(visible, 0.0, MASKED).astype(qaug_ref.dtype)

    def scores_of(e, j):
        kt = k_ref[pl.ds(pl.multiple_of(j * tk, tk), tk), heads[e]]
        lane = lax.broadcasted_iota(jnp.int32, (1, LANES), 1)
        blocks_per_tile = tk // MOBA_BLOCK
        onehot = jnp.concatenate(
            [jnp.broadcast_to(jnp.where(lane == j * blocks_per_tile + b, 1.0, 0.0),
                              (MOBA_BLOCK, LANES)) for b in range(blocks_per_tile)], axis=0)
        k_aug = jnp.concatenate([kt, onehot.astype(kt.dtype)], axis=1)
        return lax.dot_general(k_aug, qaug_ref[e], NT_DIMS, preferred_element_type=F32)

    streams = [_FlashStream(state[e * N_FLASH_REFS:(e + 1) * N_FLASH_REFS])
               for e in range(MOBA_HEADS_PER_STEP)]
    _causal_flash(i, streams, scores_of, lambda e, j: vt_ref[j, heads[e], :])
    for e, st in enumerate(streams):
        o_ref[:, heads[e]] = st.result().T.astype(o_ref.dtype)


def _moba_attention(qk, vt, n_heads, q_col, k_col, v_row):
    bsz, seq, _ = qk.shape
    t = min(ATTN_TILE, seq)
    n_tiles = seq // t
    n_blocks = seq // MOBA_BLOCK
    per = MOBA_HEADS_PER_STEP
    assert seq % t == 0 and t % MOBA_BLOCK == 0 and n_blocks <= LANES
    assert n_heads % per == 0 and q_col % per == 0 and k_col % per == 0 and v_row % per == 0
    hd = A_HEAD_DIM
    scratch = [pltpu.VMEM((per, LANES, hd), F32), pltpu.VMEM((per, t, 2 * hd), MXU_DTYPE)]
    for _ in range(per):
        scratch += _flash_scratch(t, hd)
    return pl.pallas_call(
        functools.partial(_moba_kernel, n_blocks=n_blocks),
        grid=(bsz, n_heads // per, n_tiles),
        in_specs=[pl.BlockSpec((None, t, per * hd), lambda b, h, i: (b, i, q_col // per + h)),
                  pl.BlockSpec((None, seq, per * hd), lambda b, h, i: (b, 0, k_col // per + h)),
                  pl.BlockSpec((None, n_tiles, per * hd, t),
                               lambda b, h, i: (b, 0, v_row // per + h, 0))],
        out_specs=pl.BlockSpec((None, t, per * hd), lambda b, h, i: (b, i, h)),
        out_shape=jax.ShapeDtypeStruct((bsz, seq, n_heads * hd), MXU_DTYPE),
        scratch_shapes=scratch,
        compiler_params=_params(("parallel", "parallel", "arbitrary"), 56),
    )(qk, qk, vt)


def _diff_kernel(q1_ref, q2_ref, k1_ref, k2_ref, vt_ref, lam_ref, w_ref, o_ref, *state,
                 lambda_init):
    i = pl.program_id(2)
    tk = q1_ref.shape[0]
    map1 = _FlashStream(state[:N_FLASH_REFS])
    map2 = _FlashStream(state[N_FLASH_REFS:])
    qk_refs = ((q1_ref, k1_ref), (q2_ref, k2_ref))

    def scores_of(idx, j):
        q_ref, k_ref = qk_refs[idx]
        rows = pl.ds(pl.multiple_of(j * tk, tk), tk)
        return lax.dot_general(k_ref[rows, :], q_ref[...], NT_DIMS, preferred_element_type=F32)

    _causal_flash(i, [map1, map2], scores_of, lambda idx, j: vt_ref[j])

    lv = lam_ref[...]
    lam = (jnp.exp(jnp.sum(lv[0:1] * lv[1:2], axis=1, keepdims=True))
           - jnp.exp(jnp.sum(lv[2:3] * lv[3:4], axis=1, keepdims=True)) + lambda_init)
    y = map1.result() - lam * map2.result()
    ms = jnp.mean(y * y, axis=0, keepdims=True)
    y = (y * lax.rsqrt(ms + DIFF_SUBLN_EPS)) * w_ref[...]
    o_ref[...] = (y * (1.0 - lambda_init)).T.astype(o_ref.dtype)


def _diff_attention(qk, vt, lam_vecs, subln_w, lambda_init, n_heads, q_col, k_col, v_row):
    bsz, seq, _ = qk.shape
    t = min(ATTN_TILE, seq)
    n_tiles = seq // t
    hd = B_HEAD_DIM
    return pl.pallas_call(
        functools.partial(_diff_kernel, lambda_init=lambda_init),
        grid=(bsz, n_heads, n_tiles),
        in_specs=[pl.BlockSpec((None, t, hd), lambda b, h, i: (b, i, q_col + 2 * h)),
                  pl.BlockSpec((None, t, hd), lambda b, h, i: (b, i, q_col + 2 * h + 1)),
                  pl.BlockSpec((None, seq, hd), lambda b, h, i: (b, 0, k_col + 2 * h)),
                  pl.BlockSpec((None, seq, hd), lambda b, h, i: (b, 0, k_col + 2 * h + 1)),
                  pl.BlockSpec((None, n_tiles, 2 * hd, t), lambda b, h, i: (b, 0, v_row + h, 0)),
                  pl.BlockSpec((4, hd), lambda b, h, i: (0, 0)),
                  pl.BlockSpec((2 * hd, 1), lambda b, h, i: (0, 0))],
        out_specs=pl.BlockSpec((None, t, 2 * hd), lambda b, h, i: (b, i, h)),
        out_shape=jax.ShapeDtypeStruct((bsz, seq, n_heads * 2 * hd), MXU_DTYPE),
        scratch_shapes=_flash_scratch(t, 2 * hd) + _flash_scratch(t, 2 * hd),
        compiler_params=_params(("parallel", "parallel", "arbitrary"), 56),
    )(qk, qk, qk, qk, vt, lam_vecs.astype(F32), subln_w.astype(F32).reshape(2 * hd, 1))


def _swa_kernel(q_ref, kc_ref, kp_ref, vtc_ref, vtp_ref, sink_ref, o_ref):
    i = pl.program_id(1)
    w = C_WINDOW
    hd = C_HEAD_DIM
    kk = jnp.concatenate([kp_ref[...], kc_ref[...]], axis=0)
    vt = jnp.concatenate([vtp_ref[...], vtc_ref[...]], axis=1)
    lane = lax.broadcasted_iota(jnp.int32, (w, LANES), 1)
    lo = jnp.where(lane < hd, 1.0, 0.0).astype(kk.dtype)
    hi = jnp.where(lane >= hd, 1.0, 0.0).astype(kk.dtype)
    n_heads = q_ref.shape[1] // hd
    q_heads = []
    for p in range(n_heads // 2):
        q2 = q_ref[:, p * LANES:(p + 1) * LANES]
        q_heads += [q2 * lo, q2 * hi]
    q_all = jnp.concatenate(q_heads, axis=0)
    s = lax.dot_general(kk, q_all, NT_DIMS, preferred_element_type=F32)
    krow = lax.broadcasted_iota(jnp.int32, s.shape, 0)
    qcol = lax.broadcasted_iota(jnp.int32, s.shape, 1) & (w - 1)
    rel = w + qcol - krow
    s = jnp.where((rel >= 0) & (rel < w) & ((i - 1) * w + krow >= 0), s, MASKED)
    sink = sink_ref[...] * LOG2E
    m = jnp.maximum(_col_max(s), sink)
    e_sum, e_slabs = None, []
    for r in range(0, s.shape[0], SLAB_ROWS):
        e = jnp.exp2(s[r:r + SLAB_ROWS] - m)
        e_slabs.append(e.astype(vt.dtype))
        e_sum = e if e_sum is None else e_sum + e
    denom = jnp.sum(e_sum, axis=0, keepdims=True) + jnp.exp2(sink - m)
    o = jnp.dot(vt, jnp.concatenate(e_slabs, axis=0),
                preferred_element_type=F32) / denom
    o = jnp.concatenate([o[:, h * w:(h + 1) * w] for h in range(n_heads)], axis=0)
    o_ref[...] = o.T.astype(o_ref.dtype)


def _swa_attention(qk, vt, sinks, n_kv, k_col):
    bsz, seq, _ = qk.shape
    w = C_WINDOW
    gw = C_GROUP * C_HEAD_DIM
    per_tile = vt.shape[-1] // w
    sink_cols = jnp.repeat(sinks.astype(F32).reshape(n_kv, 1, C_GROUP), w, axis=-1)
    prev = lambda i: jnp.maximum(i - 1, 0)
    vt_spec = lambda at: pl.BlockSpec(
        (None, None, C_HEAD_DIM, w),
        lambda b, i, g: (b, at(i) // per_tile, g, at(i) % per_tile))
    return pl.pallas_call(
        _swa_kernel,
        grid=(bsz, seq // w, n_kv),
        in_specs=[pl.BlockSpec((None, w, gw), lambda b, i, g: (b, i, g)),
                  pl.BlockSpec((None, w, LANES), lambda b, i, g: (b, i, k_col + g)),
                  pl.BlockSpec((None, w, LANES), lambda b, i, g: (b, prev(i), k_col + g)),
                  vt_spec(lambda i: i),
                  vt_spec(prev),
                  pl.BlockSpec((None, 1, C_GROUP * w), lambda b, i, g: (g, 0, 0))],
        out_specs=pl.BlockSpec((None, w, gw), lambda b, i, g: (b, i, g)),
        out_shape=jax.ShapeDtypeStruct((bsz, seq, n_kv * gw), MXU_DTYPE),
        compiler_params=_params(("parallel", "parallel", "parallel"), 32),
    )(qk, qk, qk, vt, vt, sink_cols)


def _outproj_kernel(*refs, n_parts, has_bias):
    y_refs, w_refs = refs[:n_parts], refs[n_parts:2 * n_parts]
    rest = list(refs[2 * n_parts:])
    b_ref = rest.pop(0) if has_bias else None
    g_ref, x_ref, o_ref = rest
    acc = jnp.dot(y_refs[0][...], w_refs[0][...], preferred_element_type=F32)
    for y_ref, w_ref in zip(y_refs[1:], w_refs[1:]):
        acc = acc + jnp.dot(y_ref[...], w_ref[...], preferred_element_type=F32)
    if has_bias:
        acc = acc + b_ref[...]
    o_ref[...] = x_ref[...] + g_ref[...] * acc


def _out_project(x, gate, ys, ws, bias=None):
    bsz, seq, d = x.shape
    tm = min(ROW_TILE, seq)
    in_specs = [pl.BlockSpec((None, tm, y.shape[-1]), lambda b, i: (b, i, 0)) for y in ys]
    in_specs += [pl.BlockSpec(w.shape, lambda b, i: (0, 0)) for w in ws]
    args = list(ys) + list(ws)
    if bias is not None:
        in_specs.append(pl.BlockSpec((1, d), lambda b, i: (0, 0)))
        args.append(bias.reshape(1, d).astype(F32))
    in_specs += [pl.BlockSpec((None, 1, d), lambda b, i: (b, 0, 0)),
                 pl.BlockSpec((None, tm, d), lambda b, i: (b, i, 0))]
    args += [gate, x]
    return pl.pallas_call(
        functools.partial(_outproj_kernel, n_parts=len(ys), has_bias=bias is not None),
        grid=(bsz, seq // tm),
        in_specs=in_specs,
        out_specs=pl.BlockSpec((None, tm, d), lambda b, i: (b, i, 0)),
        out_shape=jax.ShapeDtypeStruct(x.shape, F32),
        input_output_aliases={len(args) - 1: 0},
        compiler_params=_params(("parallel", "parallel"), 48),
    )(*args)


def _ffn_kernel(x_ref, sc_ref, sh_ref, g_ref, wg_ref, wu_ref, wo_ref, *rest, final):
    rest = list(rest)
    fw_ref = rest.pop(0) if final else None
    o_ref, h_ref, acc_ref = rest
    k = pl.program_id(2)

    @pl.when(k == 0)
    def _():
        h_ref[...] = _norm_mod(x_ref[...], sc_ref[...], sh_ref[...]).astype(h_ref.dtype)
        acc_ref[...] = jnp.zeros_like(acc_ref)

    h = h_ref[...]
    gt = jnp.dot(h, wg_ref[...], preferred_element_type=F32)
    up = jnp.dot(h, wu_ref[...], preferred_element_type=F32)
    act = ((gt * jax.nn.sigmoid(gt)) * up).astype(h.dtype)
    acc_ref[...] += jnp.dot(act, wo_ref[...], preferred_element_type=F32)

    @pl.when(k == pl.num_programs(2) - 1)
    def _():
        xn = x_ref[...] + g_ref[...] * acc_ref[...]
        if final:
            ms = jnp.mean(xn * xn, axis=-1, keepdims=True)
            xn = (xn * lax.rsqrt(ms + RMS_EPS)) * fw_ref[...]
        o_ref[...] = xn


def _ffn(x, sc, sh, gate, w_in, w_out, final_w=None):
    bsz, seq, d = x.shape
    hidden = w_out.shape[0]
    tm = min(ROW_TILE, seq)
    th = _col_tile(hidden)
    n_h = hidden // th
    mod_spec = pl.BlockSpec((None, 1, d), lambda b, i, k: (b, 0, 0))
    in_specs = [pl.BlockSpec((None, tm, d), lambda b, i, k: (b, i, 0)),
                mod_spec, mod_spec, mod_spec,
                pl.BlockSpec((d, th), lambda b, i, k: (0, k)),
                pl.BlockSpec((d, th), lambda b, i, k: (0, n_h + k)),
                pl.BlockSpec((th, d), lambda b, i, k: (k, 0))]
    args = [x, sc, sh, gate, w_in, w_in, w_out]
    if final_w is not None:
        in_specs.append(pl.BlockSpec((1, d), lambda b, i, k: (0, 0)))
        args.append(final_w.reshape(1, d).astype(F32))
    return pl.pallas_call(
        functools.partial(_ffn_kernel, final=final_w is not None),
        grid=(bsz, seq // tm, n_h),
        in_specs=in_specs,
        out_specs=pl.BlockSpec((None, tm, d), lambda b, i, k: (b, i, 0)),
        out_shape=jax.ShapeDtypeStruct(x.shape, F32),
        scratch_shapes=[pltpu.VMEM((tm, d), MXU_DTYPE), pltpu.VMEM((tm, d), F32)],
        input_output_aliases={0: 0},
        compiler_params=_params(("parallel", "parallel", "arbitrary"), 48),
    )(*args)


def _rope_tables(seq, dim, query_scale):
    pos = jnp.arange(seq, dtype=F32)
    inv_freq = ROPE_THETA ** (-jnp.arange(0, dim, 2, dtype=F32) / dim)
    ang = pos[:, None] * inv_freq[None, :]
    cos, sin = jnp.cos(ang), jnp.sin(ang)
    reps = LANES // dim
    cos = jnp.tile(jnp.concatenate([cos, cos], axis=1), (1, reps))
    sin = jnp.tile(jnp.concatenate([-sin, sin], axis=1), (1, reps))
    return jnp.stack([cos * query_scale, cos]), jnp.stack([sin * query_scale, sin])


def _lambda_init(layer):
    return 0.8 - 0.6 * math.exp(-0.3 * layer)


def _dup_heads(w, n_heads, head_dim):
    lead = w.shape[:-1]
    w = w.reshape(lead + (n_heads, 1, head_dim))
    return jnp.broadcast_to(w, lead + (n_heads, 2, head_dim)).reshape(lead + (2 * n_heads * head_dim,))


def kernel(x, c, ada_w, ada_b, ab_w_in, ab_w_out, diff_lambda, diff_subln, swa_w_in, swa_b_in,
           swa_w_out, swa_b_out, swa_sinks, ffn_w_in, ffn_w_out, final_norm):
    bsz, seq, d = x.shape
    depth = ada_w.shape[0]
    a_heads, b_heads = d // 256, d // 512
    a_width, b_width = a_heads * A_HEAD_DIM, 2 * b_heads * B_HEAD_DIM
    assert a_width == b_width
    c_q_heads = d // C_HEAD_DIM
    c_kv_heads = c_q_heads // C_GROUP
    c_q_width, c_kv_width = c_q_heads * C_HEAD_DIM, c_kv_heads * C_HEAD_DIM

    rope_ab = _rope_tables(seq, A_HEAD_DIM, A_HEAD_DIM ** -0.5 * LOG2E)
    rope_c = _rope_tables(seq, C_HEAD_DIM, C_HEAD_DIM ** -0.5 * LOG2E)
    mod = _modulation(c, ada_w, ada_b)

    for layer in range(depth):
        sh1, sc1, g1, sh2, sc2, g2 = [
            mod[layer, :, None, m * d:(m + 1) * d] for m in range(N_MOD)]
        li = layer // 2
        if layer % 2 == 0:
            w_in = ab_w_in[li].astype(MXU_DTYPE)
            aq, ak, av, bq, bk, bv = jnp.split(
                w_in, [a_width, 2 * a_width, 3 * a_width,
                       3 * a_width + b_width, 3 * a_width + 2 * b_width], axis=1)
            qk = _project(x, sc1, sh1, jnp.concatenate([aq, ak, bq, bk], axis=1),
                          rope=rope_ab, rope_dim=A_HEAD_DIM,
                          is_query_col=lambda col: (col // a_width) % 2 == 0)
            vt = _project(x, sc1, sh1, jnp.concatenate([av, bv], axis=1), transpose_out=True)
            ya = _moba_attention(qk, vt, a_heads, q_col=0, k_col=a_heads, v_row=0)
            yb = _diff_attention(qk, vt, diff_lambda[li], diff_subln[li], _lambda_init(layer),
                                 b_heads, q_col=2 * a_heads, k_col=2 * a_heads + 2 * b_heads,
                                 v_row=a_width // (2 * B_HEAD_DIM))
            w_out = ab_w_out[li].astype(MXU_DTYPE)
            x = _out_project(x, g1, [ya, yb], [w_out[:a_width], w_out[a_width:]])
        else:
            w_in, b_in = swa_w_in[li], swa_b_in[li]
            wq, wk, wv = jnp.split(w_in, [c_q_width, c_q_width + c_kv_width], axis=1)
            bq_, bk_, bv_ = jnp.split(b_in, [c_q_width, c_q_width + c_kv_width])
            w_qk = jnp.concatenate([wq, _dup_heads(wk, c_kv_heads, C_HEAD_DIM)], axis=1)
            b_qk = jnp.concatenate([bq_, _dup_heads(bk_, c_kv_heads, C_HEAD_DIM)])
            qk = _project(x, sc1, sh1, w_qk.astype(MXU_DTYPE), bias=b_qk,
                          rope=rope_c, rope_dim=C_HEAD_DIM,
                          is_query_col=lambda col: col < c_q_width)
            vt = _project(x, sc1, sh1, wv.astype(MXU_DTYPE), bias=bv_, transpose_out=True)
            y = _swa_attention(qk, vt, swa_sinks[li], c_kv_heads, k_col=c_q_width // LANES)
            x = _out_project(x, g1, [y], [swa_w_out[li].astype(MXU_DTYPE)], bias=swa_b_out[li])
        x = _ffn(x, sc2, sh2, g2, ffn_w_in[layer].astype(MXU_DTYPE),
                 ffn_w_out[layer].astype(MXU_DTYPE),
                 final_w=final_norm if layer == depth - 1 else None)
    return x
```

```python
import functools
import math

import jax
import jax.numpy as jnp
from jax import lax
from jax.experimental import pallas as pl
from jax.experimental.pallas import tpu as pltpu

F32 = jnp.float32
MXU_DTYPE = jnp.bfloat16

ROPE_THETA = 10000.0
RMS_EPS = 1e-6
DIFF_SUBLN_EPS = 1e-5
A_HEAD_DIM = 128
MOBA_BLOCK = 256
MOBA_BLOCK_LOG2 = 8
MOBA_TOPK = 3
B_HEAD_DIM = 128
C_HEAD_DIM = 64
C_GROUP = 8
C_WINDOW = 128
N_MOD = 6

LANES = 128
MASKED = -1e30
LOG2E = 1.4426950408889634

ATTN_TILE = 512
ROW_TILE = 512
MOD_COL_TILE = 1024
MIB = 2 ** 20

NT_DIMS = (((1,), (1,)), ((), ()))


def _params(semantics, vmem_mib):
    return pltpu.CompilerParams(dimension_semantics=semantics,
                                vmem_limit_bytes=vmem_mib * MIB)


def _col_tile(n):
    for t in (512, 256, 128):
        if n % t == 0:
            return t
    raise ValueError(f"column count {n} is not a multiple of {LANES}")


def _norm_mod(x, sc, sh):
    ms = jnp.mean(x * x, axis=-1, keepdims=True)
    return (x * lax.rsqrt(ms + RMS_EPS)) * (1.0 + sc) + sh


def _mod_kernel(c_ref, w_ref, b_ref, o_ref):
    c = c_ref[...]
    cond = (c * jax.nn.sigmoid(c)).astype(MXU_DTYPE)
    o_ref[...] = jnp.dot(cond, w_ref[...].astype(MXU_DTYPE),
                         preferred_element_type=F32) + b_ref[...]


def _modulation(c, ada_w, ada_b):
    depth, d, n = ada_w.shape
    bsz = c.shape[0]
    rows = -(-bsz // 8) * 8
    c_pad = jnp.pad(c, ((0, rows - bsz), (0, 0)))
    tn = MOD_COL_TILE if n % MOD_COL_TILE == 0 else _col_tile(n)
    out = pl.pallas_call(
        _mod_kernel,
        grid=(depth, n // tn),
        in_specs=[pl.BlockSpec((rows, d), lambda l, j: (0, 0)),
                  pl.BlockSpec((None, d, tn), lambda l, j: (l, 0, j)),
                  pl.BlockSpec((None, 1, tn), lambda l, j: (l, 0, j))],
        out_specs=pl.BlockSpec((None, rows, tn), lambda l, j: (l, 0, j)),
        out_shape=jax.ShapeDtypeStruct((depth, rows, n), F32),
        compiler_params=_params(("parallel", "parallel"), 40),
    )(c_pad, ada_w, ada_b.reshape(depth, 1, n))
    return out[:, :bsz]


def _rope_partner(a, rope_dim):
    if rope_dim == LANES:
        return pltpu.roll(a, LANES // 2, 1)
    lane = lax.broadcasted_iota(jnp.int32, a.shape, 1)
    half = rope_dim // 2
    first_half = (lane & (rope_dim - 1)) < half
    return jnp.where(first_half, pltpu.roll(a, LANES - half, 1), pltpu.roll(a, half, 1))


def _proj_kernel(x_ref, sc_ref, sh_ref, w_ref, *rest, rope_dim, has_bias, transpose_out, n_t):
    rest = list(rest)
    b_ref = rest.pop(0) if has_bias else None
    cos_ref, sin_ref = (rest.pop(0), rest.pop(0)) if rope_dim else (None, None)
    o_ref, h_ref, raw0_ref, raw1_ref = rest
    raws = (raw0_ref, raw1_ref)
    j = pl.program_id(2)

    def multiply(dst_ref):
        acc = jnp.dot(h_ref[...], w_ref[...], preferred_element_type=F32)
        dst_ref[...] = acc + b_ref[...] if has_bias else acc

    def finish(src_ref):
        if rope_dim:
            cos, sin = cos_ref[...], sin_ref[...]
            for t in range(src_ref.shape[1] // LANES):
                a = src_ref[:, t * LANES:(t + 1) * LANES]
                o_ref[:, t * LANES:(t + 1) * LANES] = (
                    a * cos + _rope_partner(a, rope_dim) * sin).astype(o_ref.dtype)
        elif transpose_out:
            o_ref[...] = src_ref[...].T.astype(o_ref.dtype)
        else:
            o_ref[...] = src_ref[...].astype(o_ref.dtype)

    @pl.when(j == 0)
    def _():
        h_ref[...] = _norm_mod(x_ref[...], sc_ref[...], sh_ref[...]).astype(h_ref.dtype)
        multiply(raws[0])

    for parity in (0, 1):
        @pl.when((j > 0) & (j < n_t) & (j % 2 == parity))
        def _():
            multiply(raws[parity])
            finish(raws[1 - parity])

    @pl.when(j == n_t)
    def _():
        finish(raws[(n_t - 1) % 2])


def _project(x, sc, sh, w, bias=None, rope=None, rope_dim=0, is_query_col=None,
             transpose_out=False):
    bsz, seq, d = x.shape
    n = w.shape[1]
    tm = min(ROW_TILE, seq)
    tn = _col_tile(n)
    n_t = n // tn
    multiplied = lambda j: jnp.minimum(j, n_t - 1)
    finished = lambda j: jnp.maximum(j - 1, 0)
    in_specs = [pl.BlockSpec((None, tm, d), lambda b, i, j: (b, i, 0)),
                pl.BlockSpec((None, 1, d), lambda b, i, j: (b, 0, 0)),
                pl.BlockSpec((None, 1, d), lambda b, i, j: (b, 0, 0)),
                pl.BlockSpec((d, tn), lambda b, i, j: (0, multiplied(j)))]
    args = [x, sc, sh, w]
    if bias is not None:
        in_specs.append(pl.BlockSpec((1, tn), lambda b, i, j: (0, multiplied(j))))
        args.append(bias.reshape(1, n).astype(F32))
    if rope_dim:
        table = lambda b, i, j: (jnp.where(is_query_col(finished(j) * tn), 0, 1), i, 0)
        in_specs += [pl.BlockSpec((None, tm, LANES), table)] * 2
        args += list(rope)
    if transpose_out:
        out_shape = jax.ShapeDtypeStruct((bsz, seq // tm, n, tm), MXU_DTYPE)
        out_spec = pl.BlockSpec((None, None, tn, tm), lambda b, i, j: (b, i, finished(j), 0))
    else:
        out_shape = jax.ShapeDtypeStruct((bsz, seq, n), MXU_DTYPE)
        out_spec = pl.BlockSpec((None, tm, tn), lambda b, i, j: (b, i, finished(j)))
    return pl.pallas_call(
        functools.partial(_proj_kernel, rope_dim=rope_dim, has_bias=bias is not None,
                          transpose_out=transpose_out, n_t=n_t),
        grid=(bsz, seq // tm, n_t + 1),
        in_specs=in_specs,
        out_specs=out_spec,
        out_shape=out_shape,
        scratch_shapes=[pltpu.VMEM((tm, d), MXU_DTYPE),
                        pltpu.VMEM((tm, tn), F32), pltpu.VMEM((tm, tn), F32)],
        compiler_params=_params(("parallel", "parallel", "arbitrary"), 40),
    )(*args)


N_FLASH_REFS = 10
SLAB_ROWS = 64
ONES_ROWS = 16


def _flash_scratch(t, v_dim):
    stat = pltpu.VMEM((1, t), F32)
    scores = pltpu.VMEM((t, t), F32)
    probs = pltpu.VMEM((t, t), MXU_DTYPE)
    return [stat, pltpu.VMEM((v_dim + ONES_ROWS, t), F32),
            scores, scores, stat, stat, probs, probs, stat, stat]


class _FlashStream:
    def __init__(self, refs):
        self.m, self.acc = refs[:2]
        self.s, self.s_max, self.p, self.alpha = refs[2:4], refs[4:6], refs[6:8], refs[8:10]

    def result(self):
        v_dim = self.acc.shape[0] - ONES_ROWS
        return self.acc[:v_dim, :] / self.acc[v_dim:v_dim + 1, :]


def _col_max(x):
    slab = x[:SLAB_ROWS]
    for r in range(SLAB_ROWS, x.shape[0], SLAB_ROWS):
        slab = jnp.maximum(slab, x[r:r + SLAB_ROWS])
    return jnp.max(slab, axis=0, keepdims=True)


def _causal_flash(i, streams, scores_of, values_of):
    for st in streams:
        st.m[...] = jnp.full_like(st.m, MASKED)
        st.acc[...] = jnp.zeros_like(st.acc)
        st.alpha[1][...] = jnp.ones_like(st.alpha[1])
        st.p[1][...] = jnp.zeros_like(st.p[1])

    def put_scores(st, slot, s):
        st.s[slot][...] = s
        st.s_max[slot][...] = _col_max(s)

    for idx, st in enumerate(streams):
        s = scores_of(idx, i)
        krow = lax.broadcasted_iota(jnp.int32, s.shape, 0)
        qcol = lax.broadcasted_iota(jnp.int32, s.shape, 1)
        put_scores(st, 0, jnp.where(krow <= qcol, s, MASKED))
    n_steps = i + 1
    last_past = jnp.maximum(i - 1, 0)

    def values_step(k, slot):
        tile = jnp.where(k <= 0, i, k - 1)
        for idx, st in enumerate(streams):
            vt = values_of(idx, tile)
            vt = jnp.concatenate([vt, jnp.ones((ONES_ROWS, vt.shape[1]), vt.dtype)], axis=0)
            st.acc[...] = st.alpha[slot][...] * st.acc[...] + jnp.dot(
                vt, st.p[slot][...], preferred_element_type=F32)

    def step(k, cur):
        nxt = 1 - cur
        ahead = jnp.minimum(k, last_past)
        for st in streams:
            m_prev = st.m[...]
            m_new = jnp.maximum(m_prev, st.s_max[cur][...])
            for r in range(0, st.s[cur].shape[0], SLAB_ROWS):
                p = jnp.exp2(st.s[cur][r:r + SLAB_ROWS, :] - m_new)
                st.p[cur][r:r + SLAB_ROWS, :] = p.astype(st.p[cur].dtype)
            st.m[...] = m_new
            st.alpha[cur][...] = jnp.exp2(m_prev - m_new)
        for idx, st in enumerate(streams):
            put_scores(st, nxt, scores_of(idx, ahead))
        values_step(k - 1, nxt)

    def pair_body(kk, carry):
        step(2 * kk, 0)
        step(2 * kk + 1, 1)
        return carry

    lax.fori_loop(0, n_steps // 2, pair_body, 0)
    odd = n_steps % 2 == 1

    @pl.when(odd)
    def _():
        step(n_steps - 1, 0)
        values_step(n_steps - 1, 0)

    @pl.when(jnp.logical_not(odd))
    def _():
        values_step(n_steps - 1, 1)


MOBA_HEADS_PER_STEP = 2


def _moba_kernel(q_ref, k_ref, vt_ref, o_ref, kmean_ref, qaug_ref, *state, n_blocks):
    i = pl.program_id(2)
    tq = q_ref.shape[0]
    tk = tq
    hd = A_HEAD_DIM
    heads = [slice(e * hd, (e + 1) * hd) for e in range(MOBA_HEADS_PER_STEP)]

    @pl.when(i == 0)
    def _():
        kmean_ref[...] = jnp.zeros_like(kmean_ref)

        def mean_body(n, carry):
            kb = k_ref[pl.ds(pl.multiple_of(n * MOBA_BLOCK, MOBA_BLOCK), MOBA_BLOCK), :]
            mean = jnp.sum(kb.astype(F32), axis=0, keepdims=True) * (1.0 / MOBA_BLOCK)
            for e, cols in enumerate(heads):
                kmean_ref[e, pl.ds(n, 1), :] = mean[:, cols]
            return carry

        lax.fori_loop(0, n_blocks, mean_body, 0)

    col = lax.broadcasted_iota(jnp.int32, (tq, LANES), 1)
    row = lax.broadcasted_iota(jnp.int32, (tq, LANES), 0)
    own = jnp.right_shift(i * tq + row, MOBA_BLOCK_LOG2)
    past = col < own
    colf = col.astype(F32)
    for e, cols in enumerate(heads):
        q = q_ref[:, cols]
        km = kmean_ref[e]
        k_hi = km.astype(MXU_DTYPE)
        r1 = km - k_hi.astype(F32)
        k_mid = r1.astype(MXU_DTYPE)
        k_lo = (r1 - k_mid.astype(F32)).astype(MXU_DTYPE)
        gate = (lax.dot_general(q, k_lo, NT_DIMS, preferred_element_type=F32)
                + lax.dot_general(q, k_mid, NT_DIMS, preferred_element_type=F32)
                + lax.dot_general(q, k_hi, NT_DIMS, preferred_element_type=F32))
        g = jnp.where(past, gate, -jnp.inf)
        chosen = jnp.zeros(gate.shape, jnp.bool_)
        for _ in range(min(MOBA_TOPK, n_blocks)):
            best = jnp.max(g, axis=1, keepdims=True)
            first_best = jnp.min(jnp.where(g == best, colf, float(LANES)), axis=1, keepdims=True)
            pick = colf == first_best
            chosen = jnp.logical_or(chosen, pick)
            g = jnp.where(pick, -jnp.inf, g)
        visible = jnp.logical_or(jnp.logical_and(chosen, past), col == own)
        qaug_ref[e, :, :hd] = q
        qaug_ref[e, :, hd:] = jnp.where(visible, 0.0, MASKED).astype(qaug_ref.dtype)

    def scores_of(e, j):
        kt = k_ref[pl.ds(pl.multiple_of(j * tk, tk), tk), heads[e]]
        lane = lax.broadcasted_iota(jnp.int32, (1, LANES), 1)
        blocks_per_tile = tk // MOBA_BLOCK
        onehot = jnp.concatenate(
            [jnp.broadcast_to(jnp.where(lane == j * blocks_per_tile + b, 1.0, 0.0),
                              (MOBA_BLOCK, LANES)) for b in range(blocks_per_tile)], axis=0)
        k_aug = jnp.concatenate([kt, onehot.astype(kt.dtype)], axis=1)
        return lax.dot_general(k_aug, qaug_ref[e], NT_DIMS, preferred_element_type=F32)

    streams = [_FlashStream(state[e * N_FLASH_REFS:(e + 1) * N_FLASH_REFS])
               for e in range(MOBA_HEADS_PER_STEP)]
    _causal_flash(i, streams, scores_of, lambda e, j: vt_ref[j, heads[e], :])
    for e, st in enumerate(streams):
        o_ref[:, heads[e]] = st.result().T.astype(o_ref.dtype)


def _moba_attention(qk, vt, n_heads, q_col, k_col, v_row):
    bsz, seq, _ = qk.shape
    t = min(ATTN_TILE, seq)
    n_tiles = seq // t
    n_blocks = seq // MOBA_BLOCK
    per = MOBA_HEADS_PER_STEP
    assert seq % t == 0 and t % MOBA_BLOCK == 0 and n_blocks <= LANES
    assert n_heads % per == 0 and q_col % per == 0 and k_col % per == 0 and v_row % per == 0
    hd = A_HEAD_DIM
    scratch = [pltpu.VMEM((per, LANES, hd), F32), pltpu.VMEM((per, t, 2 * hd), MXU_DTYPE)]
    for _ in range(per):
        scratch += _flash_scratch(t, hd)
    return pl.pallas_call(
        functools.partial(_moba_kernel, n_blocks=n_blocks),
        grid=(bsz, n_heads // per, n_tiles),
        in_specs=[pl.BlockSpec((None, t, per * hd), lambda b, h, i: (b, i, q_col // per + h)),
                  pl.BlockSpec((None, seq, per * hd), lambda b, h, i: (b, 0, k_col // per + h)),
                  pl.BlockSpec((None, n_tiles, per * hd, t),
                               lambda b, h, i: (b, 0, v_row // per + h, 0))],
        out_specs=pl.BlockSpec((None, t, per * hd), lambda b, h, i: (b, i, h)),
        out_shape=jax.ShapeDtypeStruct((bsz, seq, n_heads * hd), MXU_DTYPE),
        scratch_shapes=scratch,
        compiler_params=_params(("parallel", "parallel", "arbitrary"), 56),
    )(qk, qk, vt)


def _diff_kernel(q1_ref, q2_ref, k1_ref, k2_ref, vt_ref, lam_ref, w_ref, o_ref, *state,
                 lambda_init):
    i = pl.program_id(2)
    tk = q1_ref.shape[0]
    map1 = _FlashStream(state[:N_FLASH_REFS])
    map2 = _FlashStream(state[N_FLASH_REFS:])
    qk_refs = ((q1_ref, k1_ref), (q2_ref, k2_ref))

    def scores_of(idx, j):
        q_ref, k_ref = qk_refs[idx]
        rows = pl.ds(pl.multiple_of(j * tk, tk), tk)
        return lax.dot_general(k_ref[rows, :], q_ref[...], NT_DIMS, preferred_element_type=F32)

    _causal_flash(i, [map1, map2], scores_of, lambda idx, j: vt_ref[j])

    lv = lam_ref[...]
    lam = (jnp.exp(jnp.sum(lv[0:1] * lv[1:2], axis=1, keepdims=True))
           - jnp.exp(jnp.sum(lv[2:3] * lv[3:4], axis=1, keepdims=True)) + lambda_init)
    y = map1.result() - lam * map2.result()
    ms = jnp.mean(y * y, axis=0, keepdims=True)
    y = (y * lax.rsqrt(ms + DIFF_SUBLN_EPS)) * w_ref[...]
    o_ref[...] = (y * (1.0 - lambda_init)).T.astype(o_ref.dtype)


def _diff_attention(qk, vt, lam_vecs, subln_w, lambda_init, n_heads, q_col, k_col, v_row):
    bsz, seq, _ = qk.shape
    t = min(ATTN_TILE, seq)
    n_tiles = seq // t
    hd = B_HEAD_DIM
    return pl.pallas_call(
        functools.partial(_diff_kernel, lambda_init=lambda_init),
        grid=(bsz, n_heads, n_tiles),
        in_specs=[pl.BlockSpec((None, t, hd), lambda b, h, i: (b, i, q_col + 2 * h)),
                  pl.BlockSpec((None, t, hd), lambda b, h, i: (b, i, q_col + 2 * h + 1)),
                  pl.BlockSpec((None, seq, hd), lambda b, h, i: (b, 0, k_col + 2 * h)),
                  pl.BlockSpec((None, seq, hd), lambda b, h, i: (b, 0, k_col + 2 * h + 1)),
                  pl.BlockSpec((None, n_tiles, 2 * hd, t), lambda b, h, i: (b, 0, v_row + h, 0)),
                  pl.BlockSpec((4, hd), lambda b, h, i: (0, 0)),
                  pl.BlockSpec((2 * hd, 1), lambda b, h, i: (0, 0))],
        out_specs=pl.BlockSpec((None, t, 2 * hd), lambda b, h, i: (b, i, h)),
        out_shape=jax.ShapeDtypeStruct((bsz, seq, n_heads * 2 * hd), MXU_DTYPE),
        scratch_shapes=_flash_scratch(t, 2 * hd) + _flash_scratch(t, 2 * hd),
        compiler_params=_params(("parallel", "parallel", "arbitrary"), 56),
    )(qk, qk, qk, qk, vt, lam_vecs.astype(F32), subln_w.astype(F32).reshape(2 * hd, 1))


def _swa_kernel(q_ref, kc_ref, kp_ref, vtc_ref, vtp_ref, sink_ref, o_ref):
    i = pl.program_id(1)
    w = C_WINDOW
    hd = C_HEAD_DIM
    gw = C_GROUP * hd
    lane = lax.broadcasted_iota(jnp.int32, (w, LANES), 1)
    lo = jnp.where(lane < hd, 1.0, 0.0).astype(q_ref.dtype)
    hi = jnp.where(lane >= hd, 1.0, 0.0).astype(q_ref.dtype)
    krow = lax.broadcasted_iota(jnp.int32, (2 * w, C_GROUP * w), 0)
    qcol = lax.broadcasted_iota(jnp.int32, (2 * w, C_GROUP * w), 1) & (w - 1)
    rel = w + qcol - krow
    valid = (rel >= 0) & (rel < w) & ((i - 1) * w + krow >= 0)
    ones = jnp.ones((ONES_ROWS, 2 * w), vtc_ref.dtype)
    for g in range(q_ref.shape[1] // gw):
        k_cols = slice(g * LANES, (g + 1) * LANES)
        v_rows = slice(g * hd, (g + 1) * hd)
        kk = jnp.concatenate([kp_ref[:, k_cols], kc_ref[:, k_cols]], axis=0)
        vt = jnp.concatenate([vtp_ref[v_rows, :], vtc_ref[v_rows, :]], axis=1)
        q_heads = []
        for p in range(C_GROUP // 2):
            q2 = q_ref[:, g * gw + p * LANES:g * gw + (p + 1) * LANES]
            q_heads += [q2 * lo, q2 * hi]
        q_all = jnp.concatenate(q_heads, axis=0)
        s = lax.dot_general(kk, q_all, NT_DIMS, preferred_element_type=F32)
        s = jnp.where(valid, s, MASKED)
        sink = sink_ref[g] * LOG2E
        m = jnp.maximum(_col_max(s), sink)
        e = jnp.concatenate([jnp.exp2(s[r:r + SLAB_ROWS] - m).astype(vt.dtype)
                             for r in range(0, 2 * w, SLAB_ROWS)], axis=0)
        o = jnp.dot(jnp.concatenate([vt, ones], axis=0), e, preferred_element_type=F32)
        o = o[:hd] / (o[hd:hd + 1] + jnp.exp2(sink - m))
        o = jnp.concatenate([o[:, h * w:(h + 1) * w] for h in range(C_GROUP)], axis=0)
        o_ref[:, g * gw:(g + 1) * gw] = o.T.astype(o_ref.dtype)


def _swa_attention(qk, vt, sinks, n_kv):
    bsz, seq, _ = qk.shape
    w = C_WINDOW
    q_width = n_kv * C_GROUP * C_HEAD_DIM
    k_width = n_kv * LANES
    assert q_width % k_width == 0
    per_tile = vt.shape[-1] // w
    sink_cols = jnp.repeat(sinks.astype(F32).reshape(n_kv, 1, C_GROUP), w, axis=-1)
    prev = lambda i: jnp.maximum(i - 1, 0)
    vt_spec = lambda at: pl.BlockSpec(
        (None, None, n_kv * C_HEAD_DIM, w),
        lambda b, i: (b, at(i) // per_tile, 0, at(i) % per_tile))
    return pl.pallas_call(
        _swa_kernel,
        grid=(bsz, seq // w),
        in_specs=[pl.BlockSpec((None, w, q_width), lambda b, i: (b, i, 0)),
                  pl.BlockSpec((None, w, k_width), lambda b, i: (b, i, q_width // k_width)),
                  pl.BlockSpec((None, w, k_width), lambda b, i: (b, prev(i), q_width // k_width)),
                  vt_spec(lambda i: i),
                  vt_spec(prev),
                  pl.BlockSpec((n_kv, 1, C_GROUP * w), lambda b, i: (0, 0, 0))],
        out_specs=pl.BlockSpec((None, w, q_width), lambda b, i: (b, i, 0)),
        out_shape=jax.ShapeDtypeStruct((bsz, seq, q_width), MXU_DTYPE),
        compiler_params=_params(("parallel", "parallel"), 32),
    )(qk, qk, qk, vt, vt, sink_cols)


def _outproj_kernel(*refs, n_parts, has_bias):
    y_refs, w_refs = refs[:n_parts], refs[n_parts:2 * n_parts]
    rest = list(refs[2 * n_parts:])
    b_ref = rest.pop(0) if has_bias else None
    g_ref, x_ref, o_ref = rest
    acc = jnp.dot(y_refs[0][...], w_refs[0][...], preferred_element_type=F32)
    for y_ref, w_ref in zip(y_refs[1:], w_refs[1:]):
        acc = acc + jnp.dot(y_ref[...], w_ref[...], preferred_element_type=F32)
    if has_bias:
        acc = acc + b_ref[...]
    o_ref[...] = x_ref[...] + g_ref[...] * acc


def _out_project(x, gate, ys, ws, bias=None, in_place=True):
    bsz, seq, d = x.shape
    tm = min(ROW_TILE, seq)
    in_specs = [pl.BlockSpec((None, tm, y.shape[-1]), lambda b, i: (b, i, 0)) for y in ys]
    in_specs += [pl.BlockSpec(w.shape, lambda b, i: (0, 0)) for w in ws]
    args = list(ys) + list(ws)
    if bias is not None:
        in_specs.append(pl.BlockSpec((1, d), lambda b, i: (0, 0)))
        args.append(bias.reshape(1, d).astype(F32))
    in_specs += [pl.BlockSpec((None, 1, d), lambda b, i: (b, 0, 0)),
                 pl.BlockSpec((None, tm, d), lambda b, i: (b, i, 0))]
    args += [gate, x]
    return pl.pallas_call(
        functools.partial(_outproj_kernel, n_parts=len(ys), has_bias=bias is not None),
        grid=(bsz, seq // tm),
        in_specs=in_specs,
        out_specs=pl.BlockSpec((None, tm, d), lambda b, i: (b, i, 0)),
        out_shape=jax.ShapeDtypeStruct(x.shape, F32),
        input_output_aliases={len(args) - 1: 0} if in_place else {},
        compiler_params=_params(("parallel", "parallel"), 48),
    )(*args)


def _ffn_kernel(x_ref, sc_ref, sh_ref, g_ref, wg_ref, wu_ref, wo_ref, *rest, final, n_h):
    rest = list(rest)
    fw_ref = rest.pop(0) if final else None
    o_ref, h_ref, acc_ref, act0_ref, act1_ref = rest
    acts = (act0_ref, act1_ref)
    k = pl.program_id(2)

    def hidden(dst_ref):
        h = h_ref[...]
        gt = jnp.dot(h, wg_ref[...], preferred_element_type=F32)
        up = jnp.dot(h, wu_ref[...], preferred_element_type=F32)
        dst_ref[...] = ((gt * jax.nn.sigmoid(gt)) * up).astype(dst_ref.dtype)

    def project(src_ref):
        return jnp.dot(src_ref[...], wo_ref[...], preferred_element_type=F32)

    @pl.when(k == 0)
    def _():
        h_ref[...] = _norm_mod(x_ref[...], sc_ref[...], sh_ref[...]).astype(h_ref.dtype)
        acc_ref[...] = jnp.zeros_like(acc_ref)
        hidden(acts[0])

    for parity in (0, 1):
        @pl.when((k > 0) & (k < n_h) & (k % 2 == parity))
        def _():
            acc_ref[...] += project(acts[1 - parity])
            hidden(acts[parity])

    @pl.when(k == n_h)
    def _():
        xn = x_ref[...] + g_ref[...] * (acc_ref[...] + project(acts[(n_h - 1) % 2]))
        if final:
            ms = jnp.mean(xn * xn, axis=-1, keepdims=True)
            xn = (xn * lax.rsqrt(ms + RMS_EPS)) * fw_ref[...]
        o_ref[...] = xn


def _ffn(x, sc, sh, gate, w_in, w_out, final_w=None):
    bsz, seq, d = x.shape
    hidden = w_out.shape[0]
    tm = min(ROW_TILE, seq)
    th = _col_tile(hidden)
    n_h = hidden // th
    mod_spec = pl.BlockSpec((None, 1, d), lambda b, i, k: (b, 0, 0))
    in_specs = [pl.BlockSpec((None, tm, d), lambda b, i, k: (b, i, 0)),
                mod_spec, mod_spec, mod_spec,
                pl.BlockSpec((d, th), lambda b, i, k: (0, jnp.minimum(k, n_h - 1))),
                pl.BlockSpec((d, th), lambda b, i, k: (0, n_h + jnp.minimum(k, n_h - 1))),
                pl.BlockSpec((th, d), lambda b, i, k: (jnp.maximum(k - 1, 0), 0))]
    args = [x, sc, sh, gate, w_in, w_in, w_out]
    if final_w is not None:
        in_specs.append(pl.BlockSpec((1, d), lambda b, i, k: (0, 0)))
        args.append(final_w.reshape(1, d).astype(F32))
    return pl.pallas_call(
        functools.partial(_ffn_kernel, final=final_w is not None, n_h=n_h),
        grid=(bsz, seq // tm, n_h + 1),
        in_specs=in_specs,
        out_specs=pl.BlockSpec((None, tm, d), lambda b, i, k: (b, i, 0)),
        out_shape=jax.ShapeDtypeStruct(x.shape, F32),
        scratch_shapes=[pltpu.VMEM((tm, d), MXU_DTYPE), pltpu.VMEM((tm, d), F32),
                        pltpu.VMEM((tm, th), MXU_DTYPE), pltpu.VMEM((tm, th), MXU_DTYPE)],
        input_output_aliases={0: 0},
        compiler_params=_params(("parallel", "parallel", "arbitrary"), 48),
    )(*args)


def _rope_tables(seq, dim, query_scale):
    pos = jnp.arange(seq, dtype=F32)
    inv_freq = ROPE_THETA ** (-jnp.arange(0, dim, 2, dtype=F32) / dim)
    ang = pos[:, None] * inv_freq[None, :]
    cos, sin = jnp.cos(ang), jnp.sin(ang)
    reps = LANES // dim
    cos = jnp.tile(jnp.concatenate([cos, cos], axis=1), (1, reps))
    sin = jnp.tile(jnp.concatenate([-sin, sin], axis=1), (1, reps))
    return jnp.stack([cos * query_scale, cos]), jnp.stack([sin * query_scale, sin])


def _lambda_init(layer):
    return 0.8 - 0.6 * math.exp(-0.3 * layer)


def _dup_heads(w, n_heads, head_dim):
    lead = w.shape[:-1]
    w = w.reshape(lead + (n_heads, 1, head_dim))
    return jnp.broadcast_to(w, lead + (n_heads, 2, head_dim)).reshape(lead + (2 * n_heads * head_dim,))


def kernel(x, c, ada_w, ada_b, ab_w_in, ab_w_out, diff_lambda, diff_subln, swa_w_in, swa_b_in,
           swa_w_out, swa_b_out, swa_sinks, ffn_w_in, ffn_w_out, final_norm):
    bsz, seq, d = x.shape
    depth = ada_w.shape[0]
    a_heads, b_heads = d // 256, d // 512
    a_width, b_width = a_heads * A_HEAD_DIM, 2 * b_heads * B_HEAD_DIM
    assert a_width == b_width
    c_q_heads = d // C_HEAD_DIM
    c_kv_heads = c_q_heads // C_GROUP
    c_q_width, c_kv_width = c_q_heads * C_HEAD_DIM, c_kv_heads * C_HEAD_DIM

    rope_ab = _rope_tables(seq, A_HEAD_DIM, A_HEAD_DIM ** -0.5 * LOG2E)
    rope_c = _rope_tables(seq, C_HEAD_DIM, C_HEAD_DIM ** -0.5 * LOG2E)
    mod = _modulation(c, ada_w, ada_b)

    for layer in range(depth):
        sh1, sc1, g1, sh2, sc2, g2 = [
            mod[layer, :, None, m * d:(m + 1) * d] for m in range(N_MOD)]
        li = layer // 2
        if layer % 2 == 0:
            w_in = ab_w_in[li].astype(MXU_DTYPE)
            aq, ak, av, bq, bk, bv = jnp.split(
                w_in, [a_width, 2 * a_width, 3 * a_width,
                       3 * a_width + b_width, 3 * a_width + 2 * b_width], axis=1)
            qk = _project(x, sc1, sh1, jnp.concatenate([aq, ak, bq, bk], axis=1),
                          rope=rope_ab, rope_dim=A_HEAD_DIM,
                          is_query_col=lambda col: (col // a_width) % 2 == 0)
            vt = _project(x, sc1, sh1, jnp.concatenate([av, bv], axis=1), transpose_out=True)
            ya = _moba_attention(qk, vt, a_heads, q_col=0, k_col=a_heads, v_row=0)
            yb = _diff_attention(qk, vt, diff_lambda[li], diff_subln[li], _lambda_init(layer),
                                 b_heads, q_col=2 * a_heads, k_col=2 * a_heads + 2 * b_heads,
                                 v_row=a_width // (2 * B_HEAD_DIM))
            w_out = ab_w_out[li].astype(MXU_DTYPE)
            x = _out_project(x, g1, [ya, yb], [w_out[:a_width], w_out[a_width:]],
                             in_place=layer > 0)
        else:
            w_in, b_in = swa_w_in[li], swa_b_in[li]
            wq, wk, wv = jnp.split(w_in, [c_q_width, c_q_width + c_kv_width], axis=1)
            bq_, bk_, bv_ = jnp.split(b_in, [c_q_width, c_q_width + c_kv_width])
            w_qk = jnp.concatenate([wq, _dup_heads(wk, c_kv_heads, C_HEAD_DIM)], axis=1)
            b_qk = jnp.concatenate([bq_, _dup_heads(bk_, c_kv_heads, C_HEAD_DIM)])
            qk = _project(x, sc1, sh1, w_qk.astype(MXU_DTYPE), bias=b_qk,
                          rope=rope_c, rope_dim=C_HEAD_DIM,
                          is_query_col=lambda col: col < c_q_width)
            vt = _project(x, sc1, sh1, wv.astype(MXU_DTYPE), bias=bv_, transpose_out=True)
            y = _swa_attention(qk, vt, swa_sinks[li], c_kv_heads)
            x = _out_project(x, g1, [y], [swa_w_out[li].astype(MXU_DTYPE)], bias=swa_b_out[li])
        x = _ffn(x, sc2, sh2, g2, ffn_w_in[layer].astype(MXU_DTYPE),
                 ffn_w_out[layer].astype(MXU_DTYPE),
                 final_w=final_norm if layer == depth - 1 else None)
    return x
```

```python
import functools
import math

import jax
import jax.numpy as jnp
from jax import lax
from jax.experimental import pallas as pl
from jax.experimental.pallas import tpu as pltpu

F32 = jnp.float32
MXU_DTYPE = jnp.bfloat16

ROPE_THETA = 10000.0
RMS_EPS = 1e-6
DIFF_SUBLN_EPS = 1e-5
A_HEAD_DIM = 128
MOBA_BLOCK = 256
MOBA_BLOCK_LOG2 = 8
MOBA_TOPK = 3
B_HEAD_DIM = 128
C_HEAD_DIM = 64
C_GROUP = 8
C_WINDOW = 128
N_MOD = 6

LANES = 128
MASKED = -1e30
LOG2E = 1.4426950408889634

ATTN_TILE = 512
ROW_TILE = 512
MOD_COL_TILE = 1024
MIB = 2 ** 20

NT_DIMS = (((1,), (1,)), ((), ()))


def _params(semantics, vmem_mib):
    return pltpu.CompilerParams(dimension_semantics=semantics,
                                vmem_limit_bytes=vmem_mib * MIB)


def _col_tile(n):
    for t in (512, 256, 128):
        if n % t == 0:
            return t
    raise ValueError(f"column count {n} is not a multiple of {LANES}")


def _column_tiles(w, tn):
    k, n = w.shape
    return w.reshape(k, n // tn, tn).transpose(1, 0, 2)


def _norm_mod(x, sc, sh):
    ms = jnp.mean(x * x, axis=-1, keepdims=True)
    return (x * lax.rsqrt(ms + RMS_EPS)) * (1.0 + sc) + sh


def _mod_kernel(c_ref, w_ref, b_ref, o_ref):
    c = c_ref[...]
    cond = (c * jax.nn.sigmoid(c)).astype(MXU_DTYPE)
    o_ref[...] = jnp.dot(cond, w_ref[...].astype(MXU_DTYPE),
                         preferred_element_type=F32) + b_ref[...]


def _modulation(c, ada_w, ada_b):
    depth, d, n = ada_w.shape
    bsz = c.shape[0]
    rows = -(-bsz // 8) * 8
    c_pad = jnp.pad(c, ((0, rows - bsz), (0, 0)))
    tn = MOD_COL_TILE if n % MOD_COL_TILE == 0 else _col_tile(n)
    out = pl.pallas_call(
        _mod_kernel,
        grid=(depth, n // tn),
        in_specs=[pl.BlockSpec((rows, d), lambda l, j: (0, 0)),
                  pl.BlockSpec((None, d, tn), lambda l, j: (l, 0, j)),
                  pl.BlockSpec((None, 1, tn), lambda l, j: (l, 0, j))],
        out_specs=pl.BlockSpec((None, rows, tn), lambda l, j: (l, 0, j)),
        out_shape=jax.ShapeDtypeStruct((depth, rows, n), F32),
        compiler_params=_params(("parallel", "parallel"), 40),
    )(c_pad, ada_w, ada_b.reshape(depth, 1, n))
    return out[:, :bsz]


def _rope_partner(a, rope_dim):
    if rope_dim == LANES:
        return pltpu.roll(a, LANES // 2, 1)
    lane = lax.broadcasted_iota(jnp.int32, a.shape, 1)
    half = rope_dim // 2
    first_half = (lane & (rope_dim - 1)) < half
    return jnp.where(first_half, pltpu.roll(a, LANES - half, 1), pltpu.roll(a, half, 1))


def _proj_kernel(x_ref, sc_ref, sh_ref, w_ref, *rest, rope_dim, has_bias, transpose_out, n_t):
    rest = list(rest)
    b_ref = rest.pop(0) if has_bias else None
    cos_ref, sin_ref = (rest.pop(0), rest.pop(0)) if rope_dim else (None, None)
    o_ref, h_ref, raw0_ref, raw1_ref = rest
    raws = (raw0_ref, raw1_ref)
    j = pl.program_id(2)

    def multiply(dst_ref):
        acc = jnp.dot(h_ref[...], w_ref[...], preferred_element_type=F32)
        dst_ref[...] = acc + b_ref[...] if has_bias else acc

    def finish(src_ref):
        if rope_dim:
            cos, sin = cos_ref[...], sin_ref[...]
            for t in range(src_ref.shape[1] // LANES):
                a = src_ref[:, t * LANES:(t + 1) * LANES]
                o_ref[:, t * LANES:(t + 1) * LANES] = (
                    a * cos + _rope_partner(a, rope_dim) * sin).astype(o_ref.dtype)
        elif transpose_out:
            o_ref[...] = src_ref[...].T.astype(o_ref.dtype)
        else:
            o_ref[...] = src_ref[...].astype(o_ref.dtype)

    @pl.when(j == 0)
    def _():
        h_ref[...] = _norm_mod(x_ref[...], sc_ref[...], sh_ref[...]).astype(h_ref.dtype)
        multiply(raws[0])

    for parity in (0, 1):
        @pl.when((j > 0) & (j < n_t) & (j % 2 == parity))
        def _():
            multiply(raws[parity])
            finish(raws[1 - parity])

    @pl.when(j == n_t)
    def _():
        finish(raws[(n_t - 1) % 2])


def _project(x, sc, sh, w, bias=None, rope=None, rope_dim=0, is_query_col=None,
             transpose_out=False):
    bsz, seq, d = x.shape
    n = w.shape[1]
    tm = min(ROW_TILE, seq)
    tn = _col_tile(n)
    n_t = n // tn
    multiplied = lambda j: jnp.minimum(j, n_t - 1)
    finished = lambda j: jnp.maximum(j - 1, 0)
    in_specs = [pl.BlockSpec((None, tm, d), lambda b, i, j: (b, i, 0)),
                pl.BlockSpec((None, 1, d), lambda b, i, j: (b, 0, 0)),
                pl.BlockSpec((None, 1, d), lambda b, i, j: (b, 0, 0)),
                pl.BlockSpec((None, d, tn), lambda b, i, j: (multiplied(j), 0, 0))]
    args = [x, sc, sh, _column_tiles(w, tn)]
    if bias is not None:
        in_specs.append(pl.BlockSpec((1, tn), lambda b, i, j: (0, multiplied(j))))
        args.append(bias.reshape(1, n).astype(F32))
    if rope_dim:
        table = lambda b, i, j: (jnp.where(is_query_col(finished(j) * tn), 0, 1), i, 0)
        in_specs += [pl.BlockSpec((None, tm, LANES), table)] * 2
        args += list(rope)
    if transpose_out:
        out_shape = jax.ShapeDtypeStruct((bsz, seq // tm, n, tm), MXU_DTYPE)
        out_spec = pl.BlockSpec((None, None, tn, tm), lambda b, i, j: (b, i, finished(j), 0))
    else:
        out_shape = jax.ShapeDtypeStruct((bsz, seq, n), MXU_DTYPE)
        out_spec = pl.BlockSpec((None, tm, tn), lambda b, i, j: (b, i, finished(j)))
    return pl.pallas_call(
        functools.partial(_proj_kernel, rope_dim=rope_dim, has_bias=bias is not None,
                          transpose_out=transpose_out, n_t=n_t),
        grid=(bsz, seq // tm, n_t + 1),
        in_specs=in_specs,
        out_specs=out_spec,
        out_shape=out_shape,
        scratch_shapes=[pltpu.VMEM((tm, d), MXU_DTYPE),
                        pltpu.VMEM((tm, tn), F32), pltpu.VMEM((tm, tn), F32)],
        compiler_params=_params(("parallel", "parallel", "arbitrary"), 40),
    )(*args)


N_FLASH_REFS = 10
SLAB_ROWS = 64
ONES_ROWS = 16


def _flash_scratch(t, v_dim):
    stat = pltpu.VMEM((1, t), F32)
    scores = pltpu.VMEM((t, t), F32)
    probs = pltpu.VMEM((t, t), MXU_DTYPE)
    return [stat, pltpu.VMEM((v_dim + ONES_ROWS, t), F32),
            scores, scores, stat, stat, probs, probs, stat, stat]


class _FlashStream:
    def __init__(self, refs):
        self.m, self.acc = refs[:2]
        self.s, self.s_max, self.p, self.alpha = refs[2:4], refs[4:6], refs[6:8], refs[8:10]

    def result(self):
        v_dim = self.acc.shape[0] - ONES_ROWS
        return self.acc[:v_dim, :] / self.acc[v_dim:v_dim + 1, :]


def _col_max(x):
    slab = x[:SLAB_ROWS]
    for r in range(SLAB_ROWS, x.shape[0], SLAB_ROWS):
        slab = jnp.maximum(slab, x[r:r + SLAB_ROWS])
    return jnp.max(slab, axis=0, keepdims=True)


def _causal_flash(i, streams, scores_of, values_of):
    for st in streams:
        st.m[...] = jnp.full_like(st.m, MASKED)
        st.acc[...] = jnp.zeros_like(st.acc)
        st.alpha[1][...] = jnp.ones_like(st.alpha[1])
        st.p[1][...] = jnp.zeros_like(st.p[1])

    def put_scores(st, slot, s):
        st.s[slot][...] = s
        st.s_max[slot][...] = _col_max(s)

    for idx, st in enumerate(streams):
        s = scores_of(idx, i)
        krow = lax.broadcasted_iota(jnp.int32, s.shape, 0)
        qcol = lax.broadcasted_iota(jnp.int32, s.shape, 1)
        put_scores(st, 0, jnp.where(krow <= qcol, s, MASKED))
    n_steps = i + 1
    last_past = jnp.maximum(i - 1, 0)

    def values_step(k, slot):
        tile = jnp.where(k <= 0, i, k - 1)
        for idx, st in enumerate(streams):
            vt = values_of(idx, tile)
            vt = jnp.concatenate([vt, jnp.ones((ONES_ROWS, vt.shape[1]), vt.dtype)], axis=0)
            st.acc[...] = st.alpha[slot][...] * st.acc[...] + jnp.dot(
                vt, st.p[slot][...], preferred_element_type=F32)

    def step(k, cur):
        nxt = 1 - cur
        ahead = jnp.minimum(k, last_past)
        for st in streams:
            m_prev = st.m[...]
            m_new = jnp.maximum(m_prev, st.s_max[cur][...])
            for r in range(0, st.s[cur].shape[0], SLAB_ROWS):
                p = jnp.exp2(st.s[cur][r:r + SLAB_ROWS, :] - m_new)
                st.p[cur][r:r + SLAB_ROWS, :] = p.astype(st.p[cur].dtype)
            st.m[...] = m_new
            st.alpha[cur][...] = jnp.exp2(m_prev - m_new)
        for idx, st in enumerate(streams):
            put_scores(st, nxt, scores_of(idx, ahead))
        values_step(k - 1, nxt)

    def pair_body(kk, carry):
        step(2 * kk, 0)
        step(2 * kk + 1, 1)
        return carry

    lax.fori_loop(0, n_steps // 2, pair_body, 0)
    odd = n_steps % 2 == 1

    @pl.when(odd)
    def _():
        step(n_steps - 1, 0)
        values_step(n_steps - 1, 0)

    @pl.when(jnp.logical_not(odd))
    def _():
        values_step(n_steps - 1, 1)


MOBA_HEADS_PER_STEP = 2


def _moba_kernel(q_ref, k_ref, vt_ref, o_ref, kmean_ref, qaug_ref, *state, n_blocks):
    i = pl.program_id(2)
    tq = q_ref.shape[0]
    tk = tq
    hd = A_HEAD_DIM
    heads = [slice(e * hd, (e + 1) * hd) for e in range(MOBA_HEADS_PER_STEP)]

    @pl.when(i == 0)
    def _():
        kmean_ref[...] = jnp.zeros_like(kmean_ref)

        def mean_body(n, carry):
            kb = k_ref[pl.ds(pl.multiple_of(n * MOBA_BLOCK, MOBA_BLOCK), MOBA_BLOCK), :]
            mean = jnp.sum(kb.astype(F32), axis=0, keepdims=True) * (1.0 / MOBA_BLOCK)
            for e, cols in enumerate(heads):
                kmean_ref[e, pl.ds(n, 1), :] = mean[:, cols]
            return carry

        lax.fori_loop(0, n_blocks, mean_body, 0)

    col = lax.broadcasted_iota(jnp.int32, (tq, LANES), 1)
    row = lax.broadcasted_iota(jnp.int32, (tq, LANES), 0)
    own = jnp.right_shift(i * tq + row, MOBA_BLOCK_LOG2)
    past = col < own
    colf = col.astype(F32)
    for e, cols in enumerate(heads):
        q = q_ref[:, cols]
        km = kmean_ref[e]
        k_hi = km.astype(MXU_DTYPE)
        r1 = km - k_hi.astype(F32)
        k_mid = r1.astype(MXU_DTYPE)
        k_lo = (r1 - k_mid.astype(F32)).astype(MXU_DTYPE)
        gate = (lax.dot_general(q, k_lo, NT_DIMS, preferred_element_type=F32)
                + lax.dot_general(q, k_mid, NT_DIMS, preferred_element_type=F32)
                + lax.dot_general(q, k_hi, NT_DIMS, preferred_element_type=F32))
        g = jnp.where(past, gate, -jnp.inf)
        chosen = jnp.zeros(gate.shape, jnp.bool_)
        for _ in range(min(MOBA_TOPK, n_blocks)):
            best = jnp.max(g, axis=1, keepdims=True)
            first_best = jnp.min(jnp.where(g == best, colf, float(LANES)), axis=1, keepdims=True)
            pick = colf == first_best
            chosen = jnp.logical_or(chosen, pick)
            g = jnp.where(pick, -jnp.inf, g)
        visible = jnp.logical_or(jnp.logical_and(chosen, past), col == own)
        qaug_ref[e, :, :hd] = q
        qaug_ref[e, :, hd:] = jnp.where(visible, 0.0, MASKED).astype(qaug_ref.dtype)

    def scores_of(e, j):
        kt = k_ref[pl.ds(pl.multiple_of(j * tk, tk), tk), heads[e]]
        lane = lax.broadcasted_iota(jnp.int32, (1, LANES), 1)
        blocks_per_tile = tk // MOBA_BLOCK
        onehot = jnp.concatenate(
            [jnp.broadcast_to(jnp.where(lane == j * blocks_per_tile + b, 1.0, 0.0),
                              (MOBA_BLOCK, LANES)) for b in range(blocks_per_tile)], axis=0)
        k_aug = jnp.concatenate([kt, onehot.astype(kt.dtype)], axis=1)
        return lax.dot_general(k_aug, qaug_ref[e], NT_DIMS, preferred_element_type=F32)

    streams = [_FlashStream(state[e * N_FLASH_REFS:(e + 1) * N_FLASH_REFS])
               for e in range(MOBA_HEADS_PER_STEP)]
    _causal_flash(i, streams, scores_of, lambda e, j: vt_ref[j, heads[e], :])
    for e, st in enumerate(streams):
        o_ref[:, heads[e]] = st.result().T.astype(o_ref.dtype)


def _moba_attention(qk, vt, n_heads, q_col, k_col, v_row):
    bsz, seq, _ = qk.shape
    t = min(ATTN_TILE, seq)
    n_tiles = seq // t
    n_blocks = seq // MOBA_BLOCK
    per = MOBA_HEADS_PER_STEP
    assert seq % t == 0 and t % MOBA_BLOCK == 0 and n_blocks <= LANES
    assert n_heads % per == 0 and q_col % per == 0 and k_col % per == 0 and v_row % per == 0
    hd = A_HEAD_DIM
    scratch = [pltpu.VMEM((per, LANES, hd), F32), pltpu.VMEM((per, t, 2 * hd), MXU_DTYPE)]
    for _ in range(per):
        scratch += _flash_scratch(t, hd)
    return pl.pallas_call(
        functools.partial(_moba_kernel, n_blocks=n_blocks),
        grid=(bsz, n_heads // per, n_tiles),
        in_specs=[pl.BlockSpec((None, t, per * hd), lambda b, h, i: (b, i, q_col // per + h)),
                  pl.BlockSpec((None, seq, per * hd), lambda b, h, i: (b, 0, k_col // per + h)),
                  pl.BlockSpec((None, n_tiles, per * hd, t),
                               lambda b, h, i: (b, 0, v_row // per + h, 0))],
        out_specs=pl.BlockSpec((None, t, per * hd), lambda b, h, i: (b, i, h)),
        out_shape=jax.ShapeDtypeStruct((bsz, seq, n_heads * hd), MXU_DTYPE),
        scratch_shapes=scratch,
        compiler_params=_params(("parallel", "parallel", "arbitrary"), 56),
    )(qk, qk, vt)


def _diff_kernel(q1_ref, q2_ref, k1_ref, k2_ref, vt_ref, lam_ref, w_ref, o_ref, *state,
                 lambda_init):
    i = pl.program_id(2)
    tk = q1_ref.shape[0]
    map1 = _FlashStream(state[:N_FLASH_REFS])
    map2 = _FlashStream(state[N_FLASH_REFS:])
    qk_refs = ((q1_ref, k1_ref), (q2_ref, k2_ref))

    def scores_of(idx, j):
        q_ref, k_ref = qk_refs[idx]
        rows = pl.ds(pl.multiple_of(j * tk, tk), tk)
        return lax.dot_general(k_ref[rows, :], q_ref[...], NT_DIMS, preferred_element_type=F32)

    _causal_flash(i, [map1, map2], scores_of, lambda idx, j: vt_ref[j])

    lv = lam_ref[...]
    lam = (jnp.exp(jnp.sum(lv[0:1] * lv[1:2], axis=1, keepdims=True))
           - jnp.exp(jnp.sum(lv[2:3] * lv[3:4], axis=1, keepdims=True)) + lambda_init)
    y = map1.result() - lam * map2.result()
    ms = jnp.mean(y * y, axis=0, keepdims=True)
    y = (y * lax.rsqrt(ms + DIFF_SUBLN_EPS)) * w_ref[...]
    o_ref[...] = (y * (1.0 - lambda_init)).T.astype(o_ref.dtype)


def _diff_attention(qk, vt, lam_vecs, subln_w, lambda_init, n_heads, q_col, k_col, v_row):
    bsz, seq, _ = qk.shape
    t = min(ATTN_TILE, seq)
    n_tiles = seq // t
    hd = B_HEAD_DIM
    return pl.pallas_call(
        functools.partial(_diff_kernel, lambda_init=lambda_init),
        grid=(bsz, n_heads, n_tiles),
        in_specs=[pl.BlockSpec((None, t, hd), lambda b, h, i: (b, i, q_col + 2 * h)),
                  pl.BlockSpec((None, t, hd), lambda b, h, i: (b, i, q_col + 2 * h + 1)),
                  pl.BlockSpec((None, seq, hd), lambda b, h, i: (b, 0, k_col + 2 * h)),
                  pl.BlockSpec((None, seq, hd), lambda b, h, i: (b, 0, k_col + 2 * h + 1)),
                  pl.BlockSpec((None, n_tiles, 2 * hd, t), lambda b, h, i: (b, 0, v_row + h, 0)),
                  pl.BlockSpec((4, hd), lambda b, h, i: (0, 0)),
                  pl.BlockSpec((2 * hd, 1), lambda b, h, i: (0, 0))],
        out_specs=pl.BlockSpec((None, t, 2 * hd), lambda b, h, i: (b, i, h)),
        out_shape=jax.ShapeDtypeStruct((bsz, seq, n_heads * 2 * hd), MXU_DTYPE),
        scratch_shapes=_flash_scratch(t, 2 * hd) + _flash_scratch(t, 2 * hd),
        compiler_params=_params(("parallel", "parallel", "arbitrary"), 56),
    )(qk, qk, qk, qk, vt, lam_vecs.astype(F32), subln_w.astype(F32).reshape(2 * hd, 1))


def _swa_kernel(q_ref, kc_ref, kp_ref, vtc_ref, vtp_ref, sink_ref, o_ref):
    i = pl.program_id(1)
    w = C_WINDOW
    hd = C_HEAD_DIM
    gw = C_GROUP * hd
    lane = lax.broadcasted_iota(jnp.int32, (w, LANES), 1)
    lo = jnp.where(lane < hd, 1.0, 0.0).astype(q_ref.dtype)
    hi = jnp.where(lane >= hd, 1.0, 0.0).astype(q_ref.dtype)
    krow = lax.broadcasted_iota(jnp.int32, (2 * w, C_GROUP * w), 0)
    qcol = lax.broadcasted_iota(jnp.int32, (2 * w, C_GROUP * w), 1) & (w - 1)
    rel = w + qcol - krow
    valid = (rel >= 0) & (rel < w) & ((i - 1) * w + krow >= 0)
    ones = jnp.ones((ONES_ROWS, 2 * w), vtc_ref.dtype)
    for g in range(q_ref.shape[1] // gw):
        k_cols = slice(g * LANES, (g + 1) * LANES)
        v_rows = slice(g * hd, (g + 1) * hd)
        kk = jnp.concatenate([kp_ref[:, k_cols], kc_ref[:, k_cols]], axis=0)
        vt = jnp.concatenate([vtp_ref[v_rows, :], vtc_ref[v_rows, :]], axis=1)
        q_heads = []
        for p in range(C_GROUP // 2):
            q2 = q_ref[:, g * gw + p * LANES:g * gw + (p + 1) * LANES]
            q_heads += [q2 * lo, q2 * hi]
        q_all = jnp.concatenate(q_heads, axis=0)
        s = lax.dot_general(kk, q_all, NT_DIMS, preferred_element_type=F32)
        s = jnp.where(valid, s, MASKED)
        sink = sink_ref[g] * LOG2E
        m = jnp.maximum(_col_max(s), sink)
        e = jnp.concatenate([jnp.exp2(s[r:r + SLAB_ROWS] - m).astype(vt.dtype)
                             for r in range(0, 2 * w, SLAB_ROWS)], axis=0)
        o = jnp.dot(jnp.concatenate([vt, ones], axis=0), e, preferred_element_type=F32)
        o = o[:hd] / (o[hd:hd + 1] + jnp.exp2(sink - m))
        o = jnp.concatenate([o[:, h * w:(h + 1) * w] for h in range(C_GROUP)], axis=0)
        o_ref[:, g * gw:(g + 1) * gw] = o.T.astype(o_ref.dtype)


def _swa_attention(qk, vt, sinks, n_kv):
    bsz, seq, _ = qk.shape
    w = C_WINDOW
    q_width = n_kv * C_GROUP * C_HEAD_DIM
    k_width = n_kv * LANES
    assert q_width % k_width == 0
    per_tile = vt.shape[-1] // w
    sink_cols = jnp.repeat(sinks.astype(F32).reshape(n_kv, 1, C_GROUP), w, axis=-1)
    prev = lambda i: jnp.maximum(i - 1, 0)
    vt_spec = lambda at: pl.BlockSpec(
        (None, None, n_kv * C_HEAD_DIM, w),
        lambda b, i: (b, at(i) // per_tile, 0, at(i) % per_tile))
    return pl.pallas_call(
        _swa_kernel,
        grid=(bsz, seq // w),
        in_specs=[pl.BlockSpec((None, w, q_width), lambda b, i: (b, i, 0)),
                  pl.BlockSpec((None, w, k_width), lambda b, i: (b, i, q_width // k_width)),
                  pl.BlockSpec((None, w, k_width), lambda b, i: (b, prev(i), q_width // k_width)),
                  vt_spec(lambda i: i),
                  vt_spec(prev),
                  pl.BlockSpec((n_kv, 1, C_GROUP * w), lambda b, i: (0, 0, 0))],
        out_specs=pl.BlockSpec((None, w, q_width), lambda b, i: (b, i, 0)),
        out_shape=jax.ShapeDtypeStruct((bsz, seq, q_width), MXU_DTYPE),
        compiler_params=_params(("parallel", "parallel"), 32),
    )(qk, qk, qk, vt, vt, sink_cols)


def _outproj_kernel(*refs, n_parts, has_bias):
    y_refs, w_refs = refs[:n_parts], refs[n_parts:2 * n_parts]
    rest = list(refs[2 * n_parts:])
    b_ref = rest.pop(0) if has_bias else None
    g_ref, x_ref, o_ref = rest
    acc = jnp.dot(y_refs[0][...], w_refs[0][...], preferred_element_type=F32)
    for y_ref, w_ref in zip(y_refs[1:], w_refs[1:]):
        acc = acc + jnp.dot(y_ref[...], w_ref[...], preferred_element_type=F32)
    if has_bias:
        acc = acc + b_ref[...]
    o_ref[...] = x_ref[...] + g_ref[...] * acc


def _out_project(x, gate, ys, ws, bias=None, in_place=True):
    bsz, seq, d = x.shape
    tm = min(ROW_TILE, seq)
    in_specs = [pl.BlockSpec((None, tm, y.shape[-1]), lambda b, i: (b, i, 0)) for y in ys]
    in_specs += [pl.BlockSpec(w.shape, lambda b, i: (0, 0)) for w in ws]
    args = list(ys) + list(ws)
    if bias is not None:
        in_specs.append(pl.BlockSpec((1, d), lambda b, i: (0, 0)))
        args.append(bias.reshape(1, d).astype(F32))
    in_specs += [pl.BlockSpec((None, 1, d), lambda b, i: (b, 0, 0)),
                 pl.BlockSpec((None, tm, d), lambda b, i: (b, i, 0))]
    args += [gate, x]
    return pl.pallas_call(
        functools.partial(_outproj_kernel, n_parts=len(ys), has_bias=bias is not None),
        grid=(bsz, seq // tm),
        in_specs=in_specs,
        out_specs=pl.BlockSpec((None, tm, d), lambda b, i: (b, i, 0)),
        out_shape=jax.ShapeDtypeStruct(x.shape, F32),
        input_output_aliases={len(args) - 1: 0} if in_place else {},
        compiler_params=_params(("parallel", "parallel"), 48),
    )(*args)


def _ffn_kernel(x_ref, sc_ref, sh_ref, g_ref, wg_ref, wu_ref, wo_ref, *rest, final, n_h):
    rest = list(rest)
    fw_ref = rest.pop(0) if final else None
    o_ref, h_ref, acc_ref, act0_ref, act1_ref = rest
    acts = (act0_ref, act1_ref)
    k = pl.program_id(2)

    def hidden(dst_ref):
        h = h_ref[...]
        gt = jnp.dot(h, wg_ref[...], preferred_element_type=F32)
        up = jnp.dot(h, wu_ref[...], preferred_element_type=F32)
        dst_ref[...] = ((gt * jax.nn.sigmoid(gt)) * up).astype(dst_ref.dtype)

    def project(src_ref):
        return jnp.dot(src_ref[...], wo_ref[...], preferred_element_type=F32)

    @pl.when(k == 0)
    def _():
        h_ref[...] = _norm_mod(x_ref[...], sc_ref[...], sh_ref[...]).astype(h_ref.dtype)
        acc_ref[...] = jnp.zeros_like(acc_ref)
        hidden(acts[0])

    for parity in (0, 1):
        @pl.when((k > 0) & (k < n_h) & (k % 2 == parity))
        def _():
            acc_ref[...] += project(acts[1 - parity])
            hidden(acts[parity])

    @pl.when(k == n_h)
    def _():
        xn = x_ref[...] + g_ref[...] * (acc_ref[...] + project(acts[(n_h - 1) % 2]))
        if final:
            ms = jnp.mean(xn * xn, axis=-1, keepdims=True)
            xn = (xn * lax.rsqrt(ms + RMS_EPS)) * fw_ref[...]
        o_ref[...] = xn


def _ffn(x, sc, sh, gate, w_in, w_out, final_w=None):
    bsz, seq, d = x.shape
    hidden = w_out.shape[0]
    tm = min(ROW_TILE, seq)
    th = _col_tile(hidden)
    n_h = hidden // th
    mod_spec = pl.BlockSpec((None, 1, d), lambda b, i, k: (b, 0, 0))
    in_specs = [pl.BlockSpec((None, tm, d), lambda b, i, k: (b, i, 0)),
                mod_spec, mod_spec, mod_spec,
                pl.BlockSpec((None, d, th), lambda b, i, k: (jnp.minimum(k, n_h - 1), 0, 0)),
                pl.BlockSpec((None, d, th),
                             lambda b, i, k: (n_h + jnp.minimum(k, n_h - 1), 0, 0)),
                pl.BlockSpec((th, d), lambda b, i, k: (jnp.maximum(k - 1, 0), 0))]
    w_in = _column_tiles(w_in, th)
    args = [x, sc, sh, gate, w_in, w_in, w_out]
    if final_w is not None:
        in_specs.append(pl.BlockSpec((1, d), lambda b, i, k: (0, 0)))
        args.append(final_w.reshape(1, d).astype(F32))
    return pl.pallas_call(
        functools.partial(_ffn_kernel, final=final_w is not None, n_h=n_h),
        grid=(bsz, seq // tm, n_h + 1),
        in_specs=in_specs,
        out_specs=pl.BlockSpec((None, tm, d), lambda b, i, k: (b, i, 0)),
        out_shape=jax.ShapeDtypeStruct(x.shape, F32),
        scratch_shapes=[pltpu.VMEM((tm, d), MXU_DTYPE), pltpu.VMEM((tm, d), F32),
                        pltpu.VMEM((tm, th), MXU_DTYPE), pltpu.VMEM((tm, th), MXU_DTYPE)],
        input_output_aliases={0: 0},
        compiler_params=_params(("parallel", "parallel", "arbitrary"), 48),
    )(*args)


def _rope_tables(seq, dim, query_scale):
    pos = jnp.arange(seq, dtype=F32)
    inv_freq = ROPE_THETA ** (-jnp.arange(0, dim, 2, dtype=F32) / dim)
    ang = pos[:, None] * inv_freq[None, :]
    cos, sin = jnp.cos(ang), jnp.sin(ang)
    reps = LANES // dim
    cos = jnp.tile(jnp.concatenate([cos, cos], axis=1), (1, reps))
    sin = jnp.tile(jnp.concatenate([-sin, sin], axis=1), (1, reps))
    return jnp.stack([cos * query_scale, cos]), jnp.stack([sin * query_scale, sin])


def _lambda_init(layer):
    return 0.8 - 0.6 * math.exp(-0.3 * layer)


def _dup_heads(w, n_heads, head_dim):
    lead = w.shape[:-1]
    w = w.reshape(lead + (n_heads, 1, head_dim))
    return jnp.broadcast_to(w, lead + (n_heads, 2, head_dim)).reshape(lead + (2 * n_heads * head_dim,))


def kernel(x, c, ada_w, ada_b, ab_w_in, ab_w_out, diff_lambda, diff_subln, swa_w_in, swa_b_in,
           swa_w_out, swa_b_out, swa_sinks, ffn_w_in, ffn_w_out, final_norm):
    bsz, seq, d = x.shape
    depth = ada_w.shape[0]
    a_heads, b_heads = d // 256, d // 512
    a_width, b_width = a_heads * A_HEAD_DIM, 2 * b_heads * B_HEAD_DIM
    assert a_width == b_width
    c_q_heads = d // C_HEAD_DIM
    c_kv_heads = c_q_heads // C_GROUP
    c_q_width, c_kv_width = c_q_heads * C_HEAD_DIM, c_kv_heads * C_HEAD_DIM

    rope_ab = _rope_tables(seq, A_HEAD_DIM, A_HEAD_DIM ** -0.5 * LOG2E)
    rope_c = _rope_tables(seq, C_HEAD_DIM, C_HEAD_DIM ** -0.5 * LOG2E)
    mod = _modulation(c, ada_w, ada_b)

    for layer in range(depth):
        sh1, sc1, g1, sh2, sc2, g2 = [
            mod[layer, :, None, m * d:(m + 1) * d] for m in range(N_MOD)]
        li = layer // 2
        if layer % 2 == 0:
            w_in = ab_w_in[li].astype(MXU_DTYPE)
            aq, ak, av, bq, bk, bv = jnp.split(
                w_in, [a_width, 2 * a_width, 3 * a_width,
                       3 * a_width + b_width, 3 * a_width + 2 * b_width], axis=1)
            qk = _project(x, sc1, sh1, jnp.concatenate([aq, ak, bq, bk], axis=1),
                          rope=rope_ab, rope_dim=A_HEAD_DIM,
                          is_query_col=lambda col: (col // a_width) % 2 == 0)
            vt = _project(x, sc1, sh1, jnp.concatenate([av, bv], axis=1), transpose_out=True)
            ya = _moba_attention(qk, vt, a_heads, q_col=0, k_col=a_heads, v_row=0)
            yb = _diff_attention(qk, vt, diff_lambda[li], diff_subln[li], _lambda_init(layer),
                                 b_heads, q_col=2 * a_heads, k_col=2 * a_heads + 2 * b_heads,
                                 v_row=a_width // (2 * B_HEAD_DIM))
            w_out = ab_w_out[li].astype(MXU_DTYPE)
            x = _out_project(x, g1, [ya, yb], [w_out[:a_width], w_out[a_width:]],
                             in_place=layer > 0)
        else:
            w_in, b_in = swa_w_in[li], swa_b_in[li]
            wq, wk, wv = jnp.split(w_in, [c_q_width, c_q_width + c_kv_width], axis=1)
            bq_, bk_, bv_ = jnp.split(b_in, [c_q_width, c_q_width + c_kv_width])
            w_qk = jnp.concatenate([wq, _dup_heads(wk, c_kv_heads, C_HEAD_DIM)], axis=1)
            b_qk = jnp.concatenate([bq_, _dup_heads(bk_, c_kv_heads, C_HEAD_DIM)])
            qk = _project(x, sc1, sh1, w_qk.astype(MXU_DTYPE), bias=b_qk,
                          rope=rope_c, rope_dim=C_HEAD_DIM,
                          is_query_col=lambda col: col < c_q_width)
            vt = _project(x, sc1, sh1, wv.astype(MXU_DTYPE), bias=bv_, transpose_out=True)
            y = _swa_attention(qk, vt, swa_sinks[li], c_kv_heads)
            x = _out_project(x, g1, [y], [swa_w_out[li].astype(MXU_DTYPE)], bias=swa_b_out[li])
        x = _ffn(x, sc2, sh2, g2, ffn_w_in[layer].astype(MXU_DTYPE),
                 ffn_w_out[layer].astype(MXU_DTYPE),
                 final_w=final_norm if layer == depth - 1 else None)
    return x
```

```python
import functools
import math

import jax
import jax.numpy as jnp
from jax import lax
from jax.experimental import pallas as pl
from jax.experimental.pallas import tpu as pltpu

F32 = jnp.float32
MXU_DTYPE = jnp.bfloat16

ROPE_THETA = 10000.0
RMS_EPS = 1e-6
DIFF_SUBLN_EPS = 1e-5
A_HEAD_DIM = 128
MOBA_BLOCK = 256
MOBA_BLOCK_LOG2 = 8
MOBA_TOPK = 3
B_HEAD_DIM = 128
C_HEAD_DIM = 64
C_GROUP = 8
C_WINDOW = 128
N_MOD = 6

LANES = 128
MASKED = -1e30
LOG2E = 1.4426950408889634

ATTN_TILE = 512
ROW_TILE = 512
FFN_ROW_TILE = 1024
MOD_COL_TILE = 1024
MIB = 2 ** 20

NT_DIMS = (((1,), (1,)), ((), ()))


def _params(semantics, vmem_mib):
    return pltpu.CompilerParams(dimension_semantics=semantics,
                                vmem_limit_bytes=vmem_mib * MIB)


def _col_tile(n):
    for t in (512, 256, 128):
        if n % t == 0:
            return t
    raise ValueError(f"column count {n} is not a multiple of {LANES}")


def _column_tiles(w, tn):
    k, n = w.shape
    return w.reshape(k, n // tn, tn).transpose(1, 0, 2)


def _norm_mod(x, sc, sh):
    ms = jnp.mean(x * x, axis=-1, keepdims=True)
    return (x * lax.rsqrt(ms + RMS_EPS)) * (1.0 + sc) + sh


def _mod_kernel(c_ref, w_ref, b_ref, o_ref):
    c = c_ref[...]
    cond = (c * jax.nn.sigmoid(c)).astype(MXU_DTYPE)
    o_ref[...] = jnp.dot(cond, w_ref[...].astype(MXU_DTYPE),
                         preferred_element_type=F32) + b_ref[...]


def _modulation(c, ada_w, ada_b):
    depth, d, n = ada_w.shape
    bsz = c.shape[0]
    rows = -(-bsz // 8) * 8
    c_pad = jnp.pad(c, ((0, rows - bsz), (0, 0)))
    tn = MOD_COL_TILE if n % MOD_COL_TILE == 0 else _col_tile(n)
    out = pl.pallas_call(
        _mod_kernel,
        grid=(depth, n // tn),
        in_specs=[pl.BlockSpec((rows, d), lambda l, j: (0, 0)),
                  pl.BlockSpec((None, d, tn), lambda l, j: (l, 0, j)),
                  pl.BlockSpec((None, 1, tn), lambda l, j: (l, 0, j))],
        out_specs=pl.BlockSpec((None, rows, tn), lambda l, j: (l, 0, j)),
        out_shape=jax.ShapeDtypeStruct((depth, rows, n), F32),
        compiler_params=_params(("parallel", "parallel"), 40),
    )(c_pad, ada_w, ada_b.reshape(depth, 1, n))
    return out[:, :bsz]


def _rope_partner(a, rope_dim):
    if rope_dim == LANES:
        return pltpu.roll(a, LANES // 2, 1)
    lane = lax.broadcasted_iota(jnp.int32, a.shape, 1)
    half = rope_dim // 2
    first_half = (lane & (rope_dim - 1)) < half
    return jnp.where(first_half, pltpu.roll(a, LANES - half, 1), pltpu.roll(a, half, 1))


def _proj_kernel(x_ref, sc_ref, sh_ref, w_ref, *rest, rope_dim, has_bias, transpose_out, n_t):
    rest = list(rest)
    b_ref = rest.pop(0) if has_bias else None
    cos_ref, sin_ref = (rest.pop(0), rest.pop(0)) if rope_dim else (None, None)
    o_ref, h_ref, raw0_ref, raw1_ref = rest
    raws = (raw0_ref, raw1_ref)
    j = pl.program_id(2)

    def multiply(dst_ref):
        acc = jnp.dot(h_ref[...], w_ref[j], preferred_element_type=F32)
        dst_ref[...] = acc + b_ref[...] if has_bias else acc

    def finish(src_ref):
        if rope_dim:
            cos, sin = cos_ref[...], sin_ref[...]
            for t in range(src_ref.shape[1] // LANES):
                a = src_ref[:, t * LANES:(t + 1) * LANES]
                o_ref[:, t * LANES:(t + 1) * LANES] = (
                    a * cos + _rope_partner(a, rope_dim) * sin).astype(o_ref.dtype)
        elif transpose_out:
            o_ref[...] = src_ref[...].T.astype(o_ref.dtype)
        else:
            o_ref[...] = src_ref[...].astype(o_ref.dtype)

    @pl.when(j == 0)
    def _():
        h_ref[...] = _norm_mod(x_ref[...], sc_ref[...], sh_ref[...]).astype(h_ref.dtype)
        multiply(raws[0])

    for parity in (0, 1):
        @pl.when((j > 0) & (j < n_t) & (j % 2 == parity))
        def _():
            multiply(raws[parity])
            finish(raws[1 - parity])

    @pl.when(j == n_t)
    def _():
        finish(raws[(n_t - 1) % 2])


def _project(x, sc, sh, w, bias=None, rope=None, rope_dim=0, is_query_col=None,
             transpose_out=False):
    bsz, seq, d = x.shape
    n = w.shape[1]
    tm = min(ROW_TILE, seq)
    tn = _col_tile(n)
    n_t = n // tn
    multiplied = lambda j: jnp.minimum(j, n_t - 1)
    finished = lambda j: jnp.maximum(j - 1, 0)
    in_specs = [pl.BlockSpec((None, tm, d), lambda b, i, j: (b, i, 0)),
                pl.BlockSpec((None, 1, d), lambda b, i, j: (b, 0, 0)),
                pl.BlockSpec((None, 1, d), lambda b, i, j: (b, 0, 0)),
                pl.BlockSpec((n_t, d, tn), lambda b, i, j: (0, 0, 0),
                             pipeline_mode=pl.Buffered(1))]
    args = [x, sc, sh, _column_tiles(w, tn)]
    if bias is not None:
        in_specs.append(pl.BlockSpec((1, tn), lambda b, i, j: (0, multiplied(j))))
        args.append(bias.reshape(1, n).astype(F32))
    if rope_dim:
        table = lambda b, i, j: (jnp.where(is_query_col(finished(j) * tn), 0, 1), i, 0)
        in_specs += [pl.BlockSpec((None, tm, LANES), table)] * 2
        args += list(rope)
    if transpose_out:
        out_shape = jax.ShapeDtypeStruct((bsz, seq // tm, n, tm), MXU_DTYPE)
        out_spec = pl.BlockSpec((None, None, tn, tm), lambda b, i, j: (b, i, finished(j), 0))
    else:
        out_shape = jax.ShapeDtypeStruct((bsz, seq, n), MXU_DTYPE)
        out_spec = pl.BlockSpec((None, tm, tn), lambda b, i, j: (b, i, finished(j)))
    return pl.pallas_call(
        functools.partial(_proj_kernel, rope_dim=rope_dim, has_bias=bias is not None,
                          transpose_out=transpose_out, n_t=n_t),
        grid=(bsz, seq // tm, n_t + 1),
        in_specs=in_specs,
        out_specs=out_spec,
        out_shape=out_shape,
        scratch_shapes=[pltpu.VMEM((tm, d), MXU_DTYPE),
                        pltpu.VMEM((tm, tn), F32), pltpu.VMEM((tm, tn), F32)],
        compiler_params=_params(("parallel", "parallel", "arbitrary"), 40),
    )(*args)


N_FLASH_REFS = 10
SLAB_ROWS = 64
ONES_ROWS = 16


def _flash_scratch(t, v_dim):
    stat = pltpu.VMEM((1, t), F32)
    scores = pltpu.VMEM((t, t), F32)
    probs = pltpu.VMEM((t, t), MXU_DTYPE)
    return [stat, pltpu.VMEM((v_dim + ONES_ROWS, t), F32),
            scores, scores, stat, stat, probs, probs, stat, stat]


class _FlashStream:
    def __init__(self, refs):
        self.m, self.acc = refs[:2]
        self.s, self.s_max, self.p, self.alpha = refs[2:4], refs[4:6], refs[6:8], refs[8:10]

    def result(self):
        v_dim = self.acc.shape[0] - ONES_ROWS
        return self.acc[:v_dim, :] / self.acc[v_dim:v_dim + 1, :]


def _col_max(x):
    slab = x[:SLAB_ROWS]
    for r in range(SLAB_ROWS, x.shape[0], SLAB_ROWS):
        slab = jnp.maximum(slab, x[r:r + SLAB_ROWS])
    return jnp.max(slab, axis=0, keepdims=True)


def _causal_flash(i, streams, scores_of, values_of):
    for st in streams:
        st.m[...] = jnp.full_like(st.m, MASKED)
        st.acc[...] = jnp.zeros_like(st.acc)
        st.alpha[1][...] = jnp.ones_like(st.alpha[1])
        st.p[1][...] = jnp.zeros_like(st.p[1])

    def put_scores(st, slot, s):
        st.s[slot][...] = s
        st.s_max[slot][...] = _col_max(s)

    for idx, st in enumerate(streams):
        s = scores_of(idx, i)
        krow = lax.broadcasted_iota(jnp.int32, s.shape, 0)
        qcol = lax.broadcasted_iota(jnp.int32, s.shape, 1)
        put_scores(st, 0, jnp.where(krow <= qcol, s, MASKED))
    n_steps = i + 1
    last_past = jnp.maximum(i - 1, 0)

    def values_step(k, slot):
        tile = jnp.where(k <= 0, i, k - 1)
        for idx, st in enumerate(streams):
            vt = values_of(idx, tile)
            vt = jnp.concatenate([vt, jnp.ones((ONES_ROWS, vt.shape[1]), vt.dtype)], axis=0)
            st.acc[...] = st.alpha[slot][...] * st.acc[...] + jnp.dot(
                vt, st.p[slot][...], preferred_element_type=F32)

    def step(k, cur):
        nxt = 1 - cur
        ahead = jnp.minimum(k, last_past)
        for st in streams:
            m_prev = st.m[...]
            m_new = jnp.maximum(m_prev, st.s_max[cur][...])
            for r in range(0, st.s[cur].shape[0], SLAB_ROWS):
                p = jnp.exp2(st.s[cur][r:r + SLAB_ROWS, :] - m_new)
                st.p[cur][r:r + SLAB_ROWS, :] = p.astype(st.p[cur].dtype)
            st.m[...] = m_new
            st.alpha[cur][...] = jnp.exp2(m_prev - m_new)
        for idx, st in enumerate(streams):
            put_scores(st, nxt, scores_of(idx, ahead))
        values_step(k - 1, nxt)

    def pair_body(kk, carry):
        step(2 * kk, 0)
        step(2 * kk + 1, 1)
        return carry

    lax.fori_loop(0, n_steps // 2, pair_body, 0)
    odd = n_steps % 2 == 1

    @pl.when(odd)
    def _():
        step(n_steps - 1, 0)
        values_step(n_steps - 1, 0)

    @pl.when(jnp.logical_not(odd))
    def _():
        values_step(n_steps - 1, 1)


MOBA_HEADS_PER_STEP = 2


def _moba_kernel(q_ref, k_ref, vt_ref, o_ref, kmean_ref, qaug_ref, *state, n_blocks):
    i = pl.program_id(2)
    tq = q_ref.shape[0]
    tk = tq
    hd = A_HEAD_DIM
    heads = [slice(e * hd, (e + 1) * hd) for e in range(MOBA_HEADS_PER_STEP)]

    @pl.when(i == 0)
    def _():
        kmean_ref[...] = jnp.zeros_like(kmean_ref)

        def mean_body(n, carry):
            kb = k_ref[pl.ds(pl.multiple_of(n * MOBA_BLOCK, MOBA_BLOCK), MOBA_BLOCK), :]
            mean = jnp.sum(kb.astype(F32), axis=0, keepdims=True) * (1.0 / MOBA_BLOCK)
            for e, cols in enumerate(heads):
                kmean_ref[e, pl.ds(n, 1), :] = mean[:, cols]
            return carry

        lax.fori_loop(0, n_blocks, mean_body, 0)

    col = lax.broadcasted_iota(jnp.int32, (tq, LANES), 1)
    row = lax.broadcasted_iota(jnp.int32, (tq, LANES), 0)
    own = jnp.right_shift(i * tq + row, MOBA_BLOCK_LOG2)
    past = col < own
    colf = col.astype(F32)
    for e, cols in enumerate(heads):
        q = q_ref[:, cols]
        km = kmean_ref[e]
        k_hi = km.astype(MXU_DTYPE)
        r1 = km - k_hi.astype(F32)
        k_mid = r1.astype(MXU_DTYPE)
        k_lo = (r1 - k_mid.astype(F32)).astype(MXU_DTYPE)
        gate = (lax.dot_general(q, k_lo, NT_DIMS, preferred_element_type=F32)
                + lax.dot_general(q, k_mid, NT_DIMS, preferred_element_type=F32)
                + lax.dot_general(q, k_hi, NT_DIMS, preferred_element_type=F32))
        g = jnp.where(past, gate, -jnp.inf)
        chosen = jnp.zeros(gate.shape, jnp.bool_)
        for _ in range(min(MOBA_TOPK, n_blocks)):
            best = jnp.max(g, axis=1, keepdims=True)
            first_best = jnp.min(jnp.where(g == best, colf, float(LANES)), axis=1, keepdims=True)
            pick = colf == first_best
            chosen = jnp.logical_or(chosen, pick)
            g = jnp.where(pick, -jnp.inf, g)
        visible = jnp.logical_or(jnp.logical_and(chosen, past), col == own)
        qaug_ref[e, :, :hd] = q
        qaug_ref[e, :, hd:] = jnp.where(visible, 0.0, MASKED).astype(qaug_ref.dtype)

    def scores_of(e, j):
        kt = k_ref[pl.ds(pl.multiple_of(j * tk, tk), tk), heads[e]]
        lane = lax.broadcasted_iota(jnp.int32, (1, LANES), 1)
        blocks_per_tile = tk // MOBA_BLOCK
        onehot = jnp.concatenate(
            [jnp.broadcast_to(jnp.where(lane == j * blocks_per_tile + b, 1.0, 0.0),
                              (MOBA_BLOCK, LANES)) for b in range(blocks_per_tile)], axis=0)
        k_aug = jnp.concatenate([kt, onehot.astype(kt.dtype)], axis=1)
        return lax.dot_general(k_aug, qaug_ref[e], NT_DIMS, preferred_element_type=F32)

    streams = [_FlashStream(state[e * N_FLASH_REFS:(e + 1) * N_FLASH_REFS])
               for e in range(MOBA_HEADS_PER_STEP)]
    _causal_flash(i, streams, scores_of, lambda e, j: vt_ref[j, heads[e], :])
    for e, st in enumerate(streams):
        o_ref[:, heads[e]] = st.result().T.astype(o_ref.dtype)


def _moba_attention(qk, vt, n_heads, q_col, k_col, v_row):
    bsz, seq, _ = qk.shape
    t = min(ATTN_TILE, seq)
    n_tiles = seq // t
    n_blocks = seq // MOBA_BLOCK
    per = MOBA_HEADS_PER_STEP
    assert seq % t == 0 and t % MOBA_BLOCK == 0 and n_blocks <= LANES
    assert n_heads % per == 0 and q_col % per == 0 and k_col % per == 0 and v_row % per == 0
    hd = A_HEAD_DIM
    scratch = [pltpu.VMEM((per, LANES, hd), F32), pltpu.VMEM((per, t, 2 * hd), MXU_DTYPE)]
    for _ in range(per):
        scratch += _flash_scratch(t, hd)
    return pl.pallas_call(
        functools.partial(_moba_kernel, n_blocks=n_blocks),
        grid=(bsz, n_heads // per, n_tiles),
        in_specs=[pl.BlockSpec((None, t, per * hd), lambda b, h, i: (b, i, q_col // per + h)),
                  pl.BlockSpec((None, seq, per * hd), lambda b, h, i: (b, 0, k_col // per + h)),
                  pl.BlockSpec((None, n_tiles, per * hd, t),
                               lambda b, h, i: (b, 0, v_row // per + h, 0))],
        out_specs=pl.BlockSpec((None, t, per * hd), lambda b, h, i: (b, i, h)),
        out_shape=jax.ShapeDtypeStruct((bsz, seq, n_heads * hd), MXU_DTYPE),
        scratch_shapes=scratch,
        compiler_params=_params(("parallel", "parallel", "arbitrary"), 56),
    )(qk, qk, vt)


def _diff_kernel(q1_ref, q2_ref, k1_ref, k2_ref, vt_ref, lam_ref, w_ref, o_ref, *state,
                 lambda_init):
    i = pl.program_id(2)
    tk = q1_ref.shape[0]
    map1 = _FlashStream(state[:N_FLASH_REFS])
    map2 = _FlashStream(state[N_FLASH_REFS:])
    qk_refs = ((q1_ref, k1_ref), (q2_ref, k2_ref))

    def scores_of(idx, j):
        q_ref, k_ref = qk_refs[idx]
        rows = pl.ds(pl.multiple_of(j * tk, tk), tk)
        return lax.dot_general(k_ref[rows, :], q_ref[...], NT_DIMS, preferred_element_type=F32)

    _causal_flash(i, [map1, map2], scores_of, lambda idx, j: vt_ref[j])

    lv = lam_ref[...]
    lam = (jnp.exp(jnp.sum(lv[0:1] * lv[1:2], axis=1, keepdims=True))
           - jnp.exp(jnp.sum(lv[2:3] * lv[3:4], axis=1, keepdims=True)) + lambda_init)
    y = map1.result() - lam * map2.result()
    ms = jnp.mean(y * y, axis=0, keepdims=True)
    y = (y * lax.rsqrt(ms + DIFF_SUBLN_EPS)) * w_ref[...]
    o_ref[...] = (y * (1.0 - lambda_init)).T.astype(o_ref.dtype)


def _diff_attention(qk, vt, lam_vecs, subln_w, lambda_init, n_heads, q_col, k_col, v_row):
    bsz, seq, _ = qk.shape
    t = min(ATTN_TILE, seq)
    n_tiles = seq // t
    hd = B_HEAD_DIM
    return pl.pallas_call(
        functools.partial(_diff_kernel, lambda_init=lambda_init),
        grid=(bsz, n_heads, n_tiles),
        in_specs=[pl.BlockSpec((None, t, hd), lambda b, h, i: (b, i, q_col + 2 * h)),
                  pl.BlockSpec((None, t, hd), lambda b, h, i: (b, i, q_col + 2 * h + 1)),
                  pl.BlockSpec((None, seq, hd), lambda b, h, i: (b, 0, k_col + 2 * h)),
                  pl.BlockSpec((None, seq, hd), lambda b, h, i: (b, 0, k_col + 2 * h + 1)),
                  pl.BlockSpec((None, n_tiles, 2 * hd, t), lambda b, h, i: (b, 0, v_row + h, 0)),
                  pl.BlockSpec((4, hd), lambda b, h, i: (0, 0)),
                  pl.BlockSpec((2 * hd, 1), lambda b, h, i: (0, 0))],
        out_specs=pl.BlockSpec((None, t, 2 * hd), lambda b, h, i: (b, i, h)),
        out_shape=jax.ShapeDtypeStruct((bsz, seq, n_heads * 2 * hd), MXU_DTYPE),
        scratch_shapes=_flash_scratch(t, 2 * hd) + _flash_scratch(t, 2 * hd),
        compiler_params=_params(("parallel", "parallel", "arbitrary"), 56),
    )(qk, qk, qk, qk, vt, lam_vecs.astype(F32), subln_w.astype(F32).reshape(2 * hd, 1))


def _swa_kernel(q_ref, kc_ref, kp_ref, vtc_ref, vtp_ref, sink_ref, o_ref):
    i = pl.program_id(1)
    w = C_WINDOW
    hd = C_HEAD_DIM
    gw = C_GROUP * hd
    lane = lax.broadcasted_iota(jnp.int32, (w, LANES), 1)
    lo = jnp.where(lane < hd, 1.0, 0.0).astype(q_ref.dtype)
    hi = jnp.where(lane >= hd, 1.0, 0.0).astype(q_ref.dtype)
    krow = lax.broadcasted_iota(jnp.int32, (2 * w, C_GROUP * w), 0)
    qcol = lax.broadcasted_iota(jnp.int32, (2 * w, C_GROUP * w), 1) & (w - 1)
    rel = w + qcol - krow
    valid = (rel >= 0) & (rel < w) & ((i - 1) * w + krow >= 0)
    ones = jnp.ones((ONES_ROWS, 2 * w), vtc_ref.dtype)
    for g in range(q_ref.shape[1] // gw):
        k_cols = slice(g * LANES, (g + 1) * LANES)
        v_rows = slice(g * hd, (g + 1) * hd)
        kk = jnp.concatenate([kp_ref[:, k_cols], kc_ref[:, k_cols]], axis=0)
        vt = jnp.concatenate([vtp_ref[v_rows, :], vtc_ref[v_rows, :]], axis=1)
        q_heads = []
        for p in range(C_GROUP // 2):
            q2 = q_ref[:, g * gw + p * LANES:g * gw + (p + 1) * LANES]
            q_heads += [q2 * lo, q2 * hi]
        q_all = jnp.concatenate(q_heads, axis=0)
        s = lax.dot_general(kk, q_all, NT_DIMS, preferred_element_type=F32)
        s = jnp.where(valid, s, MASKED)
        sink = sink_ref[g] * LOG2E
        m = jnp.maximum(_col_max(s), sink)
        e = jnp.concatenate([jnp.exp2(s[r:r + SLAB_ROWS] - m).astype(vt.dtype)
                             for r in range(0, 2 * w, SLAB_ROWS)], axis=0)
        o = jnp.dot(jnp.concatenate([vt, ones], axis=0), e, preferred_element_type=F32)
        o = o[:hd] / (o[hd:hd + 1] + jnp.exp2(sink - m))
        o = jnp.concatenate([o[:, h * w:(h + 1) * w] for h in range(C_GROUP)], axis=0)
        o_ref[:, g * gw:(g + 1) * gw] = o.T.astype(o_ref.dtype)


def _swa_attention(qk, vt, sinks, n_kv):
    bsz, seq, _ = qk.shape
    w = C_WINDOW
    q_width = n_kv * C_GROUP * C_HEAD_DIM
    k_width = n_kv * LANES
    assert q_width % k_width == 0
    per_tile = vt.shape[-1] // w
    sink_cols = jnp.repeat(sinks.astype(F32).reshape(n_kv, 1, C_GROUP), w, axis=-1)
    prev = lambda i: jnp.maximum(i - 1, 0)
    vt_spec = lambda at: pl.BlockSpec(
        (None, None, n_kv * C_HEAD_DIM, w),
        lambda b, i: (b, at(i) // per_tile, 0, at(i) % per_tile))
    return pl.pallas_call(
        _swa_kernel,
        grid=(bsz, seq // w),
        in_specs=[pl.BlockSpec((None, w, q_width), lambda b, i: (b, i, 0)),
                  pl.BlockSpec((None, w, k_width), lambda b, i: (b, i, q_width // k_width)),
                  pl.BlockSpec((None, w, k_width), lambda b, i: (b, prev(i), q_width // k_width)),
                  vt_spec(lambda i: i),
                  vt_spec(prev),
                  pl.BlockSpec((n_kv, 1, C_GROUP * w), lambda b, i: (0, 0, 0))],
        out_specs=pl.BlockSpec((None, w, q_width), lambda b, i: (b, i, 0)),
        out_shape=jax.ShapeDtypeStruct((bsz, seq, q_width), MXU_DTYPE),
        compiler_params=_params(("parallel", "parallel"), 32),
    )(qk, qk, qk, vt, vt, sink_cols)


def _outproj_kernel(*refs, n_parts, has_bias):
    y_refs, w_refs = refs[:n_parts], refs[n_parts:2 * n_parts]
    rest = list(refs[2 * n_parts:])
    b_ref = rest.pop(0) if has_bias else None
    g_ref, x_ref, o_ref = rest
    acc = jnp.dot(y_refs[0][...], w_refs[0][...], preferred_element_type=F32)
    for y_ref, w_ref in zip(y_refs[1:], w_refs[1:]):
        acc = acc + jnp.dot(y_ref[...], w_ref[...], preferred_element_type=F32)
    if has_bias:
        acc = acc + b_ref[...]
    o_ref[...] = x_ref[...] + g_ref[...] * acc


def _out_project(x, gate, ys, ws, bias=None, in_place=True):
    bsz, seq, d = x.shape
    tm = min(ROW_TILE, seq)
    in_specs = [pl.BlockSpec((None, tm, y.shape[-1]), lambda b, i: (b, i, 0)) for y in ys]
    in_specs += [pl.BlockSpec(w.shape, lambda b, i: (0, 0)) for w in ws]
    args = list(ys) + list(ws)
    if bias is not None:
        in_specs.append(pl.BlockSpec((1, d), lambda b, i: (0, 0)))
        args.append(bias.reshape(1, d).astype(F32))
    in_specs += [pl.BlockSpec((None, 1, d), lambda b, i: (b, 0, 0)),
                 pl.BlockSpec((None, tm, d), lambda b, i: (b, i, 0))]
    args += [gate, x]
    return pl.pallas_call(
        functools.partial(_outproj_kernel, n_parts=len(ys), has_bias=bias is not None),
        grid=(bsz, seq // tm),
        in_specs=in_specs,
        out_specs=pl.BlockSpec((None, tm, d), lambda b, i: (b, i, 0)),
        out_shape=jax.ShapeDtypeStruct(x.shape, F32),
        input_output_aliases={len(args) - 1: 0} if in_place else {},
        compiler_params=_params(("parallel", "parallel"), 48),
    )(*args)


def _ffn_kernel(x_ref, sc_ref, sh_ref, g_ref, wg_ref, wu_ref, wo_ref, *rest, final):
    rest = list(rest)
    fw_ref = rest.pop(0) if final else None
    o_ref, h_ref = rest
    k = pl.program_id(2)
    tm = x_ref.shape[0]
    halves = [slice(r, r + min(ROW_TILE, tm)) for r in range(0, tm, ROW_TILE)]

    @pl.when(k == 0)
    def _():
        for rows in halves:
            h_ref[rows, :] = _norm_mod(x_ref[rows, :], sc_ref[...], sh_ref[...]).astype(h_ref.dtype)
        o_ref[...] = jnp.zeros_like(o_ref)

    for rows in halves:
        h = h_ref[rows, :]
        gt = jnp.dot(h, wg_ref[...], preferred_element_type=F32)
        up = jnp.dot(h, wu_ref[...], preferred_element_type=F32)
        act = ((gt * jax.nn.sigmoid(gt)) * up).astype(h.dtype)
        o_ref[rows, :] += jnp.dot(act, wo_ref[...], preferred_element_type=F32)

    @pl.when(k == pl.num_programs(2) - 1)
    def _():
        for rows in halves:
            xn = x_ref[rows, :] + g_ref[...] * o_ref[rows, :]
            if final:
                ms = jnp.mean(xn * xn, axis=-1, keepdims=True)
                xn = (xn * lax.rsqrt(ms + RMS_EPS)) * fw_ref[...]
            o_ref[rows, :] = xn


def _ffn(x, sc, sh, gate, w_in, w_out, final_w=None):
    bsz, seq, d = x.shape
    hidden = w_out.shape[0]
    tm = min(FFN_ROW_TILE, seq)
    th = _col_tile(hidden)
    n_h = hidden // th
    mod_spec = pl.BlockSpec((None, 1, d), lambda b, i, k: (b, 0, 0))
    in_specs = [pl.BlockSpec((None, tm, d), lambda b, i, k: (b, i, 0)),
                mod_spec, mod_spec, mod_spec,
                pl.BlockSpec((d, th), lambda b, i, k: (0, k)),
                pl.BlockSpec((d, th), lambda b, i, k: (0, n_h + k)),
                pl.BlockSpec((th, d), lambda b, i, k: (k, 0))]
    args = [x, sc, sh, gate, w_in, w_in, w_out]
    if final_w is not None:
        in_specs.append(pl.BlockSpec((1, d), lambda b, i, k: (0, 0)))
        args.append(final_w.reshape(1, d).astype(F32))
    return pl.pallas_call(
        functools.partial(_ffn_kernel, final=final_w is not None),
        grid=(bsz, seq // tm, n_h),
        in_specs=in_specs,
        out_specs=pl.BlockSpec((None, tm, d), lambda b, i, k: (b, i, 0)),
        out_shape=jax.ShapeDtypeStruct(x.shape, F32),
        scratch_shapes=[pltpu.VMEM((tm, d), MXU_DTYPE)],
        input_output_aliases={0: 0},
        compiler_params=_params(("parallel", "parallel", "arbitrary"), 60),
    )(*args)


def _rope_tables(seq, dim, query_scale):
    pos = jnp.arange(seq, dtype=F32)
    inv_freq = ROPE_THETA ** (-jnp.arange(0, dim, 2, dtype=F32) / dim)
    ang = pos[:, None] * inv_freq[None, :]
    cos, sin = jnp.cos(ang), jnp.sin(ang)
    reps = LANES // dim
    cos = jnp.tile(jnp.concatenate([cos, cos], axis=1), (1, reps))
    sin = jnp.tile(jnp.concatenate([-sin, sin], axis=1), (1, reps))
    return jnp.stack([cos * query_scale, cos]), jnp.stack([sin * query_scale, sin])


def _lambda_init(layer):
    return 0.8 - 0.6 * math.exp(-0.3 * layer)


def _dup_heads(w, n_heads, head_dim):
    lead = w.shape[:-1]
    w = w.reshape(lead + (n_heads, 1, head_dim))
    return jnp.broadcast_to(w, lead + (n_heads, 2, head_dim)).reshape(lead + (2 * n_heads * head_dim,))


def kernel(x, c, ada_w, ada_b, ab_w_in, ab_w_out, diff_lambda, diff_subln, swa_w_in, swa_b_in,
           swa_w_out, swa_b_out, swa_sinks, ffn_w_in, ffn_w_out, final_norm):
    bsz, seq, d = x.shape
    depth = ada_w.shape[0]
    a_heads, b_heads = d // 256, d // 512
    a_width, b_width = a_heads * A_HEAD_DIM, 2 * b_heads * B_HEAD_DIM
    assert a_width == b_width
    c_q_heads = d // C_HEAD_DIM
    c_kv_heads = c_q_heads // C_GROUP
    c_q_width, c_kv_width = c_q_heads * C_HEAD_DIM, c_kv_heads * C_HEAD_DIM

    rope_ab = _rope_tables(seq, A_HEAD_DIM, A_HEAD_DIM ** -0.5 * LOG2E)
    rope_c = _rope_tables(seq, C_HEAD_DIM, C_HEAD_DIM ** -0.5 * LOG2E)
    mod = _modulation(c, ada_w, ada_b)

    for layer in range(depth):
        sh1, sc1, g1, sh2, sc2, g2 = [
            mod[layer, :, None, m * d:(m + 1) * d] for m in range(N_MOD)]
        li = layer // 2
        if layer % 2 == 0:
            w_in = ab_w_in[li].astype(MXU_DTYPE)
            aq, ak, av, bq, bk, bv = jnp.split(
                w_in, [a_width, 2 * a_width, 3 * a_width,
                       3 * a_width + b_width, 3 * a_width + 2 * b_width], axis=1)
            qk = _project(x, sc1, sh1, jnp.concatenate([aq, ak, bq, bk], axis=1),
                          rope=rope_ab, rope_dim=A_HEAD_DIM,
                          is_query_col=lambda col: (col // a_width) % 2 == 0)
            vt = _project(x, sc1, sh1, jnp.concatenate([av, bv], axis=1), transpose_out=True)
            ya = _moba_attention(qk, vt, a_heads, q_col=0, k_col=a_heads, v_row=0)
            yb = _diff_attention(qk, vt, diff_lambda[li], diff_subln[li], _lambda_init(layer),
                                 b_heads, q_col=2 * a_heads, k_col=2 * a_heads + 2 * b_heads,
                                 v_row=a_width // (2 * B_HEAD_DIM))
            w_out = ab_w_out[li].astype(MXU_DTYPE)
            x = _out_project(x, g1, [ya, yb], [w_out[:a_width], w_out[a_width:]],
                             in_place=layer > 0)
        else:
            w_in, b_in = swa_w_in[li], swa_b_in[li]
            wq, wk, wv = jnp.split(w_in, [c_q_width, c_q_width + c_kv_width], axis=1)
            bq_, bk_, bv_ = jnp.split(b_in, [c_q_width, c_q_width + c_kv_width])
            w_qk = jnp.concatenate([wq, _dup_heads(wk, c_kv_heads, C_HEAD_DIM)], axis=1)
            b_qk = jnp.concatenate([bq_, _dup_heads(bk_, c_kv_heads, C_HEAD_DIM)])
            qk = _project(x, sc1, sh1, w_qk.astype(MXU_DTYPE), bias=b_qk,
                          rope=rope_c, rope_dim=C_HEAD_DIM,
                          is_query_col=lambda col: col < c_q_width)
            vt = _project(x, sc1, sh1, wv.astype(MXU_DTYPE), bias=bv_, transpose_out=True)
            y = _swa_attention(qk, vt, swa_sinks[li], c_kv_heads)
            x = _out_project(x, g1, [y], [swa_w_out[li].astype(MXU_DTYPE)], bias=swa_b_out[li])
        x = _ffn(x, sc2, sh2, g2, ffn_w_in[layer].astype(MXU_DTYPE),
                 ffn_w_out[layer].astype(MXU_DTYPE),
                 final_w=final_norm if layer == depth - 1 else None)
    return x
```

```python
import functools
import math

import jax
import jax.numpy as jnp
from jax import lax
from jax.experimental import pallas as pl
from jax.experimental.pallas import tpu as pltpu

F32 = jnp.float32
MXU_DTYPE = jnp.bfloat16

ROPE_THETA = 10000.0
RMS_EPS = 1e-6
DIFF_SUBLN_EPS = 1e-5
A_HEAD_DIM = 128
MOBA_BLOCK = 256
MOBA_BLOCK_LOG2 = 8
MOBA_TOPK = 3
B_HEAD_DIM = 128
C_HEAD_DIM = 64
C_GROUP = 8
C_WINDOW = 128
N_MOD = 6

LANES = 128
MASKED = -1e30
LOG2E = 1.4426950408889634

ATTN_TILE = 512
ROW_TILE = 512
FFN_ROW_TILE = 1024
MOD_COL_TILE = 1024
MIB = 2 ** 20

NT_DIMS = (((1,), (1,)), ((), ()))


def _params(semantics, vmem_mib):
    return pltpu.CompilerParams(dimension_semantics=semantics,
                                vmem_limit_bytes=vmem_mib * MIB)


def _col_tile(n):
    for t in (512, 256, 128):
        if n % t == 0:
            return t
    raise ValueError(f"column count {n} is not a multiple of {LANES}")


def _column_tiles(w, tn):
    k, n = w.shape
    return w.reshape(k, n // tn, tn).transpose(1, 0, 2)


def _norm_mod(x, sc, sh):
    ms = jnp.mean(x * x, axis=-1, keepdims=True)
    return (x * lax.rsqrt(ms + RMS_EPS)) * (1.0 + sc) + sh


def _mod_kernel(c_ref, w_ref, b_ref, o_ref):
    c = c_ref[...]
    cond = (c * jax.nn.sigmoid(c)).astype(MXU_DTYPE)
    o_ref[...] = jnp.dot(cond, w_ref[...].astype(MXU_DTYPE),
                         preferred_element_type=F32) + b_ref[...]


def _modulation(c, ada_w, ada_b):
    depth, d, n = ada_w.shape
    bsz = c.shape[0]
    rows = -(-bsz // 8) * 8
    c_pad = jnp.pad(c, ((0, rows - bsz), (0, 0)))
    tn = MOD_COL_TILE if n % MOD_COL_TILE == 0 else _col_tile(n)
    out = pl.pallas_call(
        _mod_kernel,
        grid=(depth, n // tn),
        in_specs=[pl.BlockSpec((rows, d), lambda l, j: (0, 0)),
                  pl.BlockSpec((None, d, tn), lambda l, j: (l, 0, j)),
                  pl.BlockSpec((None, 1, tn), lambda l, j: (l, 0, j))],
        out_specs=pl.BlockSpec((None, rows, tn), lambda l, j: (l, 0, j)),
        out_shape=jax.ShapeDtypeStruct((depth, rows, n), F32),
        compiler_params=_params(("parallel", "parallel"), 40),
    )(c_pad, ada_w, ada_b.reshape(depth, 1, n))
    return out[:, :bsz]


def _rope_partner(a, rope_dim):
    if rope_dim == LANES:
        return pltpu.roll(a, LANES // 2, 1)
    lane = lax.broadcasted_iota(jnp.int32, a.shape, 1)
    half = rope_dim // 2
    first_half = (lane & (rope_dim - 1)) < half
    return jnp.where(first_half, pltpu.roll(a, LANES - half, 1), pltpu.roll(a, half, 1))


def _proj_kernel(x_ref, sc_ref, sh_ref, w_ref, *rest, rope_dim, has_bias, transpose_out,
                 query_tiles):
    rest = list(rest)
    b_ref = rest.pop(0) if has_bias else None
    cos_ref, sin_ref = (rest.pop(0), rest.pop(0)) if rope_dim else (None, None)
    (o_ref,) = rest
    n_t, _, tn = w_ref.shape
    h = _norm_mod(x_ref[...], sc_ref[...], sh_ref[...]).astype(w_ref.dtype)
    for j in range(n_t):
        cols = slice(j * tn, (j + 1) * tn)
        acc = jnp.dot(h, w_ref[j], preferred_element_type=F32)
        if has_bias:
            acc = acc + b_ref[:, cols]
        if rope_dim:
            table = 0 if query_tiles[j] else 1
            cos, sin = cos_ref[table], sin_ref[table]
            for t in range(tn // LANES):
                a = acc[:, t * LANES:(t + 1) * LANES]
                o_ref[:, j * tn + t * LANES:j * tn + (t + 1) * LANES] = (
                    a * cos + _rope_partner(a, rope_dim) * sin).astype(o_ref.dtype)
        elif transpose_out:
            o_ref[cols, :] = acc.T.astype(o_ref.dtype)
        else:
            o_ref[:, cols] = acc.astype(o_ref.dtype)


def _project(x, sc, sh, w, bias=None, rope=None, rope_dim=0, is_query_col=None,
             transpose_out=False):
    bsz, seq, d = x.shape
    n = w.shape[1]
    tm = min(ROW_TILE, seq)
    tn = _col_tile(n)
    n_t = n // tn
    in_specs = [pl.BlockSpec((None, tm, d), lambda b, i: (b, i, 0)),
                pl.BlockSpec((None, 1, d), lambda b, i: (b, 0, 0)),
                pl.BlockSpec((None, 1, d), lambda b, i: (b, 0, 0)),
                pl.BlockSpec((n_t, d, tn), lambda b, i: (0, 0, 0), pipeline_mode=pl.Buffered(1))]
    args = [x, sc, sh, _column_tiles(w, tn)]
    if bias is not None:
        in_specs.append(pl.BlockSpec((1, n), lambda b, i: (0, 0)))
        args.append(bias.reshape(1, n).astype(F32))
    query_tiles = None
    if rope_dim:
        query_tiles = tuple(bool(is_query_col(j * tn)) for j in range(n_t))
        in_specs += [pl.BlockSpec((2, tm, LANES), lambda b, i: (0, i, 0))] * 2
        args += list(rope)
    if transpose_out:
        out_shape = jax.ShapeDtypeStruct((bsz, seq // tm, n, tm), MXU_DTYPE)
        out_spec = pl.BlockSpec((None, None, n, tm), lambda b, i: (b, i, 0, 0))
    else:
        out_shape = jax.ShapeDtypeStruct((bsz, seq, n), MXU_DTYPE)
        out_spec = pl.BlockSpec((None, tm, n), lambda b, i: (b, i, 0))
    return pl.pallas_call(
        functools.partial(_proj_kernel, rope_dim=rope_dim, has_bias=bias is not None,
                          transpose_out=transpose_out, query_tiles=query_tiles),
        grid=(bsz, seq // tm),
        in_specs=in_specs,
        out_specs=out_spec,
        out_shape=out_shape,
        compiler_params=_params(("parallel", "parallel"), 48),
    )(*args)


N_FLASH_REFS = 10
SLAB_ROWS = 64
ONES_ROWS = 16


def _flash_scratch(t, v_dim):
    stat = pltpu.VMEM((1, t), F32)
    scores = pltpu.VMEM((t, t), F32)
    probs = pltpu.VMEM((t, t), MXU_DTYPE)
    return [stat, pltpu.VMEM((v_dim + ONES_ROWS, t), F32),
            scores, scores, stat, stat, probs, probs, stat, stat]


class _FlashStream:
    def __init__(self, refs):
        self.m, self.acc = refs[:2]
        self.s, self.s_max, self.p, self.alpha = refs[2:4], refs[4:6], refs[6:8], refs[8:10]

    def result(self):
        v_dim = self.acc.shape[0] - ONES_ROWS
        return self.acc[:v_dim, :] / self.acc[v_dim:v_dim + 1, :]


def _col_max(x):
    slab = x[:SLAB_ROWS]
    for r in range(SLAB_ROWS, x.shape[0], SLAB_ROWS):
        slab = jnp.maximum(slab, x[r:r + SLAB_ROWS])
    return jnp.max(slab, axis=0, keepdims=True)


def _causal_flash(i, streams, scores_of, values_of):
    for st in streams:
        st.m[...] = jnp.full_like(st.m, MASKED)
        st.acc[...] = jnp.zeros_like(st.acc)
        st.alpha[1][...] = jnp.ones_like(st.alpha[1])
        st.p[1][...] = jnp.zeros_like(st.p[1])

    def put_scores(st, slot, s):
        st.s[slot][...] = s
        st.s_max[slot][...] = _col_max(s)

    for idx, st in enumerate(streams):
        s = scores_of(idx, i)
        krow = lax.broadcasted_iota(jnp.int32, s.shape, 0)
        qcol = lax.broadcasted_iota(jnp.int32, s.shape, 1)
        put_scores(st, 0, jnp.where(krow <= qcol, s, MASKED))
    n_steps = i + 1
    last_past = jnp.maximum(i - 1, 0)

    def values_step(k, slot):
        tile = jnp.where(k <= 0, i, k - 1)
        for idx, st in enumerate(streams):
            vt = values_of(idx, tile)
            vt = jnp.concatenate([vt, jnp.ones((ONES_ROWS, vt.shape[1]), vt.dtype)], axis=0)
            st.acc[...] = st.alpha[slot][...] * st.acc[...] + jnp.dot(
                vt, st.p[slot][...], preferred_element_type=F32)

    def step(k, cur):
        nxt = 1 - cur
        ahead = jnp.minimum(k, last_past)
        for st in streams:
            m_prev = st.m[...]
            m_new = jnp.maximum(m_prev, st.s_max[cur][...])
            for r in range(0, st.s[cur].shape[0], SLAB_ROWS):
                p = jnp.exp2(st.s[cur][r:r + SLAB_ROWS, :] - m_new)
                st.p[cur][r:r + SLAB_ROWS, :] = p.astype(st.p[cur].dtype)
            st.m[...] = m_new
            st.alpha[cur][...] = jnp.exp2(m_prev - m_new)
        for idx, st in enumerate(streams):
            put_scores(st, nxt, scores_of(idx, ahead))
        values_step(k - 1, nxt)

    def pair_body(kk, carry):
        step(2 * kk, 0)
        step(2 * kk + 1, 1)
        return carry

    lax.fori_loop(0, n_steps // 2, pair_body, 0)
    odd = n_steps % 2 == 1

    @pl.when(odd)
    def _():
        step(n_steps - 1, 0)
        values_step(n_steps - 1, 0)

    @pl.when(jnp.logical_not(odd))
    def _():
        values_step(n_steps - 1, 1)


MOBA_HEADS_PER_STEP = 2


def _moba_kernel(q_ref, k_ref, vt_ref, o_ref, kmean_ref, qaug_ref, *state, n_blocks):
    i = pl.program_id(2)
    tq = q_ref.shape[0]
    tk = tq
    hd = A_HEAD_DIM
    heads = [slice(e * hd, (e + 1) * hd) for e in range(MOBA_HEADS_PER_STEP)]

    @pl.when(i == 0)
    def _():
        kmean_ref[...] = jnp.zeros_like(kmean_ref)

        def mean_body(n, carry):
            kb = k_ref[pl.ds(pl.multiple_of(n * MOBA_BLOCK, MOBA_BLOCK), MOBA_BLOCK), :]
            mean = jnp.sum(kb.astype(F32), axis=0, keepdims=True) * (1.0 / MOBA_BLOCK)
            for e, cols in enumerate(heads):
                kmean_ref[e, pl.ds(n, 1), :] = mean[:, cols]
            return carry

        lax.fori_loop(0, n_blocks, mean_body, 0)

    col = lax.broadcasted_iota(jnp.int32, (tq, LANES), 1)
    row = lax.broadcasted_iota(jnp.int32, (tq, LANES), 0)
    own = jnp.right_shift(i * tq + row, MOBA_BLOCK_LOG2)
    past = col < own
    colf = col.astype(F32)
    for e, cols in enumerate(heads):
        q = q_ref[:, cols]
        km = kmean_ref[e]
        k_hi = km.astype(MXU_DTYPE)
        r1 = km - k_hi.astype(F32)
        k_mid = r1.astype(MXU_DTYPE)
        k_lo = (r1 - k_mid.astype(F32)).astype(MXU_DTYPE)
        gate = (lax.dot_general(q, k_lo, NT_DIMS, preferred_element_type=F32)
                + lax.dot_general(q, k_mid, NT_DIMS, preferred_element_type=F32)
                + lax.dot_general(q, k_hi, NT_DIMS, preferred_element_type=F32))
        g = jnp.where(past, gate, -jnp.inf)
        chosen = jnp.zeros(gate.shape, jnp.bool_)
        for _ in range(min(MOBA_TOPK, n_blocks)):
            best = jnp.max(g, axis=1, keepdims=True)
            first_best = jnp.min(jnp.where(g == best, colf, float(LANES)), axis=1, keepdims=True)
            pick = colf == first_best
            chosen = jnp.logical_or(chosen, pick)
            g = jnp.where(pick, -jnp.inf, g)
        visible = jnp.logical_or(jnp.logical_and(chosen, past), col == own)
        qaug_ref[e, :, :hd] = q
        qaug_ref[e, :, hd:] = jnp.where(visible, 0.0, MASKED).astype(qaug_ref.dtype)

    def scores_of(e, j):
        kt = k_ref[pl.ds(pl.multiple_of(j * tk, tk), tk), heads[e]]
        lane = lax.broadcasted_iota(jnp.int32, (1, LANES), 1)
        blocks_per_tile = tk // MOBA_BLOCK
        onehot = jnp.concatenate(
            [jnp.broadcast_to(jnp.where(lane == j * blocks_per_tile + b, 1.0, 0.0),
                              (MOBA_BLOCK, LANES)) for b in range(blocks_per_tile)], axis=0)
        k_aug = jnp.concatenate([kt, onehot.astype(kt.dtype)], axis=1)
        return lax.dot_general(k_aug, qaug_ref[e], NT_DIMS, preferred_element_type=F32)

    streams = [_FlashStream(state[e * N_FLASH_REFS:(e + 1) * N_FLASH_REFS])
               for e in range(MOBA_HEADS_PER_STEP)]
    _causal_flash(i, streams, scores_of, lambda e, j: vt_ref[j, heads[e], :])
    for e, st in enumerate(streams):
        o_ref[:, heads[e]] = st.result().T.astype(o_ref.dtype)


def _moba_attention(qk, vt, n_heads, q_col, k_col, v_row):
    bsz, seq, _ = qk.shape
    t = min(ATTN_TILE, seq)
    n_tiles = seq // t
    n_blocks = seq // MOBA_BLOCK
    per = MOBA_HEADS_PER_STEP
    assert seq % t == 0 and t % MOBA_BLOCK == 0 and n_blocks <= LANES
    assert n_heads % per == 0 and q_col % per == 0 and k_col % per == 0 and v_row % per == 0
    hd = A_HEAD_DIM
    scratch = [pltpu.VMEM((per, LANES, hd), F32), pltpu.VMEM((per, t, 2 * hd), MXU_DTYPE)]
    for _ in range(per):
        scratch += _flash_scratch(t, hd)
    return pl.pallas_call(
        functools.partial(_moba_kernel, n_blocks=n_blocks),
        grid=(bsz, n_heads // per, n_tiles),
        in_specs=[pl.BlockSpec((None, t, per * hd), lambda b, h, i: (b, i, q_col // per + h)),
                  pl.BlockSpec((None, seq, per * hd), lambda b, h, i: (b, 0, k_col // per + h)),
                  pl.BlockSpec((None, n_tiles, per * hd, t),
                               lambda b, h, i: (b, 0, v_row // per + h, 0))],
        out_specs=pl.BlockSpec((None, t, per * hd), lambda b, h, i: (b, i, h)),
        out_shape=jax.ShapeDtypeStruct((bsz, seq, n_heads * hd), MXU_DTYPE),
        scratch_shapes=scratch,
        compiler_params=_params(("parallel", "parallel", "arbitrary"), 56),
    )(qk, qk, vt)


def _diff_kernel(q1_ref, q2_ref, k1_ref, k2_ref, vt_ref, lam_ref, w_ref, o_ref, *state,
                 lambda_init):
    i = pl.program_id(2)
    tk = q1_ref.shape[0]
    map1 = _FlashStream(state[:N_FLASH_REFS])
    map2 = _FlashStream(state[N_FLASH_REFS:])
    qk_refs = ((q1_ref, k1_ref), (q2_ref, k2_ref))

    def scores_of(idx, j):
        q_ref, k_ref = qk_refs[idx]
        rows = pl.ds(pl.multiple_of(j * tk, tk), tk)
        return lax.dot_general(k_ref[rows, :], q_ref[...], NT_DIMS, preferred_element_type=F32)

    _causal_flash(i, [map1, map2], scores_of, lambda idx, j: vt_ref[j])

    lv = lam_ref[...]
    lam = (jnp.exp(jnp.sum(lv[0:1] * lv[1:2], axis=1, keepdims=True))
           - jnp.exp(jnp.sum(lv[2:3] * lv[3:4], axis=1, keepdims=True)) + lambda_init)
    y = map1.result() - lam * map2.result()
    ms = jnp.mean(y * y, axis=0, keepdims=True)
    y = (y * lax.rsqrt(ms + DIFF_SUBLN_EPS)) * w_ref[...]
    o_ref[...] = (y * (1.0 - lambda_init)).T.astype(o_ref.dtype)


def _diff_attention(qk, vt, lam_vecs, subln_w, lambda_init, n_heads, q_col, k_col, v_row):
    bsz, seq, _ = qk.shape
    t = min(ATTN_TILE, seq)
    n_tiles = seq // t
    hd = B_HEAD_DIM
    return pl.pallas_call(
        functools.partial(_diff_kernel, lambda_init=lambda_init),
        grid=(bsz, n_heads, n_tiles),
        in_specs=[pl.BlockSpec((None, t, hd), lambda b, h, i: (b, i, q_col + 2 * h)),
                  pl.BlockSpec((None, t, hd), lambda b, h, i: (b, i, q_col + 2 * h + 1)),
                  pl.BlockSpec((None, seq, hd), lambda b, h, i: (b, 0, k_col + 2 * h)),
                  pl.BlockSpec((None, seq, hd), lambda b, h, i: (b, 0, k_col + 2 * h + 1)),
                  pl.BlockSpec((None, n_tiles, 2 * hd, t), lambda b, h, i: (b, 0, v_row + h, 0)),
                  pl.BlockSpec((4, hd), lambda b, h, i: (0, 0)),
                  pl.BlockSpec((2 * hd, 1), lambda b, h, i: (0, 0))],
        out_specs=pl.BlockSpec((None, t, 2 * hd), lambda b, h, i: (b, i, h)),
        out_shape=jax.ShapeDtypeStruct((bsz, seq, n_heads * 2 * hd), MXU_DTYPE),
        scratch_shapes=_flash_scratch(t, 2 * hd) + _flash_scratch(t, 2 * hd),
        compiler_params=_params(("parallel", "parallel", "arbitrary"), 56),
    )(qk, qk, qk, qk, vt, lam_vecs.astype(F32), subln_w.astype(F32).reshape(2 * hd, 1))


def _swa_kernel(q_ref, kc_ref, kp_ref, vtc_ref, vtp_ref, sink_ref, o_ref):
    i = pl.program_id(1)
    w = C_WINDOW
    hd = C_HEAD_DIM
    gw = C_GROUP * hd
    lane = lax.broadcasted_iota(jnp.int32, (w, LANES), 1)
    lo = jnp.where(lane < hd, 1.0, 0.0).astype(q_ref.dtype)
    hi = jnp.where(lane >= hd, 1.0, 0.0).astype(q_ref.dtype)
    krow = lax.broadcasted_iota(jnp.int32, (2 * w, C_GROUP * w), 0)
    qcol = lax.broadcasted_iota(jnp.int32, (2 * w, C_GROUP * w), 1) & (w - 1)
    rel = w + qcol - krow
    valid = (rel >= 0) & (rel < w) & ((i - 1) * w + krow >= 0)
    ones = jnp.ones((ONES_ROWS, 2 * w), vtc_ref.dtype)
    for g in range(q_ref.shape[1] // gw):
        k_cols = slice(g * LANES, (g + 1) * LANES)
        v_rows = slice(g * hd, (g + 1) * hd)
        kk = jnp.concatenate([kp_ref[:, k_cols], kc_ref[:, k_cols]], axis=0)
        vt = jnp.concatenate([vtp_ref[v_rows, :], vtc_ref[v_rows, :]], axis=1)
        q_heads = []
        for p in range(C_GROUP // 2):
            q2 = q_ref[:, g * gw + p * LANES:g * gw + (p + 1) * LANES]
            q_heads += [q2 * lo, q2 * hi]
        q_all = jnp.concatenate(q_heads, axis=0)
        s = lax.dot_general(kk, q_all, NT_DIMS, preferred_element_type=F32)
        s = jnp.where(valid, s, MASKED)
        sink = sink_ref[g] * LOG2E
        m = jnp.maximum(_col_max(s), sink)
        e = jnp.concatenate([jnp.exp2(s[r:r + SLAB_ROWS] - m).astype(vt.dtype)
                             for r in range(0, 2 * w, SLAB_ROWS)], axis=0)
        o = jnp.dot(jnp.concatenate([vt, ones], axis=0), e, preferred_element_type=F32)
        o = o[:hd] / (o[hd:hd + 1] + jnp.exp2(sink - m))
        o = jnp.concatenate([o[:, h * w:(h + 1) * w] for h in range(C_GROUP)], axis=0)
        o_ref[:, g * gw:(g + 1) * gw] = o.T.astype(o_ref.dtype)


def _swa_attention(qk, vt, sinks, n_kv):
    bsz, seq, _ = qk.shape
    w = C_WINDOW
    q_width = n_kv * C_GROUP * C_HEAD_DIM
    k_width = n_kv * LANES
    assert q_width % k_width == 0
    per_tile = vt.shape[-1] // w
    sink_cols = jnp.repeat(sinks.astype(F32).reshape(n_kv, 1, C_GROUP), w, axis=-1)
    prev = lambda i: jnp.maximum(i - 1, 0)
    vt_spec = lambda at: pl.BlockSpec(
        (None, None, n_kv * C_HEAD_DIM, w),
        lambda b, i: (b, at(i) // per_tile, 0, at(i) % per_tile))
    return pl.pallas_call(
        _swa_kernel,
        grid=(bsz, seq // w),
        in_specs=[pl.BlockSpec((None, w, q_width), lambda b, i: (b, i, 0)),
                  pl.BlockSpec((None, w, k_width), lambda b, i: (b, i, q_width // k_width)),
                  pl.BlockSpec((None, w, k_width), lambda b, i: (b, prev(i), q_width // k_width)),
                  vt_spec(lambda i: i),
                  vt_spec(prev),
                  pl.BlockSpec((n_kv, 1, C_GROUP * w), lambda b, i: (0, 0, 0))],
        out_specs=pl.BlockSpec((None, w, q_width), lambda b, i: (b, i, 0)),
        out_shape=jax.ShapeDtypeStruct((bsz, seq, q_width), MXU_DTYPE),
        compiler_params=_params(("parallel", "parallel"), 32),
    )(qk, qk, qk, vt, vt, sink_cols)


def _outproj_kernel(*refs, n_parts, has_bias):
    y_refs, w_refs = refs[:n_parts], refs[n_parts:2 * n_parts]
    rest = list(refs[2 * n_parts:])
    b_ref = rest.pop(0) if has_bias else None
    g_ref, x_ref, o_ref = rest
    acc = jnp.dot(y_refs[0][...], w_refs[0][...], preferred_element_type=F32)
    for y_ref, w_ref in zip(y_refs[1:], w_refs[1:]):
        acc = acc + jnp.dot(y_ref[...], w_ref[...], preferred_element_type=F32)
    if has_bias:
        acc = acc + b_ref[...]
    o_ref[...] = x_ref[...] + g_ref[...] * acc


def _out_project(x, gate, ys, ws, bias=None, in_place=True):
    bsz, seq, d = x.shape
    tm = min(ROW_TILE, seq)
    in_specs = [pl.BlockSpec((None, tm, y.shape[-1]), lambda b, i: (b, i, 0)) for y in ys]
    in_specs += [pl.BlockSpec(w.shape, lambda b, i: (0, 0)) for w in ws]
    args = list(ys) + list(ws)
    if bias is not None:
        in_specs.append(pl.BlockSpec((1, d), lambda b, i: (0, 0)))
        args.append(bias.reshape(1, d).astype(F32))
    in_specs += [pl.BlockSpec((None, 1, d), lambda b, i: (b, 0, 0)),
                 pl.BlockSpec((None, tm, d), lambda b, i: (b, i, 0))]
    args += [gate, x]
    return pl.pallas_call(
        functools.partial(_outproj_kernel, n_parts=len(ys), has_bias=bias is not None),
        grid=(bsz, seq // tm),
        in_specs=in_specs,
        out_specs=pl.BlockSpec((None, tm, d), lambda b, i: (b, i, 0)),
        out_shape=jax.ShapeDtypeStruct(x.shape, F32),
        input_output_aliases={len(args) - 1: 0} if in_place else {},
        compiler_params=_params(("parallel", "parallel"), 48),
    )(*args)


def _ffn_kernel(x_ref, sc_ref, sh_ref, g_ref, wg_ref, wu_ref, wo_ref, *rest, final):
    rest = list(rest)
    fw_ref = rest.pop(0) if final else None
    o_ref, h_ref = rest
    k = pl.program_id(2)
    tm = x_ref.shape[0]
    halves = [slice(r, r + min(ROW_TILE, tm)) for r in range(0, tm, ROW_TILE)]

    @pl.when(k == 0)
    def _():
        for rows in halves:
            h_ref[rows, :] = _norm_mod(x_ref[rows, :], sc_ref[...], sh_ref[...]).astype(h_ref.dtype)
        o_ref[...] = jnp.zeros_like(o_ref)

    for rows in halves:
        h = h_ref[rows, :]
        gt = jnp.dot(h, wg_ref[...], preferred_element_type=F32)
        up = jnp.dot(h, wu_ref[...], preferred_element_type=F32)
        act = ((gt * jax.nn.sigmoid(gt)) * up).astype(h.dtype)
        o_ref[rows, :] += jnp.dot(act, wo_ref[...], preferred_element_type=F32)

    @pl.when(k == pl.num_programs(2) - 1)
    def _():
        for rows in halves:
            xn = x_ref[rows, :] + g_ref[...] * o_ref[rows, :]
            if final:
                ms = jnp.mean(xn * xn, axis=-1, keepdims=True)
                xn = (xn * lax.rsqrt(ms + RMS_EPS)) * fw_ref[...]
            o_ref[rows, :] = xn


def _ffn(x, sc, sh, gate, w_in, w_out, final_w=None):
    bsz, seq, d = x.shape
    hidden = w_out.shape[0]
    tm = min(FFN_ROW_TILE, seq)
    th = _col_tile(hidden)
    n_h = hidden // th
    mod_spec = pl.BlockSpec((None, 1, d), lambda b, i, k: (b, 0, 0))
    in_specs = [pl.BlockSpec((None, tm, d), lambda b, i, k: (b, i, 0)),
                mod_spec, mod_spec, mod_spec,
                pl.BlockSpec((d, th), lambda b, i, k: (0, k)),
                pl.BlockSpec((d, th), lambda b, i, k: (0, n_h + k)),
                pl.BlockSpec((th, d), lambda b, i, k: (k, 0))]
    args = [x, sc, sh, gate, w_in, w_in, w_out]
    if final_w is not None:
        in_specs.append(pl.BlockSpec((1, d), lambda b, i, k: (0, 0)))
        args.append(final_w.reshape(1, d).astype(F32))
    return pl.pallas_call(
        functools.partial(_ffn_kernel, final=final_w is not None),
        grid=(bsz, seq // tm, n_h),
        in_specs=in_specs,
        out_specs=pl.BlockSpec((None, tm, d), lambda b, i, k: (b, i, 0)),
        out_shape=jax.ShapeDtypeStruct(x.shape, F32),
        scratch_shapes=[pltpu.VMEM((tm, d), MXU_DTYPE)],
        input_output_aliases={0: 0},
        compiler_params=_params(("parallel", "parallel", "arbitrary"), 60),
    )(*args)


def _rope_tables(seq, dim, query_scale):
    pos = jnp.arange(seq, dtype=F32)
    inv_freq = ROPE_THETA ** (-jnp.arange(0, dim, 2, dtype=F32) / dim)
    ang = pos[:, None] * inv_freq[None, :]
    cos, sin = jnp.cos(ang), jnp.sin(ang)
    reps = LANES // dim
    cos = jnp.tile(jnp.concatenate([cos, cos], axis=1), (1, reps))
    sin = jnp.tile(jnp.concatenate([-sin, sin], axis=1), (1, reps))
    return jnp.stack([cos * query_scale, cos]), jnp.stack([sin * query_scale, sin])


def _lambda_init(layer):
    return 0.8 - 0.6 * math.exp(-0.3 * layer)


def _dup_heads(w, n_heads, head_dim):
    lead = w.shape[:-1]
    w = w.reshape(lead + (n_heads, 1, head_dim))
    return jnp.broadcast_to(w, lead + (n_heads, 2, head_dim)).reshape(lead + (2 * n_heads * head_dim,))


def kernel(x, c, ada_w, ada_b, ab_w_in, ab_w_out, diff_lambda, diff_subln, swa_w_in, swa_b_in,
           swa_w_out, swa_b_out, swa_sinks, ffn_w_in, ffn_w_out, final_norm):
    bsz, seq, d = x.shape
    depth = ada_w.shape[0]
    a_heads, b_heads = d // 256, d // 512
    a_width, b_width = a_heads * A_HEAD_DIM, 2 * b_heads * B_HEAD_DIM
    assert a_width == b_width
    c_q_heads = d // C_HEAD_DIM
    c_kv_heads = c_q_heads // C_GROUP
    c_q_width, c_kv_width = c_q_heads * C_HEAD_DIM, c_kv_heads * C_HEAD_DIM

    rope_ab = _rope_tables(seq, A_HEAD_DIM, A_HEAD_DIM ** -0.5 * LOG2E)
    rope_c = _rope_tables(seq, C_HEAD_DIM, C_HEAD_DIM ** -0.5 * LOG2E)
    mod = _modulation(c, ada_w, ada_b)

    for layer in range(depth):
        sh1, sc1, g1, sh2, sc2, g2 = [
            mod[layer, :, None, m * d:(m + 1) * d] for m in range(N_MOD)]
        li = layer // 2
        if layer % 2 == 0:
            w_in = ab_w_in[li].astype(MXU_DTYPE)
            aq, ak, av, bq, bk, bv = jnp.split(
                w_in, [a_width, 2 * a_width, 3 * a_width,
                       3 * a_width + b_width, 3 * a_width + 2 * b_width], axis=1)
            qk = _project(x, sc1, sh1, jnp.concatenate([aq, ak, bq, bk], axis=1),
                          rope=rope_ab, rope_dim=A_HEAD_DIM,
                          is_query_col=lambda col: (col // a_width) % 2 == 0)
            vt = _project(x, sc1, sh1, jnp.concatenate([av, bv], axis=1), transpose_out=True)
            ya = _moba_attention(qk, vt, a_heads, q_col=0, k_col=a_heads, v_row=0)
            yb = _diff_attention(qk, vt, diff_lambda[li], diff_subln[li], _lambda_init(layer),
                                 b_heads, q_col=2 * a_heads, k_col=2 * a_heads + 2 * b_heads,
                                 v_row=a_width // (2 * B_HEAD_DIM))
            w_out = ab_w_out[li].astype(MXU_DTYPE)
            x = _out_project(x, g1, [ya, yb], [w_out[:a_width], w_out[a_width:]],
                             in_place=layer > 0)
        else:
            w_in, b_in = swa_w_in[li], swa_b_in[li]
            wq, wk, wv = jnp.split(w_in, [c_q_width, c_q_width + c_kv_width], axis=1)
            bq_, bk_, bv_ = jnp.split(b_in, [c_q_width, c_q_width + c_kv_width])
            w_qk = jnp.concatenate([wq, _dup_heads(wk, c_kv_heads, C_HEAD_DIM)], axis=1)
            b_qk = jnp.concatenate([bq_, _dup_heads(bk_, c_kv_heads, C_HEAD_DIM)])
            qk = _project(x, sc1, sh1, w_qk.astype(MXU_DTYPE), bias=b_qk,
                          rope=rope_c, rope_dim=C_HEAD_DIM,
                          is_query_col=lambda col: col < c_q_width)
            vt = _project(x, sc1, sh1, wv.astype(MXU_DTYPE), bias=bv_, transpose_out=True)
            y = _swa_attention(qk, vt, swa_sinks[li], c_kv_heads)
            x = _out_project(x, g1, [y], [swa_w_out[li].astype(MXU_DTYPE)], bias=swa_b_out[li])
        x = _ffn(x, sc2, sh2, g2, ffn_w_in[layer].astype(MXU_DTYPE),
                 ffn_w_out[layer].astype(MXU_DTYPE),
                 final_w=final_norm if layer == depth - 1 else None)
    return x
```

```python
import functools
import math

import jax
import jax.numpy as jnp
from jax import lax
from jax.experimental import pallas as pl
from jax.experimental.pallas import tpu as pltpu

F32 = jnp.float32
MXU_DTYPE = jnp.bfloat16

ROPE_THETA = 10000.0
RMS_EPS = 1e-6
DIFF_SUBLN_EPS = 1e-5
A_HEAD_DIM = 128
MOBA_BLOCK = 256
MOBA_BLOCK_LOG2 = 8
MOBA_TOPK = 3
B_HEAD_DIM = 128
C_HEAD_DIM = 64
C_GROUP = 8
C_WINDOW = 128
N_MOD = 6

LANES = 128
MASKED = -1e30
LOG2E = 1.4426950408889634

ATTN_TILE = 512
ROW_TILE = 512
FFN_ROW_TILE = 1024
MOD_COL_TILE = 1024
MIB = 2 ** 20

NT_DIMS = (((1,), (1,)), ((), ()))


def _params(semantics, vmem_mib):
    return pltpu.CompilerParams(dimension_semantics=semantics,
                                vmem_limit_bytes=vmem_mib * MIB)


def _col_tile(n):
    for t in (512, 256, 128):
        if n % t == 0:
            return t
    raise ValueError(f"column count {n} is not a multiple of {LANES}")


def _column_tiles(w, tn):
    k, n = w.shape
    return w.reshape(k, n // tn, tn).transpose(1, 0, 2)


def _norm_mod(x, sc, sh):
    ms = jnp.mean(x * x, axis=-1, keepdims=True)
    return (x * lax.rsqrt(ms + RMS_EPS)) * (1.0 + sc) + sh


def _mod_kernel(c_ref, w_ref, b_ref, o_ref):
    c = c_ref[...]
    cond = (c * jax.nn.sigmoid(c)).astype(MXU_DTYPE)
    o_ref[...] = jnp.dot(cond, w_ref[...].astype(MXU_DTYPE),
                         preferred_element_type=F32) + b_ref[...]


def _modulation(c, ada_w, ada_b):
    depth, d, n = ada_w.shape
    bsz = c.shape[0]
    rows = -(-bsz // 8) * 8
    c_pad = jnp.pad(c, ((0, rows - bsz), (0, 0)))
    tn = MOD_COL_TILE if n % MOD_COL_TILE == 0 else _col_tile(n)
    out = pl.pallas_call(
        _mod_kernel,
        grid=(depth, n // tn),
        in_specs=[pl.BlockSpec((rows, d), lambda l, j: (0, 0)),
                  pl.BlockSpec((None, d, tn), lambda l, j: (l, 0, j)),
                  pl.BlockSpec((None, 1, tn), lambda l, j: (l, 0, j))],
        out_specs=pl.BlockSpec((None, rows, tn), lambda l, j: (l, 0, j)),
        out_shape=jax.ShapeDtypeStruct((depth, rows, n), F32),
        compiler_params=_params(("parallel", "parallel"), 40),
    )(c_pad, ada_w, ada_b.reshape(depth, 1, n))
    return out[:, :bsz]


def _rope_partner(a, rope_dim):
    if rope_dim == LANES:
        return pltpu.roll(a, LANES // 2, 1)
    lane = lax.broadcasted_iota(jnp.int32, a.shape, 1)
    half = rope_dim // 2
    first_half = (lane & (rope_dim - 1)) < half
    return jnp.where(first_half, pltpu.roll(a, LANES - half, 1), pltpu.roll(a, half, 1))


def _proj_kernel(x_ref, sc_ref, sh_ref, w_ref, *rest, rope_dim, has_bias, transpose_out,
                 query_tiles):
    rest = list(rest)
    b_ref = rest.pop(0) if has_bias else None
    cos_ref, sin_ref = (rest.pop(0), rest.pop(0)) if rope_dim else (None, None)
    (o_ref,) = rest
    n_t, _, tn = w_ref.shape
    h = _norm_mod(x_ref[...], sc_ref[...], sh_ref[...]).astype(w_ref.dtype)
    for j in range(n_t):
        cols = slice(j * tn, (j + 1) * tn)
        acc = jnp.dot(h, w_ref[j], preferred_element_type=F32)
        if has_bias:
            acc = acc + b_ref[:, cols]
        if rope_dim:
            table = 0 if query_tiles[j] else 1
            cos, sin = cos_ref[table], sin_ref[table]
            for t in range(tn // LANES):
                a = acc[:, t * LANES:(t + 1) * LANES]
                o_ref[:, j * tn + t * LANES:j * tn + (t + 1) * LANES] = (
                    a * cos + _rope_partner(a, rope_dim) * sin).astype(o_ref.dtype)
        elif transpose_out:
            o_ref[cols, :] = acc.T.astype(o_ref.dtype)
        else:
            o_ref[:, cols] = acc.astype(o_ref.dtype)


def _project(x, sc, sh, w, bias=None, rope=None, rope_dim=0, is_query_col=None,
             transpose_out=False):
    bsz, seq, d = x.shape
    n = w.shape[1]
    tm = min(ROW_TILE, seq)
    tn = _col_tile(n)
    n_t = n // tn
    in_specs = [pl.BlockSpec((None, tm, d), lambda b, i: (b, i, 0)),
                pl.BlockSpec((None, 1, d), lambda b, i: (b, 0, 0)),
                pl.BlockSpec((None, 1, d), lambda b, i: (b, 0, 0)),
                pl.BlockSpec((n_t, d, tn), lambda b, i: (0, 0, 0), pipeline_mode=pl.Buffered(1))]
    args = [x, sc, sh, _column_tiles(w, tn)]
    if bias is not None:
        in_specs.append(pl.BlockSpec((1, n), lambda b, i: (0, 0)))
        args.append(bias.reshape(1, n).astype(F32))
    query_tiles = None
    if rope_dim:
        query_tiles = tuple(bool(is_query_col(j * tn)) for j in range(n_t))
        in_specs += [pl.BlockSpec((2, tm, LANES), lambda b, i: (0, i, 0))] * 2
        args += list(rope)
    if transpose_out:
        out_shape = jax.ShapeDtypeStruct((bsz, seq // tm, n, tm), MXU_DTYPE)
        out_spec = pl.BlockSpec((None, None, n, tm), lambda b, i: (b, i, 0, 0))
    else:
        out_shape = jax.ShapeDtypeStruct((bsz, seq, n), MXU_DTYPE)
        out_spec = pl.BlockSpec((None, tm, n), lambda b, i: (b, i, 0))
    return pl.pallas_call(
        functools.partial(_proj_kernel, rope_dim=rope_dim, has_bias=bias is not None,
                          transpose_out=transpose_out, query_tiles=query_tiles),
        grid=(bsz, seq // tm),
        in_specs=in_specs,
        out_specs=out_spec,
        out_shape=out_shape,
        compiler_params=_params(("parallel", "parallel"), 48),
    )(*args)


N_FLASH_REFS = 10
SLAB_ROWS = 64
ONES_ROWS = 16
FLASH_UNROLL = 2
FLASH_GROUP = 2


def _flash_scratch(t, v_dim):
    stat = pltpu.VMEM((1, t), F32)
    scores = pltpu.VMEM((t, t), F32)
    probs = pltpu.VMEM((t, t), MXU_DTYPE)
    return [stat, pltpu.VMEM((v_dim + ONES_ROWS, t), F32),
            scores, scores, stat, stat, probs, probs, stat, stat]


class _FlashStream:
    def __init__(self, refs):
        self.m, self.acc = refs[:2]
        self.s, self.s_max, self.p, self.alpha = refs[2:4], refs[4:6], refs[6:8], refs[8:10]

    def result(self):
        v_dim = self.acc.shape[0] - ONES_ROWS
        return self.acc[:v_dim, :] / self.acc[v_dim:v_dim + 1, :]


def _col_max(x):
    slab = x[:SLAB_ROWS]
    for r in range(SLAB_ROWS, x.shape[0], SLAB_ROWS):
        slab = jnp.maximum(slab, x[r:r + SLAB_ROWS])
    return jnp.max(slab, axis=0, keepdims=True)


def _causal_flash(i, streams, scores_of, values_of):
    for st in streams:
        st.m[...] = jnp.full_like(st.m, MASKED)
        st.acc[...] = jnp.zeros_like(st.acc)
        st.alpha[1][...] = jnp.ones_like(st.alpha[1])
        st.p[1][...] = jnp.zeros_like(st.p[1])

    def put_scores(st, slot, s):
        st.s[slot][...] = s
        st.s_max[slot][...] = _col_max(s)

    for idx, st in enumerate(streams):
        s = scores_of(idx, i)
        krow = lax.broadcasted_iota(jnp.int32, s.shape, 0)
        qcol = lax.broadcasted_iota(jnp.int32, s.shape, 1)
        put_scores(st, 0, jnp.where(krow <= qcol, s, MASKED))
    n_steps = i + 1
    last_past = jnp.maximum(i - 1, 0)

    def values_step(k, slot):
        tile = jnp.where(k <= 0, i, k - 1)
        for idx, st in enumerate(streams):
            vt = values_of(idx, tile)
            vt = jnp.concatenate([vt, jnp.ones((ONES_ROWS, vt.shape[1]), vt.dtype)], axis=0)
            st.acc[...] = st.alpha[slot][...] * st.acc[...] + jnp.dot(
                vt, st.p[slot][...], preferred_element_type=F32)

    def step(k, cur):
        nxt = 1 - cur
        ahead = jnp.minimum(k, last_past)
        tile = jnp.where(k <= 1, i, k - 2)
        for g0 in range(0, len(streams), FLASH_GROUP):
            group = list(enumerate(streams))[g0:g0 + FLASH_GROUP]
            for idx, st in group:
                m_prev = st.m[...]
                m_new = jnp.maximum(m_prev, st.s_max[cur][...])
                for r in range(0, st.s[cur].shape[0], SLAB_ROWS):
                    p = jnp.exp2(st.s[cur][r:r + SLAB_ROWS, :] - m_new)
                    st.p[cur][r:r + SLAB_ROWS, :] = p.astype(st.p[cur].dtype)
                st.m[...] = m_new
                st.alpha[cur][...] = jnp.exp2(m_prev - m_new)
            for idx, st in group:
                put_scores(st, nxt, scores_of(idx, ahead))
            for idx, st in group:
                vt = values_of(idx, tile)
                vt = jnp.concatenate([vt, jnp.ones((ONES_ROWS, vt.shape[1]), vt.dtype)], axis=0)
                st.acc[...] = st.alpha[nxt][...] * st.acc[...] + jnp.dot(
                    vt, st.p[nxt][...], preferred_element_type=F32)

    def unrolled_body(kk, carry):
        for u in range(FLASH_UNROLL):
            step(FLASH_UNROLL * kk + u, u % 2)
        return carry

    lax.fori_loop(0, n_steps // FLASH_UNROLL, unrolled_body, 0)
    odd = n_steps % 2 == 1

    @pl.when(odd)
    def _():
        step(n_steps - 1, 0)
        values_step(n_steps - 1, 0)

    @pl.when(jnp.logical_not(odd))
    def _():
        values_step(n_steps - 1, 1)


MOBA_HEADS_PER_STEP = 4


def _moba_kernel(q_ref, k_ref, vt_ref, o_ref, kmean_ref, qaug_ref, *state, n_blocks):
    i = pl.program_id(2)
    tq = q_ref.shape[0]
    tk = tq
    hd = A_HEAD_DIM
    heads = [slice(e * hd, (e + 1) * hd) for e in range(MOBA_HEADS_PER_STEP)]

    @pl.when(i == 0)
    def _():
        kmean_ref[...] = jnp.zeros_like(kmean_ref)

        def mean_body(n, carry):
            kb = k_ref[pl.ds(pl.multiple_of(n * MOBA_BLOCK, MOBA_BLOCK), MOBA_BLOCK), :]
            mean = jnp.sum(kb.astype(F32), axis=0, keepdims=True) * (1.0 / MOBA_BLOCK)
            for e, cols in enumerate(heads):
                kmean_ref[e, pl.ds(n, 1), :] = mean[:, cols]
            return carry

        lax.fori_loop(0, n_blocks, mean_body, 0)

    blk = lax.broadcasted_iota(jnp.int32, (LANES, tq), 0)
    qpos = lax.broadcasted_iota(jnp.int32, (LANES, tq), 1)
    own = jnp.right_shift(i * tq + qpos, MOBA_BLOCK_LOG2)
    past = blk < own
    blkf = blk.astype(F32)
    for e, cols in enumerate(heads):
        q_t = q_ref[:, cols].astype(F32).T.astype(MXU_DTYPE)
        km = kmean_ref[e]
        k_hi = km.astype(MXU_DTYPE)
        r1 = km - k_hi.astype(F32)
        k_mid = r1.astype(MXU_DTYPE)
        k_lo = (r1 - k_mid.astype(F32)).astype(MXU_DTYPE)
        gate = (jnp.dot(k_lo, q_t, preferred_element_type=F32)
                + jnp.dot(k_mid, q_t, preferred_element_type=F32)
                + jnp.dot(k_hi, q_t, preferred_element_type=F32))
        g = jnp.where(past, gate, -jnp.inf)
        chosen = jnp.zeros(gate.shape, jnp.bool_)
        for _ in range(min(MOBA_TOPK, n_blocks)):
            best = jnp.max(g, axis=0, keepdims=True)
            first_best = jnp.min(jnp.where(g == best, blkf, float(LANES)), axis=0, keepdims=True)
            pick = blkf == first_best
            chosen = jnp.logical_or(chosen, pick)
            g = jnp.where(pick, -jnp.inf, g)
        visible = jnp.logical_or(jnp.logical_and(chosen, past), blk == own)
        qaug_ref[e, :hd, :] = q_t
        qaug_ref[e, hd:, :] = jnp.where(visible, 0.0, MASKED).astype(qaug_ref.dtype)

    def scores_of(e, j):
        kt = k_ref[pl.ds(pl.multiple_of(j * tk, tk), tk), heads[e]]
        lane = lax.broadcasted_iota(jnp.int32, (1, LANES), 1)
        blocks_per_tile = tk // MOBA_BLOCK
        onehot = jnp.concatenate(
            [jnp.broadcast_to(jnp.where(lane == j * blocks_per_tile + b, 1.0, 0.0),
                              (MOBA_BLOCK, LANES)) for b in range(blocks_per_tile)], axis=0)
        k_aug = jnp.concatenate([kt, onehot.astype(kt.dtype)], axis=1)
        return jnp.dot(k_aug, qaug_ref[e], preferred_element_type=F32)

    streams = [_FlashStream(state[e * N_FLASH_REFS:(e + 1) * N_FLASH_REFS])
               for e in range(MOBA_HEADS_PER_STEP)]
    _causal_flash(i, streams, scores_of, lambda e, j: vt_ref[j, heads[e], :])
    for e, st in enumerate(streams):
        o_ref[:, heads[e]] = st.result().T.astype(o_ref.dtype)


def _moba_attention(qk, vt, n_heads, q_col, k_col, v_row):
    bsz, seq, _ = qk.shape
    t = min(ATTN_TILE, seq)
    n_tiles = seq // t
    n_blocks = seq // MOBA_BLOCK
    per = MOBA_HEADS_PER_STEP
    assert seq % t == 0 and t % MOBA_BLOCK == 0 and n_blocks <= LANES
    assert n_heads % per == 0 and q_col % per == 0 and k_col % per == 0 and v_row % per == 0
    hd = A_HEAD_DIM
    scratch = [pltpu.VMEM((per, LANES, hd), F32), pltpu.VMEM((per, 2 * hd, t), MXU_DTYPE)]
    for _ in range(per):
        scratch += _flash_scratch(t, hd)
    return pl.pallas_call(
        functools.partial(_moba_kernel, n_blocks=n_blocks),
        grid=(bsz, n_heads // per, n_tiles),
        in_specs=[pl.BlockSpec((None, t, per * hd), lambda b, h, i: (b, i, q_col // per + h)),
                  pl.BlockSpec((None, seq, per * hd), lambda b, h, i: (b, 0, k_col // per + h),
                               pipeline_mode=pl.Buffered(1)),
                  pl.BlockSpec((None, n_tiles, per * hd, t),
                               lambda b, h, i: (b, 0, v_row // per + h, 0),
                               pipeline_mode=pl.Buffered(1))],
        out_specs=pl.BlockSpec((None, t, per * hd), lambda b, h, i: (b, i, h)),
        out_shape=jax.ShapeDtypeStruct((bsz, seq, n_heads * hd), MXU_DTYPE),
        scratch_shapes=scratch,
        compiler_params=_params(("parallel", "parallel", "arbitrary"), 56),
    )(qk, qk, vt)


DIFF_HEADS_PER_STEP = 2


def _diff_kernel(q_ref, k_ref, vt_ref, lam_ref, w_ref, o_ref, *state, lambda_init):
    i = pl.program_id(2)
    tk = q_ref.shape[0]
    hd = B_HEAD_DIM
    n_streams = 2 * DIFF_HEADS_PER_STEP
    streams = [_FlashStream(state[s * N_FLASH_REFS:(s + 1) * N_FLASH_REFS])
               for s in range(n_streams)]

    def scores_of(s, j):
        cols = slice(s * hd, (s + 1) * hd)
        rows = pl.ds(pl.multiple_of(j * tk, tk), tk)
        return lax.dot_general(k_ref[rows, cols], q_ref[:, cols], NT_DIMS,
                               preferred_element_type=F32)

    def values_of(s, j):
        head = s // 2
        return vt_ref[j, head * 2 * hd:(head + 1) * 2 * hd, :]

    _causal_flash(i, streams, scores_of, values_of)

    lv = lam_ref[...]
    lam = (jnp.exp(jnp.sum(lv[0:1] * lv[1:2], axis=1, keepdims=True))
           - jnp.exp(jnp.sum(lv[2:3] * lv[3:4], axis=1, keepdims=True)) + lambda_init)
    for head in range(DIFF_HEADS_PER_STEP):
        y = streams[2 * head].result() - lam * streams[2 * head + 1].result()
        ms = jnp.mean(y * y, axis=0, keepdims=True)
        y = (y * lax.rsqrt(ms + DIFF_SUBLN_EPS)) * w_ref[...]
        o_ref[:, head * 2 * hd:(head + 1) * 2 * hd] = (
            (y * (1.0 - lambda_init)).T.astype(o_ref.dtype))


def _diff_attention(qk, vt, lam_vecs, subln_w, lambda_init, n_heads, q_col, k_col, v_row):
    bsz, seq, _ = qk.shape
    t = min(ATTN_TILE, seq)
    n_tiles = seq // t
    hd = B_HEAD_DIM
    per = DIFF_HEADS_PER_STEP
    assert n_heads % per == 0 and q_col % (2 * per) == 0 and k_col % (2 * per) == 0
    assert v_row % per == 0
    width = 2 * per * hd
    scratch = []
    for _ in range(2 * per):
        scratch += _flash_scratch(t, 2 * hd)
    return pl.pallas_call(
        functools.partial(_diff_kernel, lambda_init=lambda_init),
        grid=(bsz, n_heads // per, n_tiles),
        in_specs=[pl.BlockSpec((None, t, width), lambda b, h, i: (b, i, q_col // (2 * per) + h)),
                  pl.BlockSpec((None, seq, width), lambda b, h, i: (b, 0, k_col // (2 * per) + h),
                               pipeline_mode=pl.Buffered(1)),
                  pl.BlockSpec((None, n_tiles, width, t),
                               lambda b, h, i: (b, 0, v_row // per + h, 0),
                               pipeline_mode=pl.Buffered(1)),
                  pl.BlockSpec((4, hd), lambda b, h, i: (0, 0)),
                  pl.BlockSpec((2 * hd, 1), lambda b, h, i: (0, 0))],
        out_specs=pl.BlockSpec((None, t, width), lambda b, h, i: (b, i, h)),
        out_shape=jax.ShapeDtypeStruct((bsz, seq, n_heads * 2 * hd), MXU_DTYPE),
        scratch_shapes=scratch,
        compiler_params=_params(("parallel", "parallel", "arbitrary"), 58),
    )(qk, qk, vt, lam_vecs.astype(F32), subln_w.astype(F32).reshape(2 * hd, 1))


def _swa_kernel(q_ref, kc_ref, kp_ref, vtc_ref, vtp_ref, sink_ref, o_ref):
    i = pl.program_id(1)
    w = C_WINDOW
    hd = C_HEAD_DIM
    gw = C_GROUP * hd
    lane = lax.broadcasted_iota(jnp.int32, (w, LANES), 1)
    lo = jnp.where(lane < hd, 1.0, 0.0).astype(q_ref.dtype)
    hi = jnp.where(lane >= hd, 1.0, 0.0).astype(q_ref.dtype)
    krow = lax.broadcasted_iota(jnp.int32, (2 * w, C_GROUP * w), 0)
    qcol = lax.broadcasted_iota(jnp.int32, (2 * w, C_GROUP * w), 1) & (w - 1)
    rel = w + qcol - krow
    valid = (rel >= 0) & (rel < w) & ((i - 1) * w + krow >= 0)
    ones = jnp.ones((ONES_ROWS, 2 * w), vtc_ref.dtype)
    for g in range(q_ref.shape[1] // gw):
        k_cols = slice(g * LANES, (g + 1) * LANES)
        v_rows = slice(g * hd, (g + 1) * hd)
        kk = jnp.concatenate([kp_ref[:, k_cols], kc_ref[:, k_cols]], axis=0)
        vt = jnp.concatenate([vtp_ref[v_rows, :], vtc_ref[v_rows, :]], axis=1)
        q_heads = []
        for p in range(C_GROUP // 2):
            q2 = q_ref[:, g * gw + p * LANES:g * gw + (p + 1) * LANES]
            q_heads += [q2 * lo, q2 * hi]
        q_all = jnp.concatenate(q_heads, axis=0)
        s = lax.dot_general(kk, q_all, NT_DIMS, preferred_element_type=F32)
        s = jnp.where(valid, s, MASKED)
        sink = sink_ref[g] * LOG2E
        m = jnp.maximum(_col_max(s), sink)
        e = jnp.concatenate([jnp.exp2(s[r:r + SLAB_ROWS] - m).astype(vt.dtype)
                             for r in range(0, 2 * w, SLAB_ROWS)], axis=0)
        o = jnp.dot(jnp.concatenate([vt, ones], axis=0), e, preferred_element_type=F32)
        o = o[:hd] / (o[hd:hd + 1] + jnp.exp2(sink - m))
        o = jnp.concatenate([o[:, h * w:(h + 1) * w] for h in range(C_GROUP)], axis=0)
        o_ref[:, g * gw:(g + 1) * gw] = o.T.astype(o_ref.dtype)


def _swa_attention(qk, vt, sinks, n_kv):
    bsz, seq, _ = qk.shape
    w = C_WINDOW
    q_width = n_kv * C_GROUP * C_HEAD_DIM
    k_width = n_kv * LANES
    assert q_width % k_width == 0
    per_tile = vt.shape[-1] // w
    sink_cols = jnp.repeat(sinks.astype(F32).reshape(n_kv, 1, C_GROUP), w, axis=-1)
    prev = lambda i: jnp.maximum(i - 1, 0)
    vt_spec = lambda at: pl.BlockSpec(
        (None, None, n_kv * C_HEAD_DIM, w),
        lambda b, i: (b, at(i) // per_tile, 0, at(i) % per_tile))
    return pl.pallas_call(
        _swa_kernel,
        grid=(bsz, seq // w),
        in_specs=[pl.BlockSpec((None, w, q_width), lambda b, i: (b, i, 0)),
                  pl.BlockSpec((None, w, k_width), lambda b, i: (b, i, q_width // k_width)),
                  pl.BlockSpec((None, w, k_width), lambda b, i: (b, prev(i), q_width // k_width)),
                  vt_spec(lambda i: i),
                  vt_spec(prev),
                  pl.BlockSpec((n_kv, 1, C_GROUP * w), lambda b, i: (0, 0, 0))],
        out_specs=pl.BlockSpec((None, w, q_width), lambda b, i: (b, i, 0)),
        out_shape=jax.ShapeDtypeStruct((bsz, seq, q_width), MXU_DTYPE),
        compiler_params=_params(("parallel", "parallel"), 32),
    )(qk, qk, qk, vt, vt, sink_cols)


def _outproj_kernel(*refs, n_parts, has_bias):
    y_refs, w_refs = refs[:n_parts], refs[n_parts:2 * n_parts]
    rest = list(refs[2 * n_parts:])
    b_ref = rest.pop(0) if has_bias else None
    g_ref, x_ref, o_ref = rest
    acc = jnp.dot(y_refs[0][...], w_refs[0][...], preferred_element_type=F32)
    for y_ref, w_ref in zip(y_refs[1:], w_refs[1:]):
        acc = acc + jnp.dot(y_ref[...], w_ref[...], preferred_element_type=F32)
    if has_bias:
        acc = acc + b_ref[...]
    o_ref[...] = x_ref[...] + g_ref[...] * acc


def _out_project(x, gate, ys, ws, bias=None, in_place=True):
    bsz, seq, d = x.shape
    tm = min(ROW_TILE, seq)
    in_specs = [pl.BlockSpec((None, tm, y.shape[-1]), lambda b, i: (b, i, 0)) for y in ys]
    in_specs += [pl.BlockSpec(w.shape, lambda b, i: (0, 0)) for w in ws]
    args = list(ys) + list(ws)
    if bias is not None:
        in_specs.append(pl.BlockSpec((1, d), lambda b, i: (0, 0)))
        args.append(bias.reshape(1, d).astype(F32))
    in_specs += [pl.BlockSpec((None, 1, d), lambda b, i: (b, 0, 0)),
                 pl.BlockSpec((None, tm, d), lambda b, i: (b, i, 0))]
    args += [gate, x]
    return pl.pallas_call(
        functools.partial(_outproj_kernel, n_parts=len(ys), has_bias=bias is not None),
        grid=(bsz, seq // tm),
        in_specs=in_specs,
        out_specs=pl.BlockSpec((None, tm, d), lambda b, i: (b, i, 0)),
        out_shape=jax.ShapeDtypeStruct(x.shape, F32),
        input_output_aliases={len(args) - 1: 0} if in_place else {},
        compiler_params=_params(("parallel", "parallel"), 48),
    )(*args)


def _ffn_kernel(x_ref, sc_ref, sh_ref, g_ref, wg_ref, wu_ref, wo_ref, *rest, final):
    rest = list(rest)
    fw_ref = rest.pop(0) if final else None
    o_ref, h_ref = rest
    k = pl.program_id(2)
    tm = x_ref.shape[0]
    halves = [slice(r, r + min(ROW_TILE, tm)) for r in range(0, tm, ROW_TILE)]

    @pl.when(k == 0)
    def _():
        for rows in halves:
            h_ref[rows, :] = _norm_mod(x_ref[rows, :], sc_ref[...], sh_ref[...]).astype(h_ref.dtype)
        o_ref[...] = jnp.zeros_like(o_ref)

    for rows in halves:
        h = h_ref[rows, :]
        gt = jnp.dot(h, wg_ref[...], preferred_element_type=F32)
        up = jnp.dot(h, wu_ref[...], preferred_element_type=F32)
        act = ((gt * jax.nn.sigmoid(gt)) * up).astype(h.dtype)
        o_ref[rows, :] += jnp.dot(act, wo_ref[...], preferred_element_type=F32)

    @pl.when(k == pl.num_programs(2) - 1)
    def _():
        for rows in halves:
            xn = x_ref[rows, :] + g_ref[...] * o_ref[rows, :]
            if final:
                ms = jnp.mean(xn * xn, axis=-1, keepdims=True)
                xn = (xn * lax.rsqrt(ms + RMS_EPS)) * fw_ref[...]
            o_ref[rows, :] = xn


def _ffn(x, sc, sh, gate, w_in, w_out, final_w=None):
    bsz, seq, d = x.shape
    hidden = w_out.shape[0]
    tm = min(FFN_ROW_TILE, seq)
    th = _col_tile(hidden)
    n_h = hidden // th
    mod_spec = pl.BlockSpec((None, 1, d), lambda b, i, k: (b, 0, 0))
    in_specs = [pl.BlockSpec((None, tm, d), lambda b, i, k: (b, i, 0)),
                mod_spec, mod_spec, mod_spec,
                pl.BlockSpec((d, th), lambda b, i, k: (0, k)),
                pl.BlockSpec((d, th), lambda b, i, k: (0, n_h + k)),
                pl.BlockSpec((th, d), lambda b, i, k: (k, 0))]
    args = [x, sc, sh, gate, w_in, w_in, w_out]
    if final_w is not None:
        in_specs.append(pl.BlockSpec((1, d), lambda b, i, k: (0, 0)))
        args.append(final_w.reshape(1, d).astype(F32))
    return pl.pallas_call(
        functools.partial(_ffn_kernel, final=final_w is not None),
        grid=(bsz, seq // tm, n_h),
        in_specs=in_specs,
        out_specs=pl.BlockSpec((None, tm, d), lambda b, i, k: (b, i, 0)),
        out_shape=jax.ShapeDtypeStruct(x.shape, F32),
        scratch_shapes=[pltpu.VMEM((tm, d), MXU_DTYPE)],
        input_output_aliases={0: 0},
        compiler_params=_params(("parallel", "parallel", "arbitrary"), 60),
    )(*args)


def _rope_tables(seq, dim, query_scale):
    pos = jnp.arange(seq, dtype=F32)
    inv_freq = ROPE_THETA ** (-jnp.arange(0, dim, 2, dtype=F32) / dim)
    ang = pos[:, None] * inv_freq[None, :]
    cos, sin = jnp.cos(ang), jnp.sin(ang)
    reps = LANES // dim
    cos = jnp.tile(jnp.concatenate([cos, cos], axis=1), (1, reps))
    sin = jnp.tile(jnp.concatenate([-sin, sin], axis=1), (1, reps))
    return jnp.stack([cos * query_scale, cos]), jnp.stack([sin * query_scale, sin])


def _lambda_init(layer):
    return 0.8 - 0.6 * math.exp(-0.3 * layer)


def _dup_heads(w, n_heads, head_dim):
    lead = w.shape[:-1]
    w = w.reshape(lead + (n_heads, 1, head_dim))
    return jnp.broadcast_to(w, lead + (n_heads, 2, head_dim)).reshape(lead + (2 * n_heads * head_dim,))


def kernel(x, c, ada_w, ada_b, ab_w_in, ab_w_out, diff_lambda, diff_subln, swa_w_in, swa_b_in,
           swa_w_out, swa_b_out, swa_sinks, ffn_w_in, ffn_w_out, final_norm):
    bsz, seq, d = x.shape
    depth = ada_w.shape[0]
    a_heads, b_heads = d // 256, d // 512
    a_width, b_width = a_heads * A_HEAD_DIM, 2 * b_heads * B_HEAD_DIM
    assert a_width == b_width
    c_q_heads = d // C_HEAD_DIM
    c_kv_heads = c_q_heads // C_GROUP
    c_q_width, c_kv_width = c_q_heads * C_HEAD_DIM, c_kv_heads * C_HEAD_DIM

    rope_ab = _rope_tables(seq, A_HEAD_DIM, A_HEAD_DIM ** -0.5 * LOG2E)
    rope_c = _rope_tables(seq, C_HEAD_DIM, C_HEAD_DIM ** -0.5 * LOG2E)
    mod = _modulation(c, ada_w, ada_b)

    for layer in range(depth):
        sh1, sc1, g1, sh2, sc2, g2 = [
            mod[layer, :, None, m * d:(m + 1) * d] for m in range(N_MOD)]
        li = layer // 2
        if layer % 2 == 0:
            w_in = ab_w_in[li].astype(MXU_DTYPE)
            aq, ak, av, bq, bk, bv = jnp.split(
                w_in, [a_width, 2 * a_width, 3 * a_width,
                       3 * a_width + b_width, 3 * a_width + 2 * b_width], axis=1)
            qk = _project(x, sc1, sh1, jnp.concatenate([aq, ak, bq, bk], axis=1),
                          rope=rope_ab, rope_dim=A_HEAD_DIM,
                          is_query_col=lambda col: (col // a_width) % 2 == 0)
            vt = _project(x, sc1, sh1, jnp.concatenate([av, bv], axis=1), transpose_out=True)
            ya = _moba_attention(qk, vt, a_heads, q_col=0, k_col=a_heads, v_row=0)
            yb = _diff_attention(qk, vt, diff_lambda[li], diff_subln[li], _lambda_init(layer),
                                 b_heads, q_col=2 * a_heads, k_col=2 * a_heads + 2 * b_heads,
                                 v_row=a_width // (2 * B_HEAD_DIM))
            w_out = ab_w_out[li].astype(MXU_DTYPE)
            x = _out_project(x, g1, [ya, yb], [w_out[:a_width], w_out[a_width:]],
                             in_place=layer > 0)
        else:
            w_in, b_in = swa_w_in[li], swa_b_in[li]
            wq, wk, wv = jnp.split(w_in, [c_q_width, c_q_width + c_kv_width], axis=1)
            bq_, bk_, bv_ = jnp.split(b_in, [c_q_width, c_q_width + c_kv_width])
            w_qk = jnp.concatenate([wq, _dup_heads(wk, c_kv_heads, C_HEAD_DIM)], axis=1)
            b_qk = jnp.concatenate([bq_, _dup_heads(bk_, c_kv_heads, C_HEAD_DIM)])
            qk = _project(x, sc1, sh1, w_qk.astype(MXU_DTYPE), bias=b_qk,
                          rope=rope_c, rope_dim=C_HEAD_DIM,
                          is_query_col=lambda col: col < c_q_width)
            vt = _project(x, sc1, sh1, wv.astype(MXU_DTYPE), bias=bv_, transpose_out=True)
            y = _swa_attention(qk, vt, swa_sinks[li], c_kv_heads)
            x = _out_project(x, g1, [y], [swa_w_out[li].astype(MXU_DTYPE)], bias=swa_b_out[li])
        x = _ffn(x, sc2, sh2, g2, ffn_w_in[layer].astype(MXU_DTYPE),
                 ffn_w_out[layer].astype(MXU_DTYPE),
                 final_w=final_norm if layer == depth - 1 else None)
    return x
```

```python
import functools
import math

import jax
import jax.numpy as jnp
from jax import lax
from jax.experimental import pallas as pl
from jax.experimental.pallas import tpu as pltpu

F32 = jnp.float32
MXU_DTYPE = jnp.bfloat16

ROPE_THETA = 10000.0
RMS_EPS = 1e-6
DIFF_SUBLN_EPS = 1e-5
A_HEAD_DIM = 128
MOBA_BLOCK = 256
MOBA_BLOCK_LOG2 = 8
MOBA_TOPK = 3
B_HEAD_DIM = 128
C_HEAD_DIM = 64
C_GROUP = 8
C_WINDOW = 128
N_MOD = 6

LANES = 128
MASKED = -1e30
LOG2E = 1.4426950408889634

ATTN_TILE = 512
ROW_TILE = 512
FFN_ROW_TILE = 1024
MOD_COL_TILE = 1024
MIB = 2 ** 20

NT_DIMS = (((1,), (1,)), ((), ()))


def _params(semantics, vmem_mib):
    return pltpu.CompilerParams(dimension_semantics=semantics,
                                vmem_limit_bytes=vmem_mib * MIB)


def _col_tile(n):
    for t in (512, 256, 128):
        if n % t == 0:
            return t
    raise ValueError(f"column count {n} is not a multiple of {LANES}")


def _column_tiles(w, tn):
    k, n = w.shape
    return w.reshape(k, n // tn, tn).transpose(1, 0, 2)


def _norm_mod(x, sc, sh):
    ms = jnp.mean(x * x, axis=-1, keepdims=True)
    return (x * lax.rsqrt(ms + RMS_EPS)) * (1.0 + sc) + sh


def _mod_kernel(c_ref, w_ref, b_ref, o_ref):
    c = c_ref[...]
    cond = (c * jax.nn.sigmoid(c)).astype(MXU_DTYPE)
    o_ref[...] = jnp.dot(cond, w_ref[...].astype(MXU_DTYPE),
                         preferred_element_type=F32) + b_ref[...]


def _modulation(c, ada_w, ada_b):
    depth, d, n = ada_w.shape
    bsz = c.shape[0]
    rows = -(-bsz // 8) * 8
    c_pad = jnp.pad(c, ((0, rows - bsz), (0, 0)))
    tn = MOD_COL_TILE if n % MOD_COL_TILE == 0 else _col_tile(n)
    out = pl.pallas_call(
        _mod_kernel,
        grid=(depth, n // tn),
        in_specs=[pl.BlockSpec((rows, d), lambda l, j: (0, 0)),
                  pl.BlockSpec((None, d, tn), lambda l, j: (l, 0, j)),
                  pl.BlockSpec((None, 1, tn), lambda l, j: (l, 0, j))],
        out_specs=pl.BlockSpec((None, rows, tn), lambda l, j: (l, 0, j)),
        out_shape=jax.ShapeDtypeStruct((depth, rows, n), F32),
        compiler_params=_params(("parallel", "parallel"), 40),
    )(c_pad, ada_w, ada_b.reshape(depth, 1, n))
    return out[:, :bsz]


def _rope_partner(a, rope_dim):
    if rope_dim == LANES:
        return pltpu.roll(a, LANES // 2, 1)
    lane = lax.broadcasted_iota(jnp.int32, a.shape, 1)
    half = rope_dim // 2
    first_half = (lane & (rope_dim - 1)) < half
    return jnp.where(first_half, pltpu.roll(a, LANES - half, 1), pltpu.roll(a, half, 1))


def _proj_kernel(x_ref, sc_ref, sh_ref, w_ref, *rest, rope_dim, has_bias, transpose_out,
                 query_tiles):
    rest = list(rest)
    b_ref = rest.pop(0) if has_bias else None
    cos_ref, sin_ref = (rest.pop(0), rest.pop(0)) if rope_dim else (None, None)
    (o_ref,) = rest
    n_t, _, tn = w_ref.shape
    h = _norm_mod(x_ref[...], sc_ref[...], sh_ref[...]).astype(w_ref.dtype)
    for j in range(n_t):
        cols = slice(j * tn, (j + 1) * tn)
        acc = jnp.dot(h, w_ref[j], preferred_element_type=F32)
        if has_bias:
            acc = acc + b_ref[:, cols]
        if rope_dim:
            table = 0 if query_tiles[j] else 1
            cos, sin = cos_ref[table], sin_ref[table]
            for t in range(tn // LANES):
                a = acc[:, t * LANES:(t + 1) * LANES]
                o_ref[:, j * tn + t * LANES:j * tn + (t + 1) * LANES] = (
                    a * cos + _rope_partner(a, rope_dim) * sin).astype(o_ref.dtype)
        elif transpose_out:
            o_ref[cols, :] = acc.T.astype(o_ref.dtype)
        else:
            o_ref[:, cols] = acc.astype(o_ref.dtype)


def _project(x, sc, sh, w, bias=None, rope=None, rope_dim=0, is_query_col=None,
             transpose_out=False):
    bsz, seq, d = x.shape
    n = w.shape[1]
    tm = min(ROW_TILE, seq)
    tn = _col_tile(n)
    n_t = n // tn
    in_specs = [pl.BlockSpec((None, tm, d), lambda b, i: (b, i, 0)),
                pl.BlockSpec((None, 1, d), lambda b, i: (b, 0, 0)),
                pl.BlockSpec((None, 1, d), lambda b, i: (b, 0, 0)),
                pl.BlockSpec((n_t, d, tn), lambda b, i: (0, 0, 0), pipeline_mode=pl.Buffered(1))]
    args = [x, sc, sh, _column_tiles(w, tn)]
    if bias is not None:
        in_specs.append(pl.BlockSpec((1, n), lambda b, i: (0, 0)))
        args.append(bias.reshape(1, n).astype(F32))
    query_tiles = None
    if rope_dim:
        query_tiles = tuple(bool(is_query_col(j * tn)) for j in range(n_t))
        in_specs += [pl.BlockSpec((2, tm, LANES), lambda b, i: (0, i, 0))] * 2
        args += list(rope)
    if transpose_out:
        out_shape = jax.ShapeDtypeStruct((bsz, seq // tm, n, tm), MXU_DTYPE)
        out_spec = pl.BlockSpec((None, None, n, tm), lambda b, i: (b, i, 0, 0))
    else:
        out_shape = jax.ShapeDtypeStruct((bsz, seq, n), MXU_DTYPE)
        out_spec = pl.BlockSpec((None, tm, n), lambda b, i: (b, i, 0))
    return pl.pallas_call(
        functools.partial(_proj_kernel, rope_dim=rope_dim, has_bias=bias is not None,
                          transpose_out=transpose_out, query_tiles=query_tiles),
        grid=(bsz, seq // tm),
        in_specs=in_specs,
        out_specs=out_spec,
        out_shape=out_shape,
        compiler_params=_params(("parallel", "parallel"), 48),
    )(*args)


N_FLASH_REFS = 10
SLAB_ROWS = 64
ONES_ROWS = 16
FLASH_UNROLL = 2
FLASH_GROUP = 2


def _flash_scratch(t, v_dim):
    stat = pltpu.VMEM((1, t), F32)
    scores = pltpu.VMEM((t, t), F32)
    probs = pltpu.VMEM((t, t), MXU_DTYPE)
    return [stat, pltpu.VMEM((v_dim + ONES_ROWS, t), F32),
            scores, scores, stat, stat, probs, probs, stat, stat]


class _FlashStream:
    def __init__(self, refs):
        self.m, self.acc = refs[:2]
        self.s, self.s_max, self.p, self.alpha = refs[2:4], refs[4:6], refs[6:8], refs[8:10]

    def result(self):
        v_dim = self.acc.shape[0] - ONES_ROWS
        return self.acc[:v_dim, :] / self.acc[v_dim:v_dim + 1, :]


def _col_max(x):
    slab = x[:SLAB_ROWS]
    for r in range(SLAB_ROWS, x.shape[0], SLAB_ROWS):
        slab = jnp.maximum(slab, x[r:r + SLAB_ROWS])
    return jnp.max(slab, axis=0, keepdims=True)


def _causal_flash(i, streams, scores_of, values_of):
    for st in streams:
        st.m[...] = jnp.full_like(st.m, MASKED)
        st.acc[...] = jnp.zeros_like(st.acc)
        st.alpha[1][...] = jnp.ones_like(st.alpha[1])
        st.p[1][...] = jnp.zeros_like(st.p[1])

    def put_scores(st, slot, s):
        st.s[slot][...] = s
        st.s_max[slot][...] = _col_max(s)

    for idx, st in enumerate(streams):
        s = scores_of(idx, i)
        krow = lax.broadcasted_iota(jnp.int32, s.shape, 0)
        qcol = lax.broadcasted_iota(jnp.int32, s.shape, 1)
        put_scores(st, 0, jnp.where(krow <= qcol, s, MASKED))
    n_steps = i + 1
    last_past = jnp.maximum(i - 1, 0)

    def values_step(k, slot):
        tile = jnp.where(k <= 0, i, k - 1)
        for idx, st in enumerate(streams):
            vt = values_of(idx, tile)
            vt = jnp.concatenate([vt, jnp.ones((ONES_ROWS, vt.shape[1]), vt.dtype)], axis=0)
            st.acc[...] = st.alpha[slot][...] * st.acc[...] + jnp.dot(
                vt, st.p[slot][...], preferred_element_type=F32)

    def step(k, cur):
        nxt = 1 - cur
        ahead = jnp.minimum(k, last_past)
        tile = jnp.where(k <= 1, i, k - 2)
        for g0 in range(0, len(streams), FLASH_GROUP):
            group = list(enumerate(streams))[g0:g0 + FLASH_GROUP]
            for idx, st in group:
                m_prev = st.m[...]
                m_new = jnp.maximum(m_prev, st.s_max[cur][...])
                for r in range(0, st.s[cur].shape[0], SLAB_ROWS):
                    p = jnp.exp2(st.s[cur][r:r + SLAB_ROWS, :] - m_new)
                    st.p[cur][r:r + SLAB_ROWS, :] = p.astype(st.p[cur].dtype)
                st.m[...] = m_new
                st.alpha[cur][...] = jnp.exp2(m_prev - m_new)
            for idx, st in group:
                put_scores(st, nxt, scores_of(idx, ahead))
            for idx, st in group:
                vt = values_of(idx, tile)
                vt = jnp.concatenate([vt, jnp.ones((ONES_ROWS, vt.shape[1]), vt.dtype)], axis=0)
                st.acc[...] = st.alpha[nxt][...] * st.acc[...] + jnp.dot(
                    vt, st.p[nxt][...], preferred_element_type=F32)

    def unrolled_body(kk, carry):
        for u in range(FLASH_UNROLL):
            step(FLASH_UNROLL * kk + u, u % 2)
        return carry

    lax.fori_loop(0, n_steps // FLASH_UNROLL, unrolled_body, 0)
    odd = n_steps % 2 == 1

    @pl.when(odd)
    def _():
        step(n_steps - 1, 0)
        values_step(n_steps - 1, 0)

    @pl.when(jnp.logical_not(odd))
    def _():
        values_step(n_steps - 1, 1)


MOBA_HEADS_PER_STEP = 4


def _moba_kernel(q_ref, k_ref, vt_ref, o_ref, kmean_ref, qaug_ref, *state, n_blocks):
    i = pl.program_id(2)
    tq = q_ref.shape[0]
    tk = tq
    hd = A_HEAD_DIM
    heads = [slice(e * hd, (e + 1) * hd) for e in range(MOBA_HEADS_PER_STEP)]

    @pl.when(i == 0)
    def _():
        kmean_ref[...] = jnp.zeros_like(kmean_ref)

        def mean_body(n, carry):
            kb = k_ref[pl.ds(pl.multiple_of(n * MOBA_BLOCK, MOBA_BLOCK), MOBA_BLOCK), :]
            mean = jnp.sum(kb.astype(F32), axis=0, keepdims=True) * (1.0 / MOBA_BLOCK)
            for e, cols in enumerate(heads):
                kmean_ref[e, pl.ds(n, 1), :] = mean[:, cols]
            return carry

        lax.fori_loop(0, n_blocks, mean_body, 0)

    blk = lax.broadcasted_iota(jnp.int32, (LANES, tq), 0)
    qpos = lax.broadcasted_iota(jnp.int32, (LANES, tq), 1)
    own = jnp.right_shift(i * tq + qpos, MOBA_BLOCK_LOG2)
    past = blk < own
    blkf = blk.astype(F32)
    for e, cols in enumerate(heads):
        q_t = q_ref[:, cols].astype(F32).T.astype(MXU_DTYPE)
        km = kmean_ref[e]
        k_hi = km.astype(MXU_DTYPE)
        r1 = km - k_hi.astype(F32)
        k_mid = r1.astype(MXU_DTYPE)
        k_lo = (r1 - k_mid.astype(F32)).astype(MXU_DTYPE)
        gate = (jnp.dot(k_lo, q_t, preferred_element_type=F32)
                + jnp.dot(k_mid, q_t, preferred_element_type=F32)
                + jnp.dot(k_hi, q_t, preferred_element_type=F32))
        g = jnp.where(past, gate, -jnp.inf)
        chosen = jnp.zeros(gate.shape, jnp.bool_)
        for _ in range(min(MOBA_TOPK, n_blocks)):
            best = jnp.max(g, axis=0, keepdims=True)
            first_best = jnp.min(jnp.where(g == best, blkf, float(LANES)), axis=0, keepdims=True)
            pick = blkf == first_best
            chosen = jnp.logical_or(chosen, pick)
            g = jnp.where(pick, -jnp.inf, g)
        visible = jnp.logical_or(jnp.logical_and(chosen, past), blk == own)
        qaug_ref[e, :hd, :] = q_t
        qaug_ref[e, hd:, :] = jnp.where(visible, 0.0, MASKED).astype(qaug_ref.dtype)

    def scores_of(e, j):
        kt = k_ref[pl.ds(pl.multiple_of(j * tk, tk), tk), heads[e]]
        lane = lax.broadcasted_iota(jnp.int32, (1, LANES), 1)
        blocks_per_tile = tk // MOBA_BLOCK
        onehot = jnp.concatenate(
            [jnp.broadcast_to(jnp.where(lane == j * blocks_per_tile + b, 1.0, 0.0),
                              (MOBA_BLOCK, LANES)) for b in range(blocks_per_tile)], axis=0)
        k_aug = jnp.concatenate([kt, onehot.astype(kt.dtype)], axis=1)
        return jnp.dot(k_aug, qaug_ref[e], preferred_element_type=F32)

    streams = [_FlashStream(state[e * N_FLASH_REFS:(e + 1) * N_FLASH_REFS])
               for e in range(MOBA_HEADS_PER_STEP)]
    _causal_flash(i, streams, scores_of, lambda e, j: vt_ref[j, heads[e], :])
    for e, st in enumerate(streams):
        o_ref[:, heads[e]] = st.result().T.astype(o_ref.dtype)


def _moba_attention(qk, vt, n_heads, q_col, k_col, v_row):
    bsz, seq, _ = qk.shape
    t = min(ATTN_TILE, seq)
    n_tiles = seq // t
    n_blocks = seq // MOBA_BLOCK
    per = MOBA_HEADS_PER_STEP
    assert seq % t == 0 and t % MOBA_BLOCK == 0 and n_blocks <= LANES
    assert n_heads % per == 0 and q_col % per == 0 and k_col % per == 0 and v_row % per == 0
    hd = A_HEAD_DIM
    scratch = [pltpu.VMEM((per, LANES, hd), F32), pltpu.VMEM((per, 2 * hd, t), MXU_DTYPE)]
    for _ in range(per):
        scratch += _flash_scratch(t, hd)
    return pl.pallas_call(
        functools.partial(_moba_kernel, n_blocks=n_blocks),
        grid=(bsz, n_heads // per, n_tiles),
        in_specs=[pl.BlockSpec((None, t, per * hd), lambda b, h, i: (b, i, q_col // per + h)),
                  pl.BlockSpec((None, seq, per * hd), lambda b, h, i: (b, 0, k_col // per + h),
                               pipeline_mode=pl.Buffered(1)),
                  pl.BlockSpec((None, n_tiles, per * hd, t),
                               lambda b, h, i: (b, 0, v_row // per + h, 0),
                               pipeline_mode=pl.Buffered(1))],
        out_specs=pl.BlockSpec((None, t, per * hd), lambda b, h, i: (b, i, h)),
        out_shape=jax.ShapeDtypeStruct((bsz, seq, n_heads * hd), MXU_DTYPE),
        scratch_shapes=scratch,
        compiler_params=_params(("parallel", "parallel", "arbitrary"), 56),
    )(qk, qk, vt)


DIFF_HEADS_PER_STEP = 2


def _diff_kernel(q_ref, k_ref, vt_ref, lam_ref, w_ref, o_ref, *state, lambda_init):
    i = pl.program_id(2)
    tk = q_ref.shape[0]
    hd = B_HEAD_DIM
    n_streams = 2 * DIFF_HEADS_PER_STEP
    streams = [_FlashStream(state[s * N_FLASH_REFS:(s + 1) * N_FLASH_REFS])
               for s in range(n_streams)]

    def scores_of(s, j):
        cols = slice(s * hd, (s + 1) * hd)
        rows = pl.ds(pl.multiple_of(j * tk, tk), tk)
        return lax.dot_general(k_ref[rows, cols], q_ref[:, cols], NT_DIMS,
                               preferred_element_type=F32)

    def values_of(s, j):
        head = s // 2
        return vt_ref[j, head * 2 * hd:(head + 1) * 2 * hd, :]

    _causal_flash(i, streams, scores_of, values_of)

    lv = lam_ref[...]
    lam = (jnp.exp(jnp.sum(lv[0:1] * lv[1:2], axis=1, keepdims=True))
           - jnp.exp(jnp.sum(lv[2:3] * lv[3:4], axis=1, keepdims=True)) + lambda_init)
    for head in range(DIFF_HEADS_PER_STEP):
        y = streams[2 * head].result() - lam * streams[2 * head + 1].result()
        ms = jnp.mean(y * y, axis=0, keepdims=True)
        y = (y * lax.rsqrt(ms + DIFF_SUBLN_EPS)) * w_ref[...]
        o_ref[:, head * 2 * hd:(head + 1) * 2 * hd] = (
            (y * (1.0 - lambda_init)).T.astype(o_ref.dtype))


def _diff_attention(qk, vt, lam_vecs, subln_w, lambda_init, n_heads, q_col, k_col, v_row):
    bsz, seq, _ = qk.shape
    t = min(ATTN_TILE, seq)
    n_tiles = seq // t
    hd = B_HEAD_DIM
    per = DIFF_HEADS_PER_STEP
    assert n_heads % per == 0 and q_col % (2 * per) == 0 and k_col % (2 * per) == 0
    assert v_row % per == 0
    width = 2 * per * hd
    scratch = []
    for _ in range(2 * per):
        scratch += _flash_scratch(t, 2 * hd)
    return pl.pallas_call(
        functools.partial(_diff_kernel, lambda_init=lambda_init),
        grid=(bsz, n_heads // per, n_tiles),
        in_specs=[pl.BlockSpec((None, t, width), lambda b, h, i: (b, i, q_col // (2 * per) + h)),
                  pl.BlockSpec((None, seq, width), lambda b, h, i: (b, 0, k_col // (2 * per) + h),
                               pipeline_mode=pl.Buffered(1)),
                  pl.BlockSpec((None, n_tiles, width, t),
                               lambda b, h, i: (b, 0, v_row // per + h, 0),
                               pipeline_mode=pl.Buffered(1)),
                  pl.BlockSpec((4, hd), lambda b, h, i: (0, 0)),
                  pl.BlockSpec((2 * hd, 1), lambda b, h, i: (0, 0))],
        out_specs=pl.BlockSpec((None, t, width), lambda b, h, i: (b, i, h)),
        out_shape=jax.ShapeDtypeStruct((bsz, seq, n_heads * 2 * hd), MXU_DTYPE),
        scratch_shapes=scratch,
        compiler_params=_params(("parallel", "parallel", "arbitrary"), 58),
    )(qk, qk, vt, lam_vecs.astype(F32), subln_w.astype(F32).reshape(2 * hd, 1))


def _swa_kernel(q_ref, kc_ref, kp_ref, vtc_ref, vtp_ref, sink_ref, o_ref):
    i = pl.program_id(1)
    w = C_WINDOW
    hd = C_HEAD_DIM
    gw = C_GROUP * hd
    lane = lax.broadcasted_iota(jnp.int32, (w, LANES), 1)
    lo = jnp.where(lane < hd, 1.0, 0.0).astype(q_ref.dtype)
    hi = jnp.where(lane >= hd, 1.0, 0.0).astype(q_ref.dtype)
    krow = lax.broadcasted_iota(jnp.int32, (2 * w, C_GROUP * w), 0)
    qcol = lax.broadcasted_iota(jnp.int32, (2 * w, C_GROUP * w), 1) & (w - 1)
    rel = w + qcol - krow
    valid = (rel >= 0) & (rel < w) & ((i - 1) * w + krow >= 0)
    ones = jnp.ones((ONES_ROWS, 2 * w), vtc_ref.dtype)
    groups = range(q_ref.shape[1] // gw)
    scores = []
    for g in groups:
        k_cols = slice(g * LANES, (g + 1) * LANES)
        kk = jnp.concatenate([kp_ref[:, k_cols], kc_ref[:, k_cols]], axis=0)
        q_heads = []
        for p in range(C_GROUP // 2):
            q2 = q_ref[:, g * gw + p * LANES:g * gw + (p + 1) * LANES]
            q_heads += [q2 * lo, q2 * hi]
        q_all = jnp.concatenate(q_heads, axis=0)
        s = lax.dot_general(kk, q_all, NT_DIMS, preferred_element_type=F32)
        scores.append(jnp.where(valid, s, MASKED))
    sinks = [sink_ref[g] * LOG2E for g in groups]
    maxes = [jnp.maximum(_col_max(s), sink) for s, sink in zip(scores, sinks)]
    probs = [jnp.concatenate([jnp.exp2(s[r:r + SLAB_ROWS] - m).astype(vtc_ref.dtype)
                              for r in range(0, 2 * w, SLAB_ROWS)], axis=0)
             for s, m in zip(scores, maxes)]
    for g in groups:
        v_rows = slice(g * hd, (g + 1) * hd)
        vt = jnp.concatenate([vtp_ref[v_rows, :], vtc_ref[v_rows, :]], axis=1)
        o = jnp.dot(jnp.concatenate([vt, ones], axis=0), probs[g], preferred_element_type=F32)
        o = o[:hd] / (o[hd:hd + 1] + jnp.exp2(sinks[g] - maxes[g]))
        o = jnp.concatenate([o[:, h * w:(h + 1) * w] for h in range(C_GROUP)], axis=0)
        o_ref[:, g * gw:(g + 1) * gw] = o.T.astype(o_ref.dtype)


def _swa_attention(qk, vt, sinks, n_kv):
    bsz, seq, _ = qk.shape
    w = C_WINDOW
    q_width = n_kv * C_GROUP * C_HEAD_DIM
    k_width = n_kv * LANES
    assert q_width % k_width == 0
    per_tile = vt.shape[-1] // w
    sink_cols = jnp.repeat(sinks.astype(F32).reshape(n_kv, 1, C_GROUP), w, axis=-1)
    prev = lambda i: jnp.maximum(i - 1, 0)
    vt_spec = lambda at: pl.BlockSpec(
        (None, None, n_kv * C_HEAD_DIM, w),
        lambda b, i: (b, at(i) // per_tile, 0, at(i) % per_tile))
    return pl.pallas_call(
        _swa_kernel,
        grid=(bsz, seq // w),
        in_specs=[pl.BlockSpec((None, w, q_width), lambda b, i: (b, i, 0)),
                  pl.BlockSpec((None, w, k_width), lambda b, i: (b, i, q_width // k_width)),
                  pl.BlockSpec((None, w, k_width), lambda b, i: (b, prev(i), q_width // k_width)),
                  vt_spec(lambda i: i),
                  vt_spec(prev),
                  pl.BlockSpec((n_kv, 1, C_GROUP * w), lambda b, i: (0, 0, 0))],
        out_specs=pl.BlockSpec((None, w, q_width), lambda b, i: (b, i, 0)),
        out_shape=jax.ShapeDtypeStruct((bsz, seq, q_width), MXU_DTYPE),
        compiler_params=_params(("parallel", "parallel"), 32),
    )(qk, qk, qk, vt, vt, sink_cols)


def _outproj_kernel(*refs, n_parts, has_bias):
    y_refs, w_refs = refs[:n_parts], refs[n_parts:2 * n_parts]
    rest = list(refs[2 * n_parts:])
    b_ref = rest.pop(0) if has_bias else None
    g_ref, x_ref, o_ref = rest
    acc = jnp.dot(y_refs[0][...], w_refs[0][...], preferred_element_type=F32)
    for y_ref, w_ref in zip(y_refs[1:], w_refs[1:]):
        acc = acc + jnp.dot(y_ref[...], w_ref[...], preferred_element_type=F32)
    if has_bias:
        acc = acc + b_ref[...]
    o_ref[...] = x_ref[...] + g_ref[...] * acc


def _out_project(x, gate, ys, ws, bias=None, in_place=True):
    bsz, seq, d = x.shape
    tm = min(ROW_TILE, seq)
    in_specs = [pl.BlockSpec((None, tm, y.shape[-1]), lambda b, i: (b, i, 0)) for y in ys]
    in_specs += [pl.BlockSpec(w.shape, lambda b, i: (0, 0)) for w in ws]
    args = list(ys) + list(ws)
    if bias is not None:
        in_specs.append(pl.BlockSpec((1, d), lambda b, i: (0, 0)))
        args.append(bias.reshape(1, d).astype(F32))
    in_specs += [pl.BlockSpec((None, 1, d), lambda b, i: (b, 0, 0)),
                 pl.BlockSpec((None, tm, d), lambda b, i: (b, i, 0))]
    args += [gate, x]
    return pl.pallas_call(
        functools.partial(_outproj_kernel, n_parts=len(ys), has_bias=bias is not None),
        grid=(bsz, seq // tm),
        in_specs=in_specs,
        out_specs=pl.BlockSpec((None, tm, d), lambda b, i: (b, i, 0)),
        out_shape=jax.ShapeDtypeStruct(x.shape, F32),
        input_output_aliases={len(args) - 1: 0} if in_place else {},
        compiler_params=_params(("parallel", "parallel"), 48),
    )(*args)


def _ffn_kernel(x_ref, sc_ref, sh_ref, g_ref, wg_ref, wu_ref, wo_ref, *rest, final):
    rest = list(rest)
    fw_ref = rest.pop(0) if final else None
    o_ref, h_ref = rest
    k = pl.program_id(2)
    tm = x_ref.shape[0]
    halves = [slice(r, r + min(ROW_TILE, tm)) for r in range(0, tm, ROW_TILE)]

    @pl.when(k == 0)
    def _():
        for rows in halves:
            h_ref[rows, :] = _norm_mod(x_ref[rows, :], sc_ref[...], sh_ref[...]).astype(h_ref.dtype)
        o_ref[...] = jnp.zeros_like(o_ref)

    for rows in halves:
        h = h_ref[rows, :]
        gt = jnp.dot(h, wg_ref[...], preferred_element_type=F32)
        up = jnp.dot(h, wu_ref[...], preferred_element_type=F32)
        act = ((gt * jax.nn.sigmoid(gt)) * up).astype(h.dtype)
        o_ref[rows, :] += jnp.dot(act, wo_ref[...], preferred_element_type=F32)

    @pl.when(k == pl.num_programs(2) - 1)
    def _():
        for rows in halves:
            xn = x_ref[rows, :] + g_ref[...] * o_ref[rows, :]
            if final:
                ms = jnp.mean(xn * xn, axis=-1, keepdims=True)
                xn = (xn * lax.rsqrt(ms + RMS_EPS)) * fw_ref[...]
            o_ref[rows, :] = xn


def _ffn(x, sc, sh, gate, w_in, w_out, final_w=None):
    bsz, seq, d = x.shape
    hidden = w_out.shape[0]
    tm = min(FFN_ROW_TILE, seq)
    th = _col_tile(hidden)
    n_h = hidden // th
    mod_spec = pl.BlockSpec((None, 1, d), lambda b, i, k: (b, 0, 0))
    in_specs = [pl.BlockSpec((None, tm, d), lambda b, i, k: (b, i, 0)),
                mod_spec, mod_spec, mod_spec,
                pl.BlockSpec((d, th), lambda b, i, k: (0, k)),
                pl.BlockSpec((d, th), lambda b, i, k: (0, n_h + k)),
                pl.BlockSpec((th, d), lambda b, i, k: (k, 0))]
    args = [x, sc, sh, gate, w_in, w_in, w_out]
    if final_w is not None:
        in_specs.append(pl.BlockSpec((1, d), lambda b, i, k: (0, 0)))
        args.append(final_w.reshape(1, d).astype(F32))
    return pl.pallas_call(
        functools.partial(_ffn_kernel, final=final_w is not None),
        grid=(bsz, seq // tm, n_h),
        in_specs=in_specs,
        out_specs=pl.BlockSpec((None, tm, d), lambda b, i, k: (b, i, 0)),
        out_shape=jax.ShapeDtypeStruct(x.shape, F32),
        scratch_shapes=[pltpu.VMEM((tm, d), MXU_DTYPE)],
        input_output_aliases={0: 0},
        compiler_params=_params(("parallel", "parallel", "arbitrary"), 60),
    )(*args)


def _rope_tables(seq, dim, query_scale):
    pos = jnp.arange(seq, dtype=F32)
    inv_freq = ROPE_THETA ** (-jnp.arange(0, dim, 2, dtype=F32) / dim)
    ang = pos[:, None] * inv_freq[None, :]
    cos, sin = jnp.cos(ang), jnp.sin(ang)
    reps = LANES // dim
    cos = jnp.tile(jnp.concatenate([cos, cos], axis=1), (1, reps))
    sin = jnp.tile(jnp.concatenate([-sin, sin], axis=1), (1, reps))
    return jnp.stack([cos * query_scale, cos]), jnp.stack([sin * query_scale, sin])


def _lambda_init(layer):
    return 0.8 - 0.6 * math.exp(-0.3 * layer)


def _dup_heads(w, n_heads, head_dim):
    lead = w.shape[:-1]
    w = w.reshape(lead + (n_heads, 1, head_dim))
    return jnp.broadcast_to(w, lead + (n_heads, 2, head_dim)).reshape(lead + (2 * n_heads * head_dim,))


def kernel(x, c, ada_w, ada_b, ab_w_in, ab_w_out, diff_lambda, diff_subln, swa_w_in, swa_b_in,
           swa_w_out, swa_b_out, swa_sinks, ffn_w_in, ffn_w_out, final_norm):
    bsz, seq, d = x.shape
    depth = ada_w.shape[0]
    a_heads, b_heads = d // 256, d // 512
    a_width, b_width = a_heads * A_HEAD_DIM, 2 * b_heads * B_HEAD_DIM
    assert a_width == b_width
    c_q_heads = d // C_HEAD_DIM
    c_kv_heads = c_q_heads // C_GROUP
    c_q_width, c_kv_width = c_q_heads * C_HEAD_DIM, c_kv_heads * C_HEAD_DIM

    rope_ab = _rope_tables(seq, A_HEAD_DIM, A_HEAD_DIM ** -0.5 * LOG2E)
    rope_c = _rope_tables(seq, C_HEAD_DIM, C_HEAD_DIM ** -0.5 * LOG2E)
    mod = _modulation(c, ada_w, ada_b)

    for layer in range(depth):
        sh1, sc1, g1, sh2, sc2, g2 = [
            mod[layer, :, None, m * d:(m + 1) * d] for m in range(N_MOD)]
        li = layer // 2
        if layer % 2 == 0:
            w_in = ab_w_in[li].astype(MXU_DTYPE)
            aq, ak, av, bq, bk, bv = jnp.split(
                w_in, [a_width, 2 * a_width, 3 * a_width,
                       3 * a_width + b_width, 3 * a_width + 2 * b_width], axis=1)
            qk = _project(x, sc1, sh1, jnp.concatenate([aq, ak, bq, bk], axis=1),
                          rope=rope_ab, rope_dim=A_HEAD_DIM,
                          is_query_col=lambda col: (col // a_width) % 2 == 0)
            vt = _project(x, sc1, sh1, jnp.concatenate([av, bv], axis=1), transpose_out=True)
            ya = _moba_attention(qk, vt, a_heads, q_col=0, k_col=a_heads, v_row=0)
            yb = _diff_attention(qk, vt, diff_lambda[li], diff_subln[li], _lambda_init(layer),
                                 b_heads, q_col=2 * a_heads, k_col=2 * a_heads + 2 * b_heads,
                                 v_row=a_width // (2 * B_HEAD_DIM))
            w_out = ab_w_out[li].astype(MXU_DTYPE)
            x = _out_project(x, g1, [ya, yb], [w_out[:a_width], w_out[a_width:]],
                             in_place=layer > 0)
        else:
            w_in, b_in = swa_w_in[li], swa_b_in[li]
            wq, wk, wv = jnp.split(w_in, [c_q_width, c_q_width + c_kv_width], axis=1)
            bq_, bk_, bv_ = jnp.split(b_in, [c_q_width, c_q_width + c_kv_width])
            w_qk = jnp.concatenate([wq, _dup_heads(wk, c_kv_heads, C_HEAD_DIM)], axis=1)
            b_qk = jnp.concatenate([bq_, _dup_heads(bk_, c_kv_heads, C_HEAD_DIM)])
            qk = _project(x, sc1, sh1, w_qk.astype(MXU_DTYPE), bias=b_qk,
                          rope=rope_c, rope_dim=C_HEAD_DIM,
                          is_query_col=lambda col: col < c_q_width)
            vt = _project(x, sc1, sh1, wv.astype(MXU_DTYPE), bias=bv_, transpose_out=True)
            y = _swa_attention(qk, vt, swa_sinks[li], c_kv_heads)
            x = _out_project(x, g1, [y], [swa_w_out[li].astype(MXU_DTYPE)], bias=swa_b_out[li])
        x = _ffn(x, sc2, sh2, g2, ffn_w_in[layer].astype(MXU_DTYPE),
                 ffn_w_out[layer].astype(MXU_DTYPE),
                 final_w=final_norm if layer == depth - 1 else None)
    return x
```

```python
import functools
import math

import jax
import jax.numpy as jnp
from jax import lax
from jax.experimental import pallas as pl
from jax.experimental.pallas import tpu as pltpu

F32 = jnp.float32
MXU_DTYPE = jnp.bfloat16

ROPE_THETA = 10000.0
RMS_EPS = 1e-6
DIFF_SUBLN_EPS = 1e-5
A_HEAD_DIM = 128
MOBA_BLOCK = 256
MOBA_BLOCK_LOG2 = 8
MOBA_TOPK = 3
B_HEAD_DIM = 128
C_HEAD_DIM = 64
C_GROUP = 8
C_WINDOW = 128
N_MOD = 6

LANES = 128
MASKED = -1e30
LOG2E = 1.4426950408889634

ATTN_TILE = 512
ROW_TILE = 512
FFN_ROW_TILE = 1024
MOD_COL_TILE = 1024
MIB = 2 ** 20

NT_DIMS = (((1,), (1,)), ((), ()))


def _params(semantics, vmem_mib):
    return pltpu.CompilerParams(dimension_semantics=semantics,
                                vmem_limit_bytes=vmem_mib * MIB)


def _col_tile(n):
    for t in (512, 256, 128):
        if n % t == 0:
            return t
    raise ValueError(f"column count {n} is not a multiple of {LANES}")


def _column_tiles(w, tn):
    k, n = w.shape
    return w.reshape(k, n // tn, tn).transpose(1, 0, 2)


def _norm_mod(x, sc, sh):
    ms = jnp.mean(x * x, axis=-1, keepdims=True)
    return (x * lax.rsqrt(ms + RMS_EPS)) * (1.0 + sc) + sh


def _mod_kernel(c_ref, w_ref, b_ref, o_ref):
    c = c_ref[...]
    cond = (c * jax.nn.sigmoid(c)).astype(MXU_DTYPE)
    o_ref[...] = jnp.dot(cond, w_ref[...].astype(MXU_DTYPE),
                         preferred_element_type=F32) + b_ref[...]


def _modulation(c, ada_w, ada_b):
    depth, d, n = ada_w.shape
    bsz = c.shape[0]
    rows = -(-bsz // 8) * 8
    c_pad = jnp.pad(c, ((0, rows - bsz), (0, 0)))
    tn = MOD_COL_TILE if n % MOD_COL_TILE == 0 else _col_tile(n)
    out = pl.pallas_call(
        _mod_kernel,
        grid=(depth, n // tn),
        in_specs=[pl.BlockSpec((rows, d), lambda l, j: (0, 0)),
                  pl.BlockSpec((None, d, tn), lambda l, j: (l, 0, j)),
                  pl.BlockSpec((None, 1, tn), lambda l, j: (l, 0, j))],
        out_specs=pl.BlockSpec((None, rows, tn), lambda l, j: (l, 0, j)),
        out_shape=jax.ShapeDtypeStruct((depth, rows, n), F32),
        compiler_params=_params(("parallel", "parallel"), 40),
    )(c_pad, ada_w, ada_b.reshape(depth, 1, n))
    return out[:, :bsz]


def _rope_partner(a, rope_dim):
    if rope_dim == LANES:
        return pltpu.roll(a, LANES // 2, 1)
    lane = lax.broadcasted_iota(jnp.int32, a.shape, 1)
    half = rope_dim // 2
    first_half = (lane & (rope_dim - 1)) < half
    return jnp.where(first_half, pltpu.roll(a, LANES - half, 1), pltpu.roll(a, half, 1))


def _proj_kernel(x_ref, sc_ref, sh_ref, w_ref, *rest, rope_dim, has_bias, transpose_out,
                 query_tiles):
    rest = list(rest)
    b_ref = rest.pop(0) if has_bias else None
    cos_ref, sin_ref = (rest.pop(0), rest.pop(0)) if rope_dim else (None, None)
    (o_ref,) = rest
    n_t, _, tn = w_ref.shape
    h = _norm_mod(x_ref[...], sc_ref[...], sh_ref[...]).astype(w_ref.dtype)
    for j in range(n_t):
        cols = slice(j * tn, (j + 1) * tn)
        acc = jnp.dot(h, w_ref[j], preferred_element_type=F32)
        if has_bias:
            acc = acc + b_ref[:, cols]
        if rope_dim:
            table = 0 if query_tiles[j] else 1
            cos, sin = cos_ref[table], sin_ref[table]
            for t in range(tn // LANES):
                a = acc[:, t * LANES:(t + 1) * LANES]
                o_ref[:, j * tn + t * LANES:j * tn + (t + 1) * LANES] = (
                    a * cos + _rope_partner(a, rope_dim) * sin).astype(o_ref.dtype)
        elif transpose_out:
            o_ref[cols, :] = acc.T.astype(o_ref.dtype)
        else:
            o_ref[:, cols] = acc.astype(o_ref.dtype)


def _project(x, sc, sh, w, bias=None, rope=None, rope_dim=0, is_query_col=None,
             transpose_out=False):
    bsz, seq, d = x.shape
    n = w.shape[1]
    tm = min(ROW_TILE, seq)
    tn = _col_tile(n)
    n_t = n // tn
    in_specs = [pl.BlockSpec((None, tm, d), lambda b, i: (b, i, 0)),
                pl.BlockSpec((None, 1, d), lambda b, i: (b, 0, 0)),
                pl.BlockSpec((None, 1, d), lambda b, i: (b, 0, 0)),
                pl.BlockSpec((n_t, d, tn), lambda b, i: (0, 0, 0), pipeline_mode=pl.Buffered(1))]
    args = [x, sc, sh, _column_tiles(w, tn)]
    if bias is not None:
        in_specs.append(pl.BlockSpec((1, n), lambda b, i: (0, 0)))
        args.append(bias.reshape(1, n).astype(F32))
    query_tiles = None
    if rope_dim:
        query_tiles = tuple(bool(is_query_col(j * tn)) for j in range(n_t))
        in_specs += [pl.BlockSpec((2, tm, LANES), lambda b, i: (0, i, 0))] * 2
        args += list(rope)
    if transpose_out:
        out_shape = jax.ShapeDtypeStruct((bsz, seq // tm, n, tm), MXU_DTYPE)
        out_spec = pl.BlockSpec((None, None, n, tm), lambda b, i: (b, i, 0, 0))
    else:
        out_shape = jax.ShapeDtypeStruct((bsz, seq, n), MXU_DTYPE)
        out_spec = pl.BlockSpec((None, tm, n), lambda b, i: (b, i, 0))
    return pl.pallas_call(
        functools.partial(_proj_kernel, rope_dim=rope_dim, has_bias=bias is not None,
                          transpose_out=transpose_out, query_tiles=query_tiles),
        grid=(bsz, seq // tm),
        in_specs=in_specs,
        out_specs=out_spec,
        out_shape=out_shape,
        compiler_params=_params(("parallel", "parallel"), 48),
    )(*args)


N_FLASH_REFS = 10
SLAB_ROWS = 64
ONES_ROWS = 16
FLASH_GROUP = 2


def _flash_scratch(t, v_dim):
    stat = pltpu.VMEM((1, t), F32)
    scores = pltpu.VMEM((t, t), F32)
    probs = pltpu.VMEM((t, t), MXU_DTYPE)
    return [stat, pltpu.VMEM((v_dim + ONES_ROWS, t), F32),
            scores, scores, stat, stat, probs, probs, stat, stat]


class _FlashStream:
    def __init__(self, refs):
        self.m, self.acc = refs[:2]
        self.s, self.s_max, self.p, self.alpha = refs[2:4], refs[4:6], refs[6:8], refs[8:10]

    def result(self):
        v_dim = self.acc.shape[0] - ONES_ROWS
        return self.acc[:v_dim, :] / self.acc[v_dim:v_dim + 1, :]


def _col_max(x):
    slab = x[:SLAB_ROWS]
    for r in range(SLAB_ROWS, x.shape[0], SLAB_ROWS):
        slab = jnp.maximum(slab, x[r:r + SLAB_ROWS])
    return jnp.max(slab, axis=0, keepdims=True)


def _causal_flash(i, streams, scores_of, values_of):
    for st in streams:
        st.m[...] = jnp.full_like(st.m, MASKED)
        st.acc[...] = jnp.zeros_like(st.acc)
        st.alpha[1][...] = jnp.ones_like(st.alpha[1])
        st.p[1][...] = jnp.zeros_like(st.p[1])

    def put_scores(st, slot, s):
        st.s[slot][...] = s
        st.s_max[slot][...] = _col_max(s)

    for idx, st in enumerate(streams):
        s = scores_of(idx, i)
        krow = lax.broadcasted_iota(jnp.int32, s.shape, 0)
        qcol = lax.broadcasted_iota(jnp.int32, s.shape, 1)
        put_scores(st, 0, jnp.where(krow <= qcol, s, MASKED))
    n_steps = i + 1

    def values_step(k, slot):
        tile = jnp.where(k <= 0, i, k - 1)
        for idx, st in enumerate(streams):
            vt = values_of(idx, tile)
            vt = jnp.concatenate([vt, jnp.ones((ONES_ROWS, vt.shape[1]), vt.dtype)], axis=0)
            st.acc[...] = st.alpha[slot][...] * st.acc[...] + jnp.dot(
                vt, st.p[slot][...], preferred_element_type=F32)

    def step(k, cur, is_last):
        nxt = 1 - cur
        ahead = k
        tile = jnp.where(k <= 1, i, k - 2)
        for g0 in range(0, len(streams), FLASH_GROUP):
            group = list(enumerate(streams))[g0:g0 + FLASH_GROUP]
            for idx, st in group:
                m_prev = st.m[...]
                m_new = jnp.maximum(m_prev, st.s_max[cur][...])
                for r in range(0, st.s[cur].shape[0], SLAB_ROWS):
                    p = jnp.exp2(st.s[cur][r:r + SLAB_ROWS, :] - m_new)
                    st.p[cur][r:r + SLAB_ROWS, :] = p.astype(st.p[cur].dtype)
                st.m[...] = m_new
                st.alpha[cur][...] = jnp.exp2(m_prev - m_new)
            if not is_last:
                for idx, st in group:
                    put_scores(st, nxt, scores_of(idx, ahead))
            for idx, st in group:
                vt = values_of(idx, tile)
                vt = jnp.concatenate([vt, jnp.ones((ONES_ROWS, vt.shape[1]), vt.dtype)], axis=0)
                st.acc[...] = st.alpha[nxt][...] * st.acc[...] + jnp.dot(
                    vt, st.p[nxt][...], preferred_element_type=F32)

    def pair_body(kk, carry):
        step(2 * kk, 0, False)
        step(2 * kk + 1, 1, False)
        return carry

    n_pairs = (n_steps - 1) // 2
    lax.fori_loop(0, n_pairs, pair_body, 0)
    done = 2 * n_pairs

    @pl.when(n_steps - done == 1)
    def _():
        step(done, 0, True)
        values_step(done, 0)

    @pl.when(n_steps - done == 2)
    def _():
        step(done, 0, False)
        step(done + 1, 1, True)
        values_step(done + 1, 1)


MOBA_HEADS_PER_STEP = 4


def _moba_kernel(q_ref, k_ref, vt_ref, o_ref, kmean_ref, qaug_ref, *state, n_blocks):
    i = pl.program_id(2)
    tq = q_ref.shape[0]
    tk = tq
    hd = A_HEAD_DIM
    heads = [slice(e * hd, (e + 1) * hd) for e in range(MOBA_HEADS_PER_STEP)]

    @pl.when(i == 0)
    def _():
        kmean_ref[...] = jnp.zeros_like(kmean_ref)

        def mean_body(n, carry):
            kb = k_ref[pl.ds(pl.multiple_of(n * MOBA_BLOCK, MOBA_BLOCK), MOBA_BLOCK), :]
            mean = jnp.sum(kb.astype(F32), axis=0, keepdims=True) * (1.0 / MOBA_BLOCK)
            for e, cols in enumerate(heads):
                kmean_ref[e, pl.ds(n, 1), :] = mean[:, cols]
            return carry

        lax.fori_loop(0, n_blocks, mean_body, 0)

    rows = min(LANES, -(-n_blocks // 16) * 16)
    blk = lax.broadcasted_iota(jnp.int32, (rows, tq), 0)
    qpos = lax.broadcasted_iota(jnp.int32, (rows, tq), 1)
    own = jnp.right_shift(i * tq + qpos, MOBA_BLOCK_LOG2)
    past = blk < own
    blkf = blk.astype(F32)
    for e, cols in enumerate(heads):
        q_t = q_ref[:, cols].astype(F32).T.astype(MXU_DTYPE)
        km = kmean_ref[e, :rows, :]
        k_hi = km.astype(MXU_DTYPE)
        r1 = km - k_hi.astype(F32)
        k_mid = r1.astype(MXU_DTYPE)
        k_lo = (r1 - k_mid.astype(F32)).astype(MXU_DTYPE)
        gate = (jnp.dot(k_lo, q_t, preferred_element_type=F32)
                + jnp.dot(k_mid, q_t, preferred_element_type=F32)
                + jnp.dot(k_hi, q_t, preferred_element_type=F32))
        g = jnp.where(past, gate, -jnp.inf)
        chosen = jnp.zeros(gate.shape, jnp.bool_)
        for _ in range(min(MOBA_TOPK, n_blocks)):
            best = jnp.max(g, axis=0, keepdims=True)
            first_best = jnp.min(jnp.where(g == best, blkf, float(LANES)), axis=0, keepdims=True)
            pick = blkf == first_best
            chosen = jnp.logical_or(chosen, pick)
            g = jnp.where(pick, -jnp.inf, g)
        visible = jnp.logical_or(jnp.logical_and(chosen, past), blk == own)
        qaug_ref[e, :hd, :] = q_t
        qaug_ref[e, hd:hd + rows, :] = jnp.where(visible, 0.0, MASKED).astype(qaug_ref.dtype)
        if rows < LANES:
            qaug_ref[e, hd + rows:, :] = jnp.zeros((LANES - rows, tq), qaug_ref.dtype)

    def scores_of(e, j):
        kt = k_ref[pl.ds(pl.multiple_of(j * tk, tk), tk), heads[e]]
        lane = lax.broadcasted_iota(jnp.int32, (1, LANES), 1)
        blocks_per_tile = tk // MOBA_BLOCK
        onehot = jnp.concatenate(
            [jnp.broadcast_to(jnp.where(lane == j * blocks_per_tile + b, 1.0, 0.0),
                              (MOBA_BLOCK, LANES)) for b in range(blocks_per_tile)], axis=0)
        k_aug = jnp.concatenate([kt, onehot.astype(kt.dtype)], axis=1)
        return jnp.dot(k_aug, qaug_ref[e], preferred_element_type=F32)

    streams = [_FlashStream(state[e * N_FLASH_REFS:(e + 1) * N_FLASH_REFS])
               for e in range(MOBA_HEADS_PER_STEP)]
    _causal_flash(i, streams, scores_of, lambda e, j: vt_ref[j, heads[e], :])
    for e, st in enumerate(streams):
        o_ref[:, heads[e]] = st.result().T.astype(o_ref.dtype)


def _moba_attention(qk, vt, n_heads, q_col, k_col, v_row):
    bsz, seq, _ = qk.shape
    t = min(ATTN_TILE, seq)
    n_tiles = seq // t
    n_blocks = seq // MOBA_BLOCK
    per = MOBA_HEADS_PER_STEP
    assert seq % t == 0 and t % MOBA_BLOCK == 0 and n_blocks <= LANES
    assert n_heads % per == 0 and q_col % per == 0 and k_col % per == 0 and v_row % per == 0
    hd = A_HEAD_DIM
    scratch = [pltpu.VMEM((per, LANES, hd), F32), pltpu.VMEM((per, 2 * hd, t), MXU_DTYPE)]
    for _ in range(per):
        scratch += _flash_scratch(t, hd)
    return pl.pallas_call(
        functools.partial(_moba_kernel, n_blocks=n_blocks),
        grid=(bsz, n_heads // per, n_tiles),
        in_specs=[pl.BlockSpec((None, t, per * hd), lambda b, h, i: (b, i, q_col // per + h)),
                  pl.BlockSpec((None, seq, per * hd), lambda b, h, i: (b, 0, k_col // per + h),
                               pipeline_mode=pl.Buffered(1)),
                  pl.BlockSpec((None, n_tiles, per * hd, t),
                               lambda b, h, i: (b, 0, v_row // per + h, 0),
                               pipeline_mode=pl.Buffered(1))],
        out_specs=pl.BlockSpec((None, t, per * hd), lambda b, h, i: (b, i, h)),
        out_shape=jax.ShapeDtypeStruct((bsz, seq, n_heads * hd), MXU_DTYPE),
        scratch_shapes=scratch,
        compiler_params=_params(("parallel", "parallel", "arbitrary"), 56),
    )(qk, qk, vt)


DIFF_HEADS_PER_STEP = 2


def _diff_kernel(q_ref, k_ref, vt_ref, lam_ref, w_ref, o_ref, *state, lambda_init):
    i = pl.program_id(2)
    tk = q_ref.shape[0]
    hd = B_HEAD_DIM
    n_streams = 2 * DIFF_HEADS_PER_STEP
    streams = [_FlashStream(state[s * N_FLASH_REFS:(s + 1) * N_FLASH_REFS])
               for s in range(n_streams)]

    def scores_of(s, j):
        cols = slice(s * hd, (s + 1) * hd)
        rows = pl.ds(pl.multiple_of(j * tk, tk), tk)
        return lax.dot_general(k_ref[rows, cols], q_ref[:, cols], NT_DIMS,
                               preferred_element_type=F32)

    def values_of(s, j):
        head = s // 2
        return vt_ref[j, head * 2 * hd:(head + 1) * 2 * hd, :]

    _causal_flash(i, streams, scores_of, values_of)

    lv = lam_ref[...]
    lam = (jnp.exp(jnp.sum(lv[0:1] * lv[1:2], axis=1, keepdims=True))
           - jnp.exp(jnp.sum(lv[2:3] * lv[3:4], axis=1, keepdims=True)) + lambda_init)
    for head in range(DIFF_HEADS_PER_STEP):
        y = streams[2 * head].result() - lam * streams[2 * head + 1].result()
        ms = jnp.mean(y * y, axis=0, keepdims=True)
        y = (y * lax.rsqrt(ms + DIFF_SUBLN_EPS)) * w_ref[...]
        o_ref[:, head * 2 * hd:(head + 1) * 2 * hd] = (
            (y * (1.0 - lambda_init)).T.astype(o_ref.dtype))


def _diff_attention(qk, vt, lam_vecs, subln_w, lambda_init, n_heads, q_col, k_col, v_row):
    bsz, seq, _ = qk.shape
    t = min(ATTN_TILE, seq)
    n_tiles = seq // t
    hd = B_HEAD_DIM
    per = DIFF_HEADS_PER_STEP
    assert n_heads % per == 0 and q_col % (2 * per) == 0 and k_col % (2 * per) == 0
    assert v_row % per == 0
    width = 2 * per * hd
    scratch = []
    for _ in range(2 * per):
        scratch += _flash_scratch(t, 2 * hd)
    return pl.pallas_call(
        functools.partial(_diff_kernel, lambda_init=lambda_init),
        grid=(bsz, n_heads // per, n_tiles),
        in_specs=[pl.BlockSpec((None, t, width), lambda b, h, i: (b, i, q_col // (2 * per) + h)),
                  pl.BlockSpec((None, seq, width), lambda b, h, i: (b, 0, k_col // (2 * per) + h),
                               pipeline_mode=pl.Buffered(1)),
                  pl.BlockSpec((None, n_tiles, width, t),
                               lambda b, h, i: (b, 0, v_row // per + h, 0),
                               pipeline_mode=pl.Buffered(1)),
                  pl.BlockSpec((4, hd), lambda b, h, i: (0, 0)),
                  pl.BlockSpec((2 * hd, 1), lambda b, h, i: (0, 0))],
        out_specs=pl.BlockSpec((None, t, width), lambda b, h, i: (b, i, h)),
        out_shape=jax.ShapeDtypeStruct((bsz, seq, n_heads * 2 * hd), MXU_DTYPE),
        scratch_shapes=scratch,
        compiler_params=_params(("parallel", "parallel", "arbitrary"), 58),
    )(qk, qk, vt, lam_vecs.astype(F32), subln_w.astype(F32).reshape(2 * hd, 1))


def _swa_kernel(q_ref, kc_ref, kp_ref, vtc_ref, vtp_ref, sink_ref, o_ref):
    i = pl.program_id(1)
    w = C_WINDOW
    hd = C_HEAD_DIM
    gw = C_GROUP * hd
    lane = lax.broadcasted_iota(jnp.int32, (w, LANES), 1)
    lo = jnp.where(lane < hd, 1.0, 0.0).astype(q_ref.dtype)
    hi = jnp.where(lane >= hd, 1.0, 0.0).astype(q_ref.dtype)
    krow = lax.broadcasted_iota(jnp.int32, (2 * w, C_GROUP * w), 0)
    qcol = lax.broadcasted_iota(jnp.int32, (2 * w, C_GROUP * w), 1) & (w - 1)
    rel = w + qcol - krow
    valid = (rel >= 0) & (rel < w) & ((i - 1) * w + krow >= 0)
    ones = jnp.ones((ONES_ROWS, 2 * w), vtc_ref.dtype)
    groups = range(q_ref.shape[1] // gw)
    scores = []
    for g in groups:
        k_cols = slice(g * LANES, (g + 1) * LANES)
        kk = jnp.concatenate([kp_ref[:, k_cols], kc_ref[:, k_cols]], axis=0)
        q_heads = []
        for p in range(C_GROUP // 2):
            q2 = q_ref[:, g * gw + p * LANES:g * gw + (p + 1) * LANES]
            q_heads += [q2 * lo, q2 * hi]
        q_all = jnp.concatenate(q_heads, axis=0)
        s = lax.dot_general(kk, q_all, NT_DIMS, preferred_element_type=F32)
        scores.append(jnp.where(valid, s, MASKED))
    sinks = [sink_ref[g] * LOG2E for g in groups]
    maxes = [jnp.maximum(_col_max(s), sink) for s, sink in zip(scores, sinks)]
    probs = [jnp.concatenate([jnp.exp2(s[r:r + SLAB_ROWS] - m).astype(vtc_ref.dtype)
                              for r in range(0, 2 * w, SLAB_ROWS)], axis=0)
             for s, m in zip(scores, maxes)]
    for g in groups:
        v_rows = slice(g * hd, (g + 1) * hd)
        vt = jnp.concatenate([vtp_ref[v_rows, :], vtc_ref[v_rows, :]], axis=1)
        o = jnp.dot(jnp.concatenate([vt, ones], axis=0), probs[g], preferred_element_type=F32)
        o = o[:hd] / (o[hd:hd + 1] + jnp.exp2(sinks[g] - maxes[g]))
        o = jnp.concatenate([o[:, h * w:(h + 1) * w] for h in range(C_GROUP)], axis=0)
        o_ref[:, g * gw:(g + 1) * gw] = o.T.astype(o_ref.dtype)


def _swa_attention(qk, vt, sinks, n_kv):
    bsz, seq, _ = qk.shape
    w = C_WINDOW
    q_width = n_kv * C_GROUP * C_HEAD_DIM
    k_width = n_kv * LANES
    assert q_width % k_width == 0
    per_tile = vt.shape[-1] // w
    sink_cols = jnp.repeat(sinks.astype(F32).reshape(n_kv, 1, C_GROUP), w, axis=-1)
    prev = lambda i: jnp.maximum(i - 1, 0)
    vt_spec = lambda at: pl.BlockSpec(
        (None, None, n_kv * C_HEAD_DIM, w),
        lambda b, i: (b, at(i) // per_tile, 0, at(i) % per_tile))
    return pl.pallas_call(
        _swa_kernel,
        grid=(bsz, seq // w),
        in_specs=[pl.BlockSpec((None, w, q_width), lambda b, i: (b, i, 0)),
                  pl.BlockSpec((None, w, k_width), lambda b, i: (b, i, q_width // k_width)),
                  pl.BlockSpec((None, w, k_width), lambda b, i: (b, prev(i), q_width // k_width)),
                  vt_spec(lambda i: i),
                  vt_spec(prev),
                  pl.BlockSpec((n_kv, 1, C_GROUP * w), lambda b, i: (0, 0, 0))],
        out_specs=pl.BlockSpec((None, w, q_width), lambda b, i: (b, i, 0)),
        out_shape=jax.ShapeDtypeStruct((bsz, seq, q_width), MXU_DTYPE),
        compiler_params=_params(("parallel", "parallel"), 32),
    )(qk, qk, qk, vt, vt, sink_cols)


def _outproj_kernel(*refs, n_parts, has_bias):
    y_refs, w_refs = refs[:n_parts], refs[n_parts:2 * n_parts]
    rest = list(refs[2 * n_parts:])
    b_ref = rest.pop(0) if has_bias else None
    g_ref, x_ref, o_ref = rest
    acc = jnp.dot(y_refs[0][...], w_refs[0][...], preferred_element_type=F32)
    for y_ref, w_ref in zip(y_refs[1:], w_refs[1:]):
        acc = acc + jnp.dot(y_ref[...], w_ref[...], preferred_element_type=F32)
    if has_bias:
        acc = acc + b_ref[...]
    o_ref[...] = x_ref[...] + g_ref[...] * acc


def _out_project(x, gate, ys, ws, bias=None, in_place=True):
    bsz, seq, d = x.shape
    tm = min(ROW_TILE, seq)
    in_specs = [pl.BlockSpec((None, tm, y.shape[-1]), lambda b, i: (b, i, 0)) for y in ys]
    in_specs += [pl.BlockSpec(w.shape, lambda b, i: (0, 0)) for w in ws]
    args = list(ys) + list(ws)
    if bias is not None:
        in_specs.append(pl.BlockSpec((1, d), lambda b, i: (0, 0)))
        args.append(bias.reshape(1, d).astype(F32))
    in_specs += [pl.BlockSpec((None, 1, d), lambda b, i: (b, 0, 0)),
                 pl.BlockSpec((None, tm, d), lambda b, i: (b, i, 0))]
    args += [gate, x]
    return pl.pallas_call(
        functools.partial(_outproj_kernel, n_parts=len(ys), has_bias=bias is not None),
        grid=(bsz, seq // tm),
        in_specs=in_specs,
        out_specs=pl.BlockSpec((None, tm, d), lambda b, i: (b, i, 0)),
        out_shape=jax.ShapeDtypeStruct(x.shape, F32),
        input_output_aliases={len(args) - 1: 0} if in_place else {},
        compiler_params=_params(("parallel", "parallel"), 48),
    )(*args)


def _ffn_kernel(x_ref, sc_ref, sh_ref, g_ref, wg_ref, wu_ref, wo_ref, *rest, final):
    rest = list(rest)
    fw_ref = rest.pop(0) if final else None
    o_ref, h_ref = rest
    k = pl.program_id(2)
    tm = x_ref.shape[0]
    halves = [slice(r, r + min(ROW_TILE, tm)) for r in range(0, tm, ROW_TILE)]

    @pl.when(k == 0)
    def _():
        for rows in halves:
            h_ref[rows, :] = _norm_mod(x_ref[rows, :], sc_ref[...], sh_ref[...]).astype(h_ref.dtype)
        o_ref[...] = jnp.zeros_like(o_ref)

    for rows in halves:
        h = h_ref[rows, :]
        gt = jnp.dot(h, wg_ref[...], preferred_element_type=F32)
        up = jnp.dot(h, wu_ref[...], preferred_element_type=F32)
        act = ((gt * jax.nn.sigmoid(gt)) * up).astype(h.dtype)
        o_ref[rows, :] += jnp.dot(act, wo_ref[...], preferred_element_type=F32)

    @pl.when(k == pl.num_programs(2) - 1)
    def _():
        for rows in halves:
            xn = x_ref[rows, :] + g_ref[...] * o_ref[rows, :]
            if final:
                ms = jnp.mean(xn * xn, axis=-1, keepdims=True)
                xn = (xn * lax.rsqrt(ms + RMS_EPS)) * fw_ref[...]
            o_ref[rows, :] = xn


def _ffn(x, sc, sh, gate, w_in, w_out, final_w=None):
    bsz, seq, d = x.shape
    hidden = w_out.shape[0]
    tm = min(FFN_ROW_TILE, seq)
    th = _col_tile(hidden)
    n_h = hidden // th
    mod_spec = pl.BlockSpec((None, 1, d), lambda b, i, k: (b, 0, 0))
    in_specs = [pl.BlockSpec((None, tm, d), lambda b, i, k: (b, i, 0)),
                mod_spec, mod_spec, mod_spec,
                pl.BlockSpec((d, th), lambda b, i, k: (0, k)),
                pl.BlockSpec((d, th), lambda b, i, k: (0, n_h + k)),
                pl.BlockSpec((th, d), lambda b, i, k: (k, 0))]
    args = [x, sc, sh, gate, w_in, w_in, w_out]
    if final_w is not None:
        in_specs.append(pl.BlockSpec((1, d), lambda b, i, k: (0, 0)))
        args.append(final_w.reshape(1, d).astype(F32))
    return pl.pallas_call(
        functools.partial(_ffn_kernel, final=final_w is not None),
        grid=(bsz, seq // tm, n_h),
        in_specs=in_specs,
        out_specs=pl.BlockSpec((None, tm, d), lambda b, i, k: (b, i, 0)),
        out_shape=jax.ShapeDtypeStruct(x.shape, F32),
        scratch_shapes=[pltpu.VMEM((tm, d), MXU_DTYPE)],
        input_output_aliases={0: 0},
        compiler_params=_params(("parallel", "parallel", "arbitrary"), 60),
    )(*args)


def _rope_tables(seq, dim, query_scale):
    pos = jnp.arange(seq, dtype=F32)
    inv_freq = ROPE_THETA ** (-jnp.arange(0, dim, 2, dtype=F32) / dim)
    ang = pos[:, None] * inv_freq[None, :]
    cos, sin = jnp.cos(ang), jnp.sin(ang)
    reps = LANES // dim
    cos = jnp.tile(jnp.concatenate([cos, cos], axis=1), (1, reps))
    sin = jnp.tile(jnp.concatenate([-sin, sin], axis=1), (1, reps))
    return jnp.stack([cos * query_scale, cos]), jnp.stack([sin * query_scale, sin])


def _lambda_init(layer):
    return 0.8 - 0.6 * math.exp(-0.3 * layer)


def _dup_heads(w, n_heads, head_dim):
    lead = w.shape[:-1]
    w = w.reshape(lead + (n_heads, 1, head_dim))
    return jnp.broadcast_to(w, lead + (n_heads, 2, head_dim)).reshape(lead + (2 * n_heads * head_dim,))


def kernel(x, c, ada_w, ada_b, ab_w_in, ab_w_out, diff_lambda, diff_subln, swa_w_in, swa_b_in,
           swa_w_out, swa_b_out, swa_sinks, ffn_w_in, ffn_w_out, final_norm):
    bsz, seq, d = x.shape
    depth = ada_w.shape[0]
    a_heads, b_heads = d // 256, d // 512
    a_width, b_width = a_heads * A_HEAD_DIM, 2 * b_heads * B_HEAD_DIM
    assert a_width == b_width
    c_q_heads = d // C_HEAD_DIM
    c_kv_heads = c_q_heads // C_GROUP
    c_q_width, c_kv_width = c_q_heads * C_HEAD_DIM, c_kv_heads * C_HEAD_DIM

    rope_ab = _rope_tables(seq, A_HEAD_DIM, A_HEAD_DIM ** -0.5 * LOG2E)
    rope_c = _rope_tables(seq, C_HEAD_DIM, C_HEAD_DIM ** -0.5 * LOG2E)
    mod = _modulation(c, ada_w, ada_b)

    for layer in range(depth):
        sh1, sc1, g1, sh2, sc2, g2 = [
            mod[layer, :, None, m * d:(m + 1) * d] for m in range(N_MOD)]
        li = layer // 2
        if layer % 2 == 0:
            w_in = ab_w_in[li].astype(MXU_DTYPE)
            aq, ak, av, bq, bk, bv = jnp.split(
                w_in, [a_width, 2 * a_width, 3 * a_width,
                       3 * a_width + b_width, 3 * a_width + 2 * b_width], axis=1)
            qk = _project(x, sc1, sh1, jnp.concatenate([aq, ak, bq, bk], axis=1),
                          rope=rope_ab, rope_dim=A_HEAD_DIM,
                          is_query_col=lambda col: (col // a_width) % 2 == 0)
            vt = _project(x, sc1, sh1, jnp.concatenate([av, bv], axis=1), transpose_out=True)
            ya = _moba_attention(qk, vt, a_heads, q_col=0, k_col=a_heads, v_row=0)
            yb = _diff_attention(qk, vt, diff_lambda[li], diff_subln[li], _lambda_init(layer),
                                 b_heads, q_col=2 * a_heads, k_col=2 * a_heads + 2 * b_heads,
                                 v_row=a_width // (2 * B_HEAD_DIM))
            w_out = ab_w_out[li].astype(MXU_DTYPE)
            x = _out_project(x, g1, [ya, yb], [w_out[:a_width], w_out[a_width:]],
                             in_place=layer > 0)
        else:
            w_in, b_in = swa_w_in[li], swa_b_in[li]
            wq, wk, wv = jnp.split(w_in, [c_q_width, c_q_width + c_kv_width], axis=1)
            bq_, bk_, bv_ = jnp.split(b_in, [c_q_width, c_q_width + c_kv_width])
            w_qk = jnp.concatenate([wq, _dup_heads(wk, c_kv_heads, C_HEAD_DIM)], axis=1)
            b_qk = jnp.concatenate([bq_, _dup_heads(bk_, c_kv_heads, C_HEAD_DIM)])
            qk = _project(x, sc1, sh1, w_qk.astype(MXU_DTYPE), bias=b_qk,
                          rope=rope_c, rope_dim=C_HEAD_DIM,
                          is_query_col=lambda col: col < c_q_width)
            vt = _project(x, sc1, sh1, wv.astype(MXU_DTYPE), bias=bv_, transpose_out=True)
            y = _swa_attention(qk, vt, swa_sinks[li], c_kv_heads)
            x = _out_project(x, g1, [y], [swa_w_out[li].astype(MXU_DTYPE)], bias=swa_b_out[li])
        x = _ffn(x, sc2, sh2, g2, ffn_w_in[layer].astype(MXU_DTYPE),
                 ffn_w_out[layer].astype(MXU_DTYPE),
                 final_w=final_norm if layer == depth - 1 else None)
    return x
```

```python
import functools
import math

import jax
import jax.numpy as jnp
from jax import lax
from jax.experimental import pallas as pl
from jax.experimental.pallas import tpu as pltpu

F32 = jnp.float32
MXU_DTYPE = jnp.bfloat16

ROPE_THETA = 10000.0
RMS_EPS = 1e-6
DIFF_SUBLN_EPS = 1e-5
A_HEAD_DIM = 128
MOBA_BLOCK = 256
MOBA_BLOCK_LOG2 = 8
MOBA_TOPK = 3
B_HEAD_DIM = 128
C_HEAD_DIM = 64
C_GROUP = 8
C_WINDOW = 128
N_MOD = 6

LANES = 128
MASKED = -1e30
LOG2E = 1.4426950408889634

ATTN_TILE = 512
ROW_TILE = 512
FFN_ROW_TILE = 1024
MOD_COL_TILE = 1024
MIB = 2 ** 20

NT_DIMS = (((1,), (1,)), ((), ()))


def _params(semantics, vmem_mib):
    return pltpu.CompilerParams(dimension_semantics=semantics,
                                vmem_limit_bytes=vmem_mib * MIB)


def _col_tile(n):
    for t in (512, 256, 128):
        if n % t == 0:
            return t
    raise ValueError(f"column count {n} is not a multiple of {LANES}")


def _column_tiles(w, tn):
    k, n = w.shape
    return w.reshape(k, n // tn, tn).transpose(1, 0, 2)


def _norm_mod(x, sc, sh):
    ms = jnp.mean(x * x, axis=-1, keepdims=True)
    return (x * lax.rsqrt(ms + RMS_EPS)) * (1.0 + sc) + sh


def _mod_kernel(c_ref, w_ref, b_ref, o_ref):
    c = c_ref[...]
    cond = (c * jax.nn.sigmoid(c)).astype(MXU_DTYPE)
    o_ref[...] = jnp.dot(cond, w_ref[...].astype(MXU_DTYPE),
                         preferred_element_type=F32) + b_ref[...]


def _modulation(c, ada_w, ada_b):
    depth, d, n = ada_w.shape
    bsz = c.shape[0]
    rows = -(-bsz // 8) * 8
    c_pad = jnp.pad(c, ((0, rows - bsz), (0, 0)))
    tn = MOD_COL_TILE if n % MOD_COL_TILE == 0 else _col_tile(n)
    out = pl.pallas_call(
        _mod_kernel,
        grid=(depth, n // tn),
        in_specs=[pl.BlockSpec((rows, d), lambda l, j: (0, 0)),
                  pl.BlockSpec((None, d, tn), lambda l, j: (l, 0, j)),
                  pl.BlockSpec((None, 1, tn), lambda l, j: (l, 0, j))],
        out_specs=pl.BlockSpec((None, rows, tn), lambda l, j: (l, 0, j)),
        out_shape=jax.ShapeDtypeStruct((depth, rows, n), F32),
        compiler_params=_params(("parallel", "parallel"), 40),
    )(c_pad, ada_w, ada_b.reshape(depth, 1, n))
    return out[:, :bsz]


def _rope_partner(a, rope_dim):
    if rope_dim == LANES:
        return pltpu.roll(a, LANES // 2, 1)
    lane = lax.broadcasted_iota(jnp.int32, a.shape, 1)
    half = rope_dim // 2
    first_half = (lane & (rope_dim - 1)) < half
    return jnp.where(first_half, pltpu.roll(a, LANES - half, 1), pltpu.roll(a, half, 1))


def _proj_kernel(x_ref, sc_ref, sh_ref, w_ref, *rest, rope_dim, has_bias, transpose_out,
                 query_tiles):
    rest = list(rest)
    b_ref = rest.pop(0) if has_bias else None
    cos_ref, sin_ref = (rest.pop(0), rest.pop(0)) if rope_dim else (None, None)
    (o_ref,) = rest
    n_t, _, tn = w_ref.shape
    h = _norm_mod(x_ref[...], sc_ref[...], sh_ref[...]).astype(w_ref.dtype)
    for j in range(n_t):
        cols = slice(j * tn, (j + 1) * tn)
        acc = jnp.dot(h, w_ref[j], preferred_element_type=F32)
        if has_bias:
            acc = acc + b_ref[:, cols]
        if rope_dim:
            table = 0 if query_tiles[j] else 1
            cos, sin = cos_ref[table], sin_ref[table]
            for t in range(tn // LANES):
                a = acc[:, t * LANES:(t + 1) * LANES]
                o_ref[:, j * tn + t * LANES:j * tn + (t + 1) * LANES] = (
                    a * cos + _rope_partner(a, rope_dim) * sin).astype(o_ref.dtype)
        elif transpose_out:
            o_ref[cols, :] = acc.T.astype(o_ref.dtype)
        else:
            o_ref[:, cols] = acc.astype(o_ref.dtype)


def _project(x, sc, sh, w, bias=None, rope=None, rope_dim=0, is_query_col=None,
             transpose_out=False):
    bsz, seq, d = x.shape
    n = w.shape[1]
    tm = min(ROW_TILE, seq)
    tn = _col_tile(n)
    n_t = n // tn
    in_specs = [pl.BlockSpec((None, tm, d), lambda b, i: (b, i, 0)),
                pl.BlockSpec((None, 1, d), lambda b, i: (b, 0, 0)),
                pl.BlockSpec((None, 1, d), lambda b, i: (b, 0, 0)),
                pl.BlockSpec((n_t, d, tn), lambda b, i: (0, 0, 0), pipeline_mode=pl.Buffered(1))]
    args = [x, sc, sh, _column_tiles(w, tn)]
    if bias is not None:
        in_specs.append(pl.BlockSpec((1, n), lambda b, i: (0, 0)))
        args.append(bias.reshape(1, n).astype(F32))
    query_tiles = None
    if rope_dim:
        query_tiles = tuple(bool(is_query_col(j * tn)) for j in range(n_t))
        in_specs += [pl.BlockSpec((2, tm, LANES), lambda b, i: (0, i, 0))] * 2
        args += list(rope)
    if transpose_out:
        out_shape = jax.ShapeDtypeStruct((bsz, seq // tm, n, tm), MXU_DTYPE)
        out_spec = pl.BlockSpec((None, None, n, tm), lambda b, i: (b, i, 0, 0))
    else:
        out_shape = jax.ShapeDtypeStruct((bsz, seq, n), MXU_DTYPE)
        out_spec = pl.BlockSpec((None, tm, n), lambda b, i: (b, i, 0))
    return pl.pallas_call(
        functools.partial(_proj_kernel, rope_dim=rope_dim, has_bias=bias is not None,
                          transpose_out=transpose_out, query_tiles=query_tiles),
        grid=(bsz, seq // tm),
        in_specs=in_specs,
        out_specs=out_spec,
        out_shape=out_shape,
        compiler_params=_params(("parallel", "parallel"), 48),
    )(*args)


N_FLASH_REFS = 10
SLAB_ROWS = 64
ONES_ROWS = 16
FLASH_GROUP = 2


def _flash_scratch(t, v_dim):
    stat = pltpu.VMEM((1, t), F32)
    scores = pltpu.VMEM((t, t), F32)
    probs = pltpu.VMEM((t, t), MXU_DTYPE)
    return [stat, pltpu.VMEM((v_dim + ONES_ROWS, t), F32),
            scores, scores, stat, stat, probs, probs, stat, stat]


class _FlashStream:
    def __init__(self, refs):
        self.m, self.acc = refs[:2]
        self.s, self.s_max, self.p, self.alpha = refs[2:4], refs[4:6], refs[6:8], refs[8:10]

    def result(self):
        v_dim = self.acc.shape[0] - ONES_ROWS
        return self.acc[:v_dim, :] / self.acc[v_dim:v_dim + 1, :]


def _col_max(x):
    slab = x[:SLAB_ROWS]
    for r in range(SLAB_ROWS, x.shape[0], SLAB_ROWS):
        slab = jnp.maximum(slab, x[r:r + SLAB_ROWS])
    return jnp.max(slab, axis=0, keepdims=True)


def _causal_flash(i, streams, scores_of, values_of):
    for st in streams:
        st.m[...] = jnp.full_like(st.m, MASKED)
        st.acc[...] = jnp.zeros_like(st.acc)
        st.alpha[1][...] = jnp.ones_like(st.alpha[1])
        st.p[1][...] = jnp.zeros_like(st.p[1])

    def put_scores(st, slot, s):
        st.s[slot][...] = s
        st.s_max[slot][...] = _col_max(s)

    for idx, st in enumerate(streams):
        s = scores_of(idx, i)
        krow = lax.broadcasted_iota(jnp.int32, s.shape, 0)
        qcol = lax.broadcasted_iota(jnp.int32, s.shape, 1)
        put_scores(st, 0, jnp.where(krow <= qcol, s, MASKED))
    n_steps = i + 1

    def values_step(k, slot):
        tile = jnp.where(k <= 0, i, k - 1)
        for idx, st in enumerate(streams):
            vt = values_of(idx, tile)
            vt = jnp.concatenate([vt, jnp.ones((ONES_ROWS, vt.shape[1]), vt.dtype)], axis=0)
            st.acc[...] = st.alpha[slot][...] * st.acc[...] + jnp.dot(
                vt, st.p[slot][...], preferred_element_type=F32)

    def step(k, cur, is_last):
        nxt = 1 - cur
        ahead = k
        tile = jnp.where(k <= 1, i, k - 2)
        for g0 in range(0, len(streams), FLASH_GROUP):
            group = list(enumerate(streams))[g0:g0 + FLASH_GROUP]
            for idx, st in group:
                m_prev = st.m[...]
                m_new = jnp.maximum(m_prev, st.s_max[cur][...])
                for r in range(0, st.s[cur].shape[0], SLAB_ROWS):
                    p = jnp.exp2(st.s[cur][r:r + SLAB_ROWS, :] - m_new)
                    st.p[cur][r:r + SLAB_ROWS, :] = p.astype(st.p[cur].dtype)
                st.m[...] = m_new
                st.alpha[cur][...] = jnp.exp2(m_prev - m_new)
            if not is_last:
                for idx, st in group:
                    put_scores(st, nxt, scores_of(idx, ahead))
            for idx, st in group:
                vt = values_of(idx, tile)
                vt = jnp.concatenate([vt, jnp.ones((ONES_ROWS, vt.shape[1]), vt.dtype)], axis=0)
                st.acc[...] = st.alpha[nxt][...] * st.acc[...] + jnp.dot(
                    vt, st.p[nxt][...], preferred_element_type=F32)

    def pair_body(kk, carry):
        step(2 * kk, 0, False)
        step(2 * kk + 1, 1, False)
        return carry

    n_pairs = (n_steps - 1) // 2
    lax.fori_loop(0, n_pairs, pair_body, 0)
    done = 2 * n_pairs

    @pl.when(n_steps - done == 1)
    def _():
        step(done, 0, True)
        values_step(done, 0)

    @pl.when(n_steps - done == 2)
    def _():
        step(done, 0, False)
        step(done + 1, 1, True)
        values_step(done + 1, 1)


MOBA_HEADS_PER_STEP = 4


def _moba_kernel(q_ref, k_ref, vt_ref, o_ref, kmean_ref, qaug_ref, *state, n_blocks):
    i = pl.program_id(2)
    tq = q_ref.shape[0]
    tk = tq
    hd = A_HEAD_DIM
    heads = [slice(e * hd, (e + 1) * hd) for e in range(MOBA_HEADS_PER_STEP)]

    @pl.when(i == 0)
    def _():
        kmean_ref[...] = jnp.zeros_like(kmean_ref)

        def mean_body(n, carry):
            kb = k_ref[pl.ds(pl.multiple_of(n * MOBA_BLOCK, MOBA_BLOCK), MOBA_BLOCK), :]
            mean = jnp.sum(kb.astype(F32), axis=0, keepdims=True) * (1.0 / MOBA_BLOCK)
            for e, cols in enumerate(heads):
                kmean_ref[e, pl.ds(n, 1), :] = mean[:, cols]
            return carry

        lax.fori_loop(0, n_blocks, mean_body, 0)

    rows = min(LANES, -(-n_blocks // 16) * 16)
    blk = lax.broadcasted_iota(jnp.int32, (rows, tq), 0)
    qpos = lax.broadcasted_iota(jnp.int32, (rows, tq), 1)
    own = jnp.right_shift(i * tq + qpos, MOBA_BLOCK_LOG2)
    past = blk < own
    blkf = blk.astype(F32)
    for e, cols in enumerate(heads):
        q_t = q_ref[:, cols].astype(F32).T.astype(MXU_DTYPE)
        km = kmean_ref[e, :rows, :]
        k_hi = km.astype(MXU_DTYPE)
        r1 = km - k_hi.astype(F32)
        k_mid = r1.astype(MXU_DTYPE)
        k_lo = (r1 - k_mid.astype(F32)).astype(MXU_DTYPE)
        gate = (jnp.dot(k_lo, q_t, preferred_element_type=F32)
                + jnp.dot(k_mid, q_t, preferred_element_type=F32)
                + jnp.dot(k_hi, q_t, preferred_element_type=F32))
        g = jnp.where(past, gate, -jnp.inf)
        chosen = jnp.zeros(gate.shape, jnp.bool_)
        for _ in range(min(MOBA_TOPK, n_blocks)):
            best = jnp.max(g, axis=0, keepdims=True)
            first_best = jnp.min(jnp.where(g == best, blkf, float(LANES)), axis=0, keepdims=True)
            pick = blkf == first_best
            chosen = jnp.logical_or(chosen, pick)
            g = jnp.where(pick, -jnp.inf, g)
        visible = jnp.logical_or(jnp.logical_and(chosen, past), blk == own)
        qaug_ref[e, :hd, :] = q_t
        qaug_ref[e, hd:hd + rows, :] = jnp.where(visible, 0.0, MASKED).astype(qaug_ref.dtype)
        if rows < LANES:
            qaug_ref[e, hd + rows:, :] = jnp.zeros((LANES - rows, tq), qaug_ref.dtype)

    def scores_of(e, j):
        kt = k_ref[pl.ds(pl.multiple_of(j * tk, tk), tk), heads[e]]
        lane = lax.broadcasted_iota(jnp.int32, (1, LANES), 1)
        blocks_per_tile = tk // MOBA_BLOCK
        onehot = jnp.concatenate(
            [jnp.broadcast_to(jnp.where(lane == j * blocks_per_tile + b, 1.0, 0.0),
                              (MOBA_BLOCK, LANES)) for b in range(blocks_per_tile)], axis=0)
        k_aug = jnp.concatenate([kt, onehot.astype(kt.dtype)], axis=1)
        return jnp.dot(k_aug, qaug_ref[e], preferred_element_type=F32)

    streams = [_FlashStream(state[e * N_FLASH_REFS:(e + 1) * N_FLASH_REFS])
               for e in range(MOBA_HEADS_PER_STEP)]
    _causal_flash(i, streams, scores_of, lambda e, j: vt_ref[j, heads[e], :])
    for e, st in enumerate(streams):
        o_ref[:, heads[e]] = st.result().T.astype(o_ref.dtype)


def _moba_attention(qk, vt, n_heads, q_col, k_col, v_row):
    bsz, seq, _ = qk.shape
    t = min(ATTN_TILE, seq)
    n_tiles = seq // t
    n_blocks = seq // MOBA_BLOCK
    per = MOBA_HEADS_PER_STEP
    assert seq % t == 0 and t % MOBA_BLOCK == 0 and n_blocks <= LANES
    assert n_heads % per == 0 and q_col % per == 0 and k_col % per == 0 and v_row % per == 0
    hd = A_HEAD_DIM
    scratch = [pltpu.VMEM((per, LANES, hd), F32), pltpu.VMEM((per, 2 * hd, t), MXU_DTYPE)]
    for _ in range(per):
        scratch += _flash_scratch(t, hd)
    return pl.pallas_call(
        functools.partial(_moba_kernel, n_blocks=n_blocks),
        grid=(bsz, n_heads // per, n_tiles),
        in_specs=[pl.BlockSpec((None, t, per * hd), lambda b, h, i: (b, i, q_col // per + h)),
                  pl.BlockSpec((None, seq, per * hd), lambda b, h, i: (b, 0, k_col // per + h),
                               pipeline_mode=pl.Buffered(1)),
                  pl.BlockSpec((None, n_tiles, per * hd, t),
                               lambda b, h, i: (b, 0, v_row // per + h, 0),
                               pipeline_mode=pl.Buffered(1))],
        out_specs=pl.BlockSpec((None, t, per * hd), lambda b, h, i: (b, i, h)),
        out_shape=jax.ShapeDtypeStruct((bsz, seq, n_heads * hd), MXU_DTYPE),
        scratch_shapes=scratch,
        compiler_params=_params(("parallel", "parallel", "arbitrary"), 56),
    )(qk, qk, vt)


DIFF_HEADS_PER_STEP = 2


def _diff_kernel(q_ref, k_ref, vt_ref, lam_ref, w_ref, o_ref, *state, lambda_init):
    i = pl.program_id(2)
    tk = q_ref.shape[0]
    hd = B_HEAD_DIM
    n_streams = 2 * DIFF_HEADS_PER_STEP
    streams = [_FlashStream(state[s * N_FLASH_REFS:(s + 1) * N_FLASH_REFS])
               for s in range(n_streams)]

    def scores_of(s, j):
        cols = slice(s * hd, (s + 1) * hd)
        rows = pl.ds(pl.multiple_of(j * tk, tk), tk)
        return lax.dot_general(k_ref[rows, cols], q_ref[:, cols], NT_DIMS,
                               preferred_element_type=F32)

    def values_of(s, j):
        head = s // 2
        return vt_ref[j, head * 2 * hd:(head + 1) * 2 * hd, :]

    _causal_flash(i, streams, scores_of, values_of)

    lv = lam_ref[...]
    lam = (jnp.exp(jnp.sum(lv[0:1] * lv[1:2], axis=1, keepdims=True))
           - jnp.exp(jnp.sum(lv[2:3] * lv[3:4], axis=1, keepdims=True)) + lambda_init)
    for head in range(DIFF_HEADS_PER_STEP):
        y = streams[2 * head].result() - lam * streams[2 * head + 1].result()
        ms = jnp.mean(y * y, axis=0, keepdims=True)
        y = (y * lax.rsqrt(ms + DIFF_SUBLN_EPS)) * w_ref[...]
        o_ref[:, head * 2 * hd:(head + 1) * 2 * hd] = (
            (y * (1.0 - lambda_init)).T.astype(o_ref.dtype))


def _diff_attention(qk, vt, lam_vecs, subln_w, lambda_init, n_heads, q_col, k_col, v_row):
    bsz, seq, _ = qk.shape
    t = min(ATTN_TILE, seq)
    n_tiles = seq // t
    hd = B_HEAD_DIM
    per = DIFF_HEADS_PER_STEP
    assert n_heads % per == 0 and q_col % (2 * per) == 0 and k_col % (2 * per) == 0
    assert v_row % per == 0
    width = 2 * per * hd
    scratch = []
    for _ in range(2 * per):
        scratch += _flash_scratch(t, 2 * hd)
    return pl.pallas_call(
        functools.partial(_diff_kernel, lambda_init=lambda_init),
        grid=(bsz, n_heads // per, n_tiles),
        in_specs=[pl.BlockSpec((None, t, width), lambda b, h, i: (b, i, q_col // (2 * per) + h)),
                  pl.BlockSpec((None, seq, width), lambda b, h, i: (b, 0, k_col // (2 * per) + h),
                               pipeline_mode=pl.Buffered(1)),
                  pl.BlockSpec((None, n_tiles, width, t),
                               lambda b, h, i: (b, 0, v_row // per + h, 0),
                               pipeline_mode=pl.Buffered(1)),
                  pl.BlockSpec((4, hd), lambda b, h, i: (0, 0)),
                  pl.BlockSpec((2 * hd, 1), lambda b, h, i: (0, 0))],
        out_specs=pl.BlockSpec((None, t, width), lambda b, h, i: (b, i, h)),
        out_shape=jax.ShapeDtypeStruct((bsz, seq, n_heads * 2 * hd), MXU_DTYPE),
        scratch_shapes=scratch,
        compiler_params=_params(("parallel", "parallel", "arbitrary"), 58),
    )(qk, qk, vt, lam_vecs.astype(F32), subln_w.astype(F32).reshape(2 * hd, 1))


def _swa_kernel(q_ref, kc_ref, kp_ref, vtc_ref, vtp_ref, sink_ref, o_ref):
    i = pl.program_id(1)
    w = C_WINDOW
    hd = C_HEAD_DIM
    gw = C_GROUP * hd
    lane = lax.broadcasted_iota(jnp.int32, (w, LANES), 1)
    lo = jnp.where(lane < hd, 1.0, 0.0).astype(q_ref.dtype)
    hi = jnp.where(lane >= hd, 1.0, 0.0).astype(q_ref.dtype)
    krow = lax.broadcasted_iota(jnp.int32, (2 * w, C_GROUP * w), 0)
    qcol = lax.broadcasted_iota(jnp.int32, (2 * w, C_GROUP * w), 1) & (w - 1)
    rel = w + qcol - krow
    valid = (rel >= 0) & (rel < w) & ((i - 1) * w + krow >= 0)
    ones = jnp.ones((ONES_ROWS, 2 * w), vtc_ref.dtype)
    groups = range(q_ref.shape[1] // gw)
    scores = []
    for g in groups:
        k_cols = slice(g * LANES, (g + 1) * LANES)
        kk = jnp.concatenate([kp_ref[:, k_cols], kc_ref[:, k_cols]], axis=0)
        q_heads = []
        for p in range(C_GROUP // 2):
            q2 = q_ref[:, g * gw + p * LANES:g * gw + (p + 1) * LANES]
            q_heads += [q2 * lo, q2 * hi]
        q_all = jnp.concatenate(q_heads, axis=0)
        s = lax.dot_general(kk, q_all, NT_DIMS, preferred_element_type=F32)
        scores.append(jnp.where(valid, s, MASKED))
    sinks = [sink_ref[g] * LOG2E for g in groups]
    maxes = [jnp.maximum(_col_max(s), sink) for s, sink in zip(scores, sinks)]
    probs = [jnp.concatenate([jnp.exp2(s[r:r + SLAB_ROWS] - m).astype(vtc_ref.dtype)
                              for r in range(0, 2 * w, SLAB_ROWS)], axis=0)
             for s, m in zip(scores, maxes)]
    for g in groups:
        v_rows = slice(g * hd, (g + 1) * hd)
        vt = jnp.concatenate([vtp_ref[v_rows, :], vtc_ref[v_rows, :]], axis=1)
        o = jnp.dot(jnp.concatenate([vt, ones], axis=0), probs[g], preferred_element_type=F32)
        o = o[:hd] / (o[hd:hd + 1] + jnp.exp2(sinks[g] - maxes[g]))
        o = jnp.concatenate([o[:, h * w:(h + 1) * w] for h in range(C_GROUP)], axis=0)
        o_ref[:, g * gw:(g + 1) * gw] = o.T.astype(o_ref.dtype)


def _swa_attention(qk, vt, sinks, n_kv):
    bsz, seq, _ = qk.shape
    w = C_WINDOW
    q_width = n_kv * C_GROUP * C_HEAD_DIM
    k_width = n_kv * LANES
    assert q_width % k_width == 0
    per_tile = vt.shape[-1] // w
    sink_cols = jnp.repeat(sinks.astype(F32).reshape(n_kv, 1, C_GROUP), w, axis=-1)
    prev = lambda i: jnp.maximum(i - 1, 0)
    vt_spec = lambda at: pl.BlockSpec(
        (None, None, n_kv * C_HEAD_DIM, w),
        lambda b, i: (b, at(i) // per_tile, 0, at(i) % per_tile))
    return pl.pallas_call(
        _swa_kernel,
        grid=(bsz, seq // w),
        in_specs=[pl.BlockSpec((None, w, q_width), lambda b, i: (b, i, 0)),
                  pl.BlockSpec((None, w, k_width), lambda b, i: (b, i, q_width // k_width)),
                  pl.BlockSpec((None, w, k_width), lambda b, i: (b, prev(i), q_width // k_width)),
                  vt_spec(lambda i: i),
                  vt_spec(prev),
                  pl.BlockSpec((n_kv, 1, C_GROUP * w), lambda b, i: (0, 0, 0))],
        out_specs=pl.BlockSpec((None, w, q_width), lambda b, i: (b, i, 0)),
        out_shape=jax.ShapeDtypeStruct((bsz, seq, q_width), MXU_DTYPE),
        compiler_params=_params(("parallel", "parallel"), 32),
    )(qk, qk, qk, vt, vt, sink_cols)


def _outproj_kernel(*refs, n_parts, has_bias):
    y_refs, w_refs = refs[:n_parts], refs[n_parts:2 * n_parts]
    rest = list(refs[2 * n_parts:])
    b_ref = rest.pop(0) if has_bias else None
    g_ref, x_ref, o_ref = rest
    acc = jnp.dot(y_refs[0][...], w_refs[0][...], preferred_element_type=F32)
    for y_ref, w_ref in zip(y_refs[1:], w_refs[1:]):
        acc = acc + jnp.dot(y_ref[...], w_ref[...], preferred_element_type=F32)
    if has_bias:
        acc = acc + b_ref[...]
    o_ref[...] = x_ref[...] + g_ref[...] * acc


def _out_project(x, gate, ys, ws, bias=None, in_place=True):
    bsz, seq, d = x.shape
    tm = min(ROW_TILE, seq)
    in_specs = [pl.BlockSpec((None, tm, y.shape[-1]), lambda b, i: (b, i, 0)) for y in ys]
    in_specs += [pl.BlockSpec(w.shape, lambda b, i: (0, 0)) for w in ws]
    args = list(ys) + list(ws)
    if bias is not None:
        in_specs.append(pl.BlockSpec((1, d), lambda b, i: (0, 0)))
        args.append(bias.reshape(1, d).astype(F32))
    in_specs += [pl.BlockSpec((None, 1, d), lambda b, i: (b, 0, 0)),
                 pl.BlockSpec((None, tm, d), lambda b, i: (b, i, 0))]
    args += [gate, x]
    return pl.pallas_call(
        functools.partial(_outproj_kernel, n_parts=len(ys), has_bias=bias is not None),
        grid=(bsz, seq // tm),
        in_specs=in_specs,
        out_specs=pl.BlockSpec((None, tm, d), lambda b, i: (b, i, 0)),
        out_shape=jax.ShapeDtypeStruct(x.shape, F32),
        input_output_aliases={len(args) - 1: 0} if in_place else {},
        compiler_params=_params(("parallel", "parallel"), 48),
    )(*args)


def _ffn_kernel(x_ref, sc_ref, sh_ref, g_ref, wg_ref, wu_ref, wo_ref, *rest, final):
    rest = list(rest)
    fw_ref = rest.pop(0) if final else None
    o_ref, h_ref = rest
    k = pl.program_id(2)
    n_h = pl.num_programs(2)
    tm = x_ref.shape[0]
    halves = [slice(r, r + min(ROW_TILE, tm)) for r in range(0, tm, ROW_TILE)]

    def chunk(first, last):
        for rows in halves:
            if first:
                h = _norm_mod(x_ref[rows, :], sc_ref[...], sh_ref[...]).astype(h_ref.dtype)
                h_ref[rows, :] = h
            else:
                h = h_ref[rows, :]
            gt = jnp.dot(h, wg_ref[...], preferred_element_type=F32)
            up = jnp.dot(h, wu_ref[...], preferred_element_type=F32)
            act = ((gt * jax.nn.sigmoid(gt)) * up).astype(h.dtype)
            acc = jnp.dot(act, wo_ref[...], preferred_element_type=F32)
            if not first:
                acc = o_ref[rows, :] + acc
            if last:
                acc = x_ref[rows, :] + g_ref[...] * acc
                if final:
                    ms = jnp.mean(acc * acc, axis=-1, keepdims=True)
                    acc = (acc * lax.rsqrt(ms + RMS_EPS)) * fw_ref[...]
            o_ref[rows, :] = acc

    @pl.when(k == 0)
    def _():
        chunk(True, False)

    @pl.when((k > 0) & (k < n_h - 1))
    def _():
        chunk(False, False)

    @pl.when(k == n_h - 1)
    def _():
        chunk(False, True)


def _ffn(x, sc, sh, gate, w_in, w_out, final_w=None):
    bsz, seq, d = x.shape
    hidden = w_out.shape[0]
    tm = min(FFN_ROW_TILE, seq)
    th = _col_tile(hidden)
    n_h = hidden // th
    assert n_h >= 2
    mod_spec = pl.BlockSpec((None, 1, d), lambda b, i, k: (b, 0, 0))
    in_specs = [pl.BlockSpec((None, tm, d), lambda b, i, k: (b, i, 0)),
                mod_spec, mod_spec, mod_spec,
                pl.BlockSpec((d, th), lambda b, i, k: (0, k)),
                pl.BlockSpec((d, th), lambda b, i, k: (0, n_h + k)),
                pl.BlockSpec((th, d), lambda b, i, k: (k, 0))]
    args = [x, sc, sh, gate, w_in, w_in, w_out]
    if final_w is not None:
        in_specs.append(pl.BlockSpec((1, d), lambda b, i, k: (0, 0)))
        args.append(final_w.reshape(1, d).astype(F32))
    return pl.pallas_call(
        functools.partial(_ffn_kernel, final=final_w is not None),
        grid=(bsz, seq // tm, n_h),
        in_specs=in_specs,
        out_specs=pl.BlockSpec((None, tm, d), lambda b, i, k: (b, i, 0)),
        out_shape=jax.ShapeDtypeStruct(x.shape, F32),
        scratch_shapes=[pltpu.VMEM((tm, d), MXU_DTYPE)],
        input_output_aliases={0: 0},
        compiler_params=_params(("parallel", "parallel", "arbitrary"), 60),
    )(*args)


def _rope_tables(seq, dim, query_scale):
    pos = jnp.arange(seq, dtype=F32)
    inv_freq = ROPE_THETA ** (-jnp.arange(0, dim, 2, dtype=F32) / dim)
    ang = pos[:, None] * inv_freq[None, :]
    cos, sin = jnp.cos(ang), jnp.sin(ang)
    reps = LANES // dim
    cos = jnp.tile(jnp.concatenate([cos, cos], axis=1), (1, reps))
    sin = jnp.tile(jnp.concatenate([-sin, sin], axis=1), (1, reps))
    return jnp.stack([cos * query_scale, cos]), jnp.stack([sin * query_scale, sin])


def _lambda_init(layer):
    return 0.8 - 0.6 * math.exp(-0.3 * layer)


def _dup_heads(w, n_heads, head_dim):
    lead = w.shape[:-1]
    w = w.reshape(lead + (n_heads, 1, head_dim))
    return jnp.broadcast_to(w, lead + (n_heads, 2, head_dim)).reshape(lead + (2 * n_heads * head_dim,))


def kernel(x, c, ada_w, ada_b, ab_w_in, ab_w_out, diff_lambda, diff_subln, swa_w_in, swa_b_in,
           swa_w_out, swa_b_out, swa_sinks, ffn_w_in, ffn_w_out, final_norm):
    bsz, seq, d = x.shape
    depth = ada_w.shape[0]
    a_heads, b_heads = d // 256, d // 512
    a_width, b_width = a_heads * A_HEAD_DIM, 2 * b_heads * B_HEAD_DIM
    assert a_width == b_width
    c_q_heads = d // C_HEAD_DIM
    c_kv_heads = c_q_heads // C_GROUP
    c_q_width, c_kv_width = c_q_heads * C_HEAD_DIM, c_kv_heads * C_HEAD_DIM

    rope_ab = _rope_tables(seq, A_HEAD_DIM, A_HEAD_DIM ** -0.5 * LOG2E)
    rope_c = _rope_tables(seq, C_HEAD_DIM, C_HEAD_DIM ** -0.5 * LOG2E)
    mod = _modulation(c, ada_w, ada_b)

    for layer in range(depth):
        sh1, sc1, g1, sh2, sc2, g2 = [
            mod[layer, :, None, m * d:(m + 1) * d] for m in range(N_MOD)]
        li = layer // 2
        if layer % 2 == 0:
            w_in = ab_w_in[li].astype(MXU_DTYPE)
            aq, ak, av, bq, bk, bv = jnp.split(
                w_in, [a_width, 2 * a_width, 3 * a_width,
                       3 * a_width + b_width, 3 * a_width + 2 * b_width], axis=1)
            qk = _project(x, sc1, sh1, jnp.concatenate([aq, ak, bq, bk], axis=1),
                          rope=rope_ab, rope_dim=A_HEAD_DIM,
                          is_query_col=lambda col: (col // a_width) % 2 == 0)
            vt = _project(x, sc1, sh1, jnp.concatenate([av, bv], axis=1), transpose_out=True)
            ya = _moba_attention(qk, vt, a_heads, q_col=0, k_col=a_heads, v_row=0)
            yb = _diff_attention(qk, vt, diff_lambda[li], diff_subln[li], _lambda_init(layer),
                                 b_heads, q_col=2 * a_heads, k_col=2 * a_heads + 2 * b_heads,
                                 v_row=a_width // (2 * B_HEAD_DIM))
            w_out = ab_w_out[li].astype(MXU_DTYPE)
            x = _out_project(x, g1, [ya, yb], [w_out[:a_width], w_out[a_width:]],
                             in_place=layer > 0)
        else:
            w_in, b_in = swa_w_in[li], swa_b_in[li]
            wq, wk, wv = jnp.split(w_in, [c_q_width, c_q_width + c_kv_width], axis=1)
            bq_, bk_, bv_ = jnp.split(b_in, [c_q_width, c_q_width + c_kv_width])
            w_qk = jnp.concatenate([wq, _dup_heads(wk, c_kv_heads, C_HEAD_DIM)], axis=1)
            b_qk = jnp.concatenate([bq_, _dup_heads(bk_, c_kv_heads, C_HEAD_DIM)])
            qk = _project(x, sc1, sh1, w_qk.astype(MXU_DTYPE), bias=b_qk,
                          rope=rope_c, rope_dim=C_HEAD_DIM,
                          is_query_col=lambda col: col < c_q_width)
            vt = _project(x, sc1, sh1, wv.astype(MXU_DTYPE), bias=bv_, transpose_out=True)
            y = _swa_attention(qk, vt, swa_sinks[li], c_kv_heads)
            x = _out_project(x, g1, [y], [swa_w_out[li].astype(MXU_DTYPE)], bias=swa_b_out[li])
        x = _ffn(x, sc2, sh2, g2, ffn_w_in[layer].astype(MXU_DTYPE),
                 ffn_w_out[layer].astype(MXU_DTYPE),
                 final_w=final_norm if layer == depth - 1 else None)
    return x
```

```python
import functools
import math

import jax
import jax.numpy as jnp
from jax import lax
from jax.experimental import pallas as pl
from jax.experimental.pallas import tpu as pltpu

F32 = jnp.float32
MXU_DTYPE = jnp.bfloat16

ROPE_THETA = 10000.0
RMS_EPS = 1e-6
DIFF_SUBLN_EPS = 1e-5
A_HEAD_DIM = 128
MOBA_BLOCK = 256
MOBA_BLOCK_LOG2 = 8
MOBA_TOPK = 3
B_HEAD_DIM = 128
C_HEAD_DIM = 64
C_GROUP = 8
C_WINDOW = 128
N_MOD = 6

LANES = 128
MASKED = -1e30
LOG2E = 1.4426950408889634

ATTN_TILE = 512
ROW_TILE = 512
FFN_ROW_TILE = 1024
MOD_COL_TILE = 1024
MIB = 2 ** 20
VMEM_LIMIT_MIB = dict(modulation=40, projection=48, moba=56, diff=58, swa=32,
                      out_projection=48, ffn=60)

NT_DIMS = (((1,), (1,)), ((), ()))


def _params(semantics, call):
    return pltpu.CompilerParams(dimension_semantics=semantics,
                                vmem_limit_bytes=VMEM_LIMIT_MIB[call] * MIB)


def _col_tile(n):
    for t in (512, 256, 128):
        if n % t == 0:
            return t
    raise ValueError(f"column count {n} is not a multiple of {LANES}")


def _column_tiles(w, tn):
    k, n = w.shape
    return w.reshape(k, n // tn, tn).transpose(1, 0, 2)


def _norm_mod(x, sc, sh):
    ms = jnp.mean(x * x, axis=-1, keepdims=True)
    return (x * lax.rsqrt(ms + RMS_EPS)) * (1.0 + sc) + sh


def _mod_kernel(c_ref, w_ref, b_ref, o_ref):
    c = c_ref[...]
    cond = (c * jax.nn.sigmoid(c)).astype(MXU_DTYPE)
    o_ref[...] = jnp.dot(cond, w_ref[...].astype(MXU_DTYPE),
                         preferred_element_type=F32) + b_ref[...]


def _modulation(c, ada_w, ada_b):
    depth, d, n = ada_w.shape
    bsz = c.shape[0]
    rows = -(-bsz // 8) * 8
    c_pad = jnp.pad(c, ((0, rows - bsz), (0, 0)))
    tn = MOD_COL_TILE if n % MOD_COL_TILE == 0 else _col_tile(n)
    out = pl.pallas_call(
        _mod_kernel,
        grid=(depth, n // tn),
        in_specs=[pl.BlockSpec((rows, d), lambda l, j: (0, 0)),
                  pl.BlockSpec((None, d, tn), lambda l, j: (l, 0, j)),
                  pl.BlockSpec((None, 1, tn), lambda l, j: (l, 0, j))],
        out_specs=pl.BlockSpec((None, rows, tn), lambda l, j: (l, 0, j)),
        out_shape=jax.ShapeDtypeStruct((depth, rows, n), F32),
        compiler_params=_params(("parallel", "parallel"), "modulation"),
    )(c_pad, ada_w, ada_b.reshape(depth, 1, n))
    return out[:, :bsz]


def _rope_partner(a, rope_dim):
    if rope_dim == LANES:
        return pltpu.roll(a, LANES // 2, 1)
    lane = lax.broadcasted_iota(jnp.int32, a.shape, 1)
    half = rope_dim // 2
    first_half = (lane & (rope_dim - 1)) < half
    return jnp.where(first_half, pltpu.roll(a, LANES - half, 1), pltpu.roll(a, half, 1))


def _proj_kernel(x_ref, sc_ref, sh_ref, w_ref, *rest, rope_dim, has_bias, transpose_out,
                 query_tiles):
    rest = list(rest)
    b_ref = rest.pop(0) if has_bias else None
    cos_ref, sin_ref = (rest.pop(0), rest.pop(0)) if rope_dim else (None, None)
    (o_ref,) = rest
    n_t, _, tn = w_ref.shape
    h = _norm_mod(x_ref[...], sc_ref[...], sh_ref[...]).astype(w_ref.dtype)
    for j in range(n_t):
        cols = slice(j * tn, (j + 1) * tn)
        acc = jnp.dot(h, w_ref[j], preferred_element_type=F32)
        if has_bias:
            acc = acc + b_ref[:, cols]
        if rope_dim:
            table = 0 if query_tiles[j] else 1
            cos, sin = cos_ref[table], sin_ref[table]
            for t in range(tn // LANES):
                a = acc[:, t * LANES:(t + 1) * LANES]
                o_ref[:, j * tn + t * LANES:j * tn + (t + 1) * LANES] = (
                    a * cos + _rope_partner(a, rope_dim) * sin).astype(o_ref.dtype)
        elif transpose_out:
            o_ref[cols, :] = acc.T.astype(o_ref.dtype)
        else:
            o_ref[:, cols] = acc.astype(o_ref.dtype)


def _project(x, sc, sh, w, bias=None, rope=None, rope_dim=0, is_query_col=None,
             transpose_out=False):
    bsz, seq, d = x.shape
    n = w.shape[1]
    tm = min(ROW_TILE, seq)
    tn = _col_tile(n)
    n_t = n // tn
    in_specs = [pl.BlockSpec((None, tm, d), lambda b, i: (b, i, 0)),
                pl.BlockSpec((None, 1, d), lambda b, i: (b, 0, 0)),
                pl.BlockSpec((None, 1, d), lambda b, i: (b, 0, 0)),
                pl.BlockSpec((n_t, d, tn), lambda b, i: (0, 0, 0), pipeline_mode=pl.Buffered(1))]
    args = [x, sc, sh, _column_tiles(w, tn)]
    if bias is not None:
        in_specs.append(pl.BlockSpec((1, n), lambda b, i: (0, 0)))
        args.append(bias.reshape(1, n).astype(F32))
    query_tiles = None
    if rope_dim:
        query_tiles = tuple(bool(is_query_col(j * tn)) for j in range(n_t))
        in_specs += [pl.BlockSpec((2, tm, LANES), lambda b, i: (0, i, 0))] * 2
        args += list(rope)
    if transpose_out:
        out_shape = jax.ShapeDtypeStruct((bsz, seq // tm, n, tm), MXU_DTYPE)
        out_spec = pl.BlockSpec((None, None, n, tm), lambda b, i: (b, i, 0, 0))
    else:
        out_shape = jax.ShapeDtypeStruct((bsz, seq, n), MXU_DTYPE)
        out_spec = pl.BlockSpec((None, tm, n), lambda b, i: (b, i, 0))
    return pl.pallas_call(
        functools.partial(_proj_kernel, rope_dim=rope_dim, has_bias=bias is not None,
                          transpose_out=transpose_out, query_tiles=query_tiles),
        grid=(bsz, seq // tm),
        in_specs=in_specs,
        out_specs=out_spec,
        out_shape=out_shape,
        compiler_params=_params(("parallel", "parallel"), "projection"),
    )(*args)


N_FLASH_REFS = 10
SLAB_ROWS = 64
ONES_ROWS = 16
FLASH_GROUP = 2


def _flash_scratch(t, v_dim):
    stat = pltpu.VMEM((1, t), F32)
    scores = pltpu.VMEM((t, t), F32)
    probs = pltpu.VMEM((t, t), MXU_DTYPE)
    return [stat, pltpu.VMEM((v_dim + ONES_ROWS, t), F32),
            scores, scores, stat, stat, probs, probs, stat, stat]


class _FlashStream:
    def __init__(self, refs):
        self.m, self.acc = refs[:2]
        self.s, self.s_max, self.p, self.alpha = refs[2:4], refs[4:6], refs[6:8], refs[8:10]

    def result(self):
        v_dim = self.acc.shape[0] - ONES_ROWS
        return self.acc[:v_dim, :] / self.acc[v_dim:v_dim + 1, :]


def _col_max(x):
    slab = x[:SLAB_ROWS]
    for r in range(SLAB_ROWS, x.shape[0], SLAB_ROWS):
        slab = jnp.maximum(slab, x[r:r + SLAB_ROWS])
    return jnp.max(slab, axis=0, keepdims=True)


def _causal_flash(i, streams, scores_of, values_of):
    for st in streams:
        st.m[...] = jnp.full_like(st.m, MASKED)
        st.acc[...] = jnp.zeros_like(st.acc)
        st.alpha[1][...] = jnp.ones_like(st.alpha[1])
        st.p[1][...] = jnp.zeros_like(st.p[1])

    def put_scores(st, slot, s):
        st.s[slot][...] = s
        st.s_max[slot][...] = _col_max(s)

    for idx, st in enumerate(streams):
        s = scores_of(idx, i)
        krow = lax.broadcasted_iota(jnp.int32, s.shape, 0)
        qcol = lax.broadcasted_iota(jnp.int32, s.shape, 1)
        put_scores(st, 0, jnp.where(krow <= qcol, s, MASKED))
    n_steps = i + 1

    def values_step(k, slot):
        tile = jnp.where(k <= 0, i, k - 1)
        for idx, st in enumerate(streams):
            vt = values_of(idx, tile)
            vt = jnp.concatenate([vt, jnp.ones((ONES_ROWS, vt.shape[1]), vt.dtype)], axis=0)
            st.acc[...] = st.alpha[slot][...] * st.acc[...] + jnp.dot(
                vt, st.p[slot][...], preferred_element_type=F32)

    def step(k, cur, is_last):
        nxt = 1 - cur
        ahead = k
        tile = jnp.where(k <= 1, i, k - 2)
        for g0 in range(0, len(streams), FLASH_GROUP):
            group = list(enumerate(streams))[g0:g0 + FLASH_GROUP]
            for idx, st in group:
                m_prev = st.m[...]
                m_new = jnp.maximum(m_prev, st.s_max[cur][...])
                for r in range(0, st.s[cur].shape[0], SLAB_ROWS):
                    p = jnp.exp2(st.s[cur][r:r + SLAB_ROWS, :] - m_new)
                    st.p[cur][r:r + SLAB_ROWS, :] = p.astype(st.p[cur].dtype)
                st.m[...] = m_new
                st.alpha[cur][...] = jnp.exp2(m_prev - m_new)
            if not is_last:
                for idx, st in group:
                    put_scores(st, nxt, scores_of(idx, ahead))
            for idx, st in group:
                vt = values_of(idx, tile)
                vt = jnp.concatenate([vt, jnp.ones((ONES_ROWS, vt.shape[1]), vt.dtype)], axis=0)
                st.acc[...] = st.alpha[nxt][...] * st.acc[...] + jnp.dot(
                    vt, st.p[nxt][...], preferred_element_type=F32)

    def pair_body(kk, carry):
        step(2 * kk, 0, False)
        step(2 * kk + 1, 1, False)
        return carry

    n_pairs = (n_steps - 1) // 2
    lax.fori_loop(0, n_pairs, pair_body, 0)
    done = 2 * n_pairs

    @pl.when(n_steps - done == 1)
    def _():
        step(done, 0, True)
        values_step(done, 0)

    @pl.when(n_steps - done == 2)
    def _():
        step(done, 0, False)
        step(done + 1, 1, True)
        values_step(done + 1, 1)


MOBA_HEADS_PER_STEP = 4


def _moba_kernel(q_ref, k_ref, vt_ref, o_ref, kmean_ref, qaug_ref, *state, n_blocks):
    i = pl.program_id(2)
    tq = q_ref.shape[0]
    tk = tq
    hd = A_HEAD_DIM
    heads = [slice(e * hd, (e + 1) * hd) for e in range(MOBA_HEADS_PER_STEP)]

    @pl.when(i == 0)
    def _():
        kmean_ref[...] = jnp.zeros_like(kmean_ref)

        def mean_body(n, carry):
            kb = k_ref[pl.ds(pl.multiple_of(n * MOBA_BLOCK, MOBA_BLOCK), MOBA_BLOCK), :]
            mean = jnp.sum(kb.astype(F32), axis=0, keepdims=True) * (1.0 / MOBA_BLOCK)
            for e, cols in enumerate(heads):
                kmean_ref[e, pl.ds(n, 1), :] = mean[:, cols]
            return carry

        lax.fori_loop(0, n_blocks, mean_body, 0)

    rows = min(LANES, -(-n_blocks // 16) * 16)
    blk = lax.broadcasted_iota(jnp.int32, (rows, tq), 0)
    qpos = lax.broadcasted_iota(jnp.int32, (rows, tq), 1)
    own = jnp.right_shift(i * tq + qpos, MOBA_BLOCK_LOG2)
    past = blk < own
    blkf = blk.astype(F32)
    for e, cols in enumerate(heads):
        q_t = q_ref[:, cols].astype(F32).T.astype(MXU_DTYPE)
        km = kmean_ref[e, :rows, :]
        k_hi = km.astype(MXU_DTYPE)
        r1 = km - k_hi.astype(F32)
        k_mid = r1.astype(MXU_DTYPE)
        k_lo = (r1 - k_mid.astype(F32)).astype(MXU_DTYPE)
        gate = (jnp.dot(k_lo, q_t, preferred_element_type=F32)
                + jnp.dot(k_mid, q_t, preferred_element_type=F32)
                + jnp.dot(k_hi, q_t, preferred_element_type=F32))
        g = jnp.where(past, gate, -jnp.inf)
        chosen = jnp.zeros(gate.shape, jnp.bool_)
        for _ in range(min(MOBA_TOPK, n_blocks)):
            best = jnp.max(g, axis=0, keepdims=True)
            first_best = jnp.min(jnp.where(g == best, blkf, float(LANES)), axis=0, keepdims=True)
            pick = blkf == first_best
            chosen = jnp.logical_or(chosen, pick)
            g = jnp.where(pick, -jnp.inf, g)
        visible = jnp.logical_or(jnp.logical_and(chosen, past), blk == own)
        qaug_ref[e, :hd, :] = q_t
        qaug_ref[e, hd:hd + rows, :] = jnp.where(visible, 0.0, MASKED).astype(qaug_ref.dtype)
        if rows < LANES:
            qaug_ref[e, hd + rows:, :] = jnp.zeros((LANES - rows, tq), qaug_ref.dtype)

    def scores_of(e, j):
        kt = k_ref[pl.ds(pl.multiple_of(j * tk, tk), tk), heads[e]]
        lane = lax.broadcasted_iota(jnp.int32, (1, LANES), 1)
        blocks_per_tile = tk // MOBA_BLOCK
        onehot = jnp.concatenate(
            [jnp.broadcast_to(jnp.where(lane == j * blocks_per_tile + b, 1.0, 0.0),
                              (MOBA_BLOCK, LANES)) for b in range(blocks_per_tile)], axis=0)
        k_aug = jnp.concatenate([kt, onehot.astype(kt.dtype)], axis=1)
        return jnp.dot(k_aug, qaug_ref[e], preferred_element_type=F32)

    streams = [_FlashStream(state[e * N_FLASH_REFS:(e + 1) * N_FLASH_REFS])
               for e in range(MOBA_HEADS_PER_STEP)]
    _causal_flash(i, streams, scores_of, lambda e, j: vt_ref[j, heads[e], :])
    for e, st in enumerate(streams):
        o_ref[:, heads[e]] = st.result().T.astype(o_ref.dtype)


def _moba_attention(qk, vt, n_heads, q_col, k_col, v_row):
    bsz, seq, _ = qk.shape
    t = min(ATTN_TILE, seq)
    n_tiles = seq // t
    n_blocks = seq // MOBA_BLOCK
    per = MOBA_HEADS_PER_STEP
    assert seq % t == 0 and t % MOBA_BLOCK == 0 and n_blocks <= LANES
    assert n_heads % per == 0 and q_col % per == 0 and k_col % per == 0 and v_row % per == 0
    hd = A_HEAD_DIM
    scratch = [pltpu.VMEM((per, LANES, hd), F32), pltpu.VMEM((per, 2 * hd, t), MXU_DTYPE)]
    for _ in range(per):
        scratch += _flash_scratch(t, hd)
    return pl.pallas_call(
        functools.partial(_moba_kernel, n_blocks=n_blocks),
        grid=(bsz, n_heads // per, n_tiles),
        in_specs=[pl.BlockSpec((None, t, per * hd), lambda b, h, i: (b, i, q_col // per + h)),
                  pl.BlockSpec((None, seq, per * hd), lambda b, h, i: (b, 0, k_col // per + h),
                               pipeline_mode=pl.Buffered(1)),
                  pl.BlockSpec((None, n_tiles, per * hd, t),
                               lambda b, h, i: (b, 0, v_row // per + h, 0),
                               pipeline_mode=pl.Buffered(1))],
        out_specs=pl.BlockSpec((None, t, per * hd), lambda b, h, i: (b, i, h)),
        out_shape=jax.ShapeDtypeStruct((bsz, seq, n_heads * hd), MXU_DTYPE),
        scratch_shapes=scratch,
        compiler_params=_params(("parallel", "parallel", "arbitrary"), "moba"),
    )(qk, qk, vt)


DIFF_HEADS_PER_STEP = 2


def _diff_kernel(q_ref, k_ref, vt_ref, lam_ref, w_ref, o_ref, *state, lambda_init):
    i = pl.program_id(2)
    tk = q_ref.shape[0]
    hd = B_HEAD_DIM
    n_streams = 2 * DIFF_HEADS_PER_STEP
    streams = [_FlashStream(state[s * N_FLASH_REFS:(s + 1) * N_FLASH_REFS])
               for s in range(n_streams)]

    def scores_of(s, j):
        cols = slice(s * hd, (s + 1) * hd)
        rows = pl.ds(pl.multiple_of(j * tk, tk), tk)
        return lax.dot_general(k_ref[rows, cols], q_ref[:, cols], NT_DIMS,
                               preferred_element_type=F32)

    def values_of(s, j):
        head = s // 2
        return vt_ref[j, head * 2 * hd:(head + 1) * 2 * hd, :]

    _causal_flash(i, streams, scores_of, values_of)

    lv = lam_ref[...]
    lam = (jnp.exp(jnp.sum(lv[0:1] * lv[1:2], axis=1, keepdims=True))
           - jnp.exp(jnp.sum(lv[2:3] * lv[3:4], axis=1, keepdims=True)) + lambda_init)
    for head in range(DIFF_HEADS_PER_STEP):
        y = streams[2 * head].result() - lam * streams[2 * head + 1].result()
        ms = jnp.mean(y * y, axis=0, keepdims=True)
        y = (y * lax.rsqrt(ms + DIFF_SUBLN_EPS)) * w_ref[...]
        o_ref[:, head * 2 * hd:(head + 1) * 2 * hd] = (
            (y * (1.0 - lambda_init)).T.astype(o_ref.dtype))


def _diff_attention(qk, vt, lam_vecs, subln_w, lambda_init, n_heads, q_col, k_col, v_row):
    bsz, seq, _ = qk.shape
    t = min(ATTN_TILE, seq)
    n_tiles = seq // t
    hd = B_HEAD_DIM
    per = DIFF_HEADS_PER_STEP
    assert n_heads % per == 0 and q_col % (2 * per) == 0 and k_col % (2 * per) == 0
    assert v_row % per == 0
    width = 2 * per * hd
    scratch = []
    for _ in range(2 * per):
        scratch += _flash_scratch(t, 2 * hd)
    return pl.pallas_call(
        functools.partial(_diff_kernel, lambda_init=lambda_init),
        grid=(bsz, n_heads // per, n_tiles),
        in_specs=[pl.BlockSpec((None, t, width), lambda b, h, i: (b, i, q_col // (2 * per) + h)),
                  pl.BlockSpec((None, seq, width), lambda b, h, i: (b, 0, k_col // (2 * per) + h),
                               pipeline_mode=pl.Buffered(1)),
                  pl.BlockSpec((None, n_tiles, width, t),
                               lambda b, h, i: (b, 0, v_row // per + h, 0),
                               pipeline_mode=pl.Buffered(1)),
                  pl.BlockSpec((4, hd), lambda b, h, i: (0, 0)),
                  pl.BlockSpec((2 * hd, 1), lambda b, h, i: (0, 0))],
        out_specs=pl.BlockSpec((None, t, width), lambda b, h, i: (b, i, h)),
        out_shape=jax.ShapeDtypeStruct((bsz, seq, n_heads * 2 * hd), MXU_DTYPE),
        scratch_shapes=scratch,
        compiler_params=_params(("parallel", "parallel", "arbitrary"), "diff"),
    )(qk, qk, vt, lam_vecs.astype(F32), subln_w.astype(F32).reshape(2 * hd, 1))


def _swa_kernel(q_ref, kc_ref, kp_ref, vtc_ref, vtp_ref, sink_ref, o_ref):
    i = pl.program_id(1)
    w = C_WINDOW
    hd = C_HEAD_DIM
    gw = C_GROUP * hd
    lane = lax.broadcasted_iota(jnp.int32, (w, LANES), 1)
    lo = jnp.where(lane < hd, 1.0, 0.0).astype(q_ref.dtype)
    hi = jnp.where(lane >= hd, 1.0, 0.0).astype(q_ref.dtype)
    krow = lax.broadcasted_iota(jnp.int32, (2 * w, C_GROUP * w), 0)
    qcol = lax.broadcasted_iota(jnp.int32, (2 * w, C_GROUP * w), 1) & (w - 1)
    rel = w + qcol - krow
    valid = (rel >= 0) & (rel < w) & ((i - 1) * w + krow >= 0)
    ones = jnp.ones((ONES_ROWS, 2 * w), vtc_ref.dtype)
    groups = range(q_ref.shape[1] // gw)
    scores = []
    for g in groups:
        k_cols = slice(g * LANES, (g + 1) * LANES)
        kk = jnp.concatenate([kp_ref[:, k_cols], kc_ref[:, k_cols]], axis=0)
        q_heads = []
        for p in range(C_GROUP // 2):
            q2 = q_ref[:, g * gw + p * LANES:g * gw + (p + 1) * LANES]
            q_heads += [q2 * lo, q2 * hi]
        q_all = jnp.concatenate(q_heads, axis=0)
        s = lax.dot_general(kk, q_all, NT_DIMS, preferred_element_type=F32)
        scores.append(jnp.where(valid, s, MASKED))
    sinks = [sink_ref[g] * LOG2E for g in groups]
    maxes = [jnp.maximum(_col_max(s), sink) for s, sink in zip(scores, sinks)]
    probs = [jnp.concatenate([jnp.exp2(s[r:r + SLAB_ROWS] - m).astype(vtc_ref.dtype)
                              for r in range(0, 2 * w, SLAB_ROWS)], axis=0)
             for s, m in zip(scores, maxes)]
    for g in groups:
        v_rows = slice(g * hd, (g + 1) * hd)
        vt = jnp.concatenate([vtp_ref[v_rows, :], vtc_ref[v_rows, :]], axis=1)
        o = jnp.dot(jnp.concatenate([vt, ones], axis=0), probs[g], preferred_element_type=F32)
        o = o[:hd] / (o[hd:hd + 1] + jnp.exp2(sinks[g] - maxes[g]))
        o = jnp.concatenate([o[:, h * w:(h + 1) * w] for h in range(C_GROUP)], axis=0)
        o_ref[:, g * gw:(g + 1) * gw] = o.T.astype(o_ref.dtype)


def _swa_attention(qk, vt, sinks, n_kv):
    bsz, seq, _ = qk.shape
    w = C_WINDOW
    q_width = n_kv * C_GROUP * C_HEAD_DIM
    k_width = n_kv * LANES
    assert q_width % k_width == 0
    per_tile = vt.shape[-1] // w
    sink_cols = jnp.repeat(sinks.astype(F32).reshape(n_kv, 1, C_GROUP), w, axis=-1)
    prev = lambda i: jnp.maximum(i - 1, 0)
    vt_spec = lambda at: pl.BlockSpec(
        (None, None, n_kv * C_HEAD_DIM, w),
        lambda b, i: (b, at(i) // per_tile, 0, at(i) % per_tile))
    return pl.pallas_call(
        _swa_kernel,
        grid=(bsz, seq // w),
        in_specs=[pl.BlockSpec((None, w, q_width), lambda b, i: (b, i, 0)),
                  pl.BlockSpec((None, w, k_width), lambda b, i: (b, i, q_width // k_width)),
                  pl.BlockSpec((None, w, k_width), lambda b, i: (b, prev(i), q_width // k_width)),
                  vt_spec(lambda i: i),
                  vt_spec(prev),
                  pl.BlockSpec((n_kv, 1, C_GROUP * w), lambda b, i: (0, 0, 0))],
        out_specs=pl.BlockSpec((None, w, q_width), lambda b, i: (b, i, 0)),
        out_shape=jax.ShapeDtypeStruct((bsz, seq, q_width), MXU_DTYPE),
        compiler_params=_params(("parallel", "parallel"), "swa"),
    )(qk, qk, qk, vt, vt, sink_cols)


def _outproj_kernel(*refs, n_parts, has_bias):
    y_refs, w_refs = refs[:n_parts], refs[n_parts:2 * n_parts]
    rest = list(refs[2 * n_parts:])
    b_ref = rest.pop(0) if has_bias else None
    g_ref, x_ref, o_ref = rest
    acc = jnp.dot(y_refs[0][...], w_refs[0][...], preferred_element_type=F32)
    for y_ref, w_ref in zip(y_refs[1:], w_refs[1:]):
        acc = acc + jnp.dot(y_ref[...], w_ref[...], preferred_element_type=F32)
    if has_bias:
        acc = acc + b_ref[...]
    o_ref[...] = x_ref[...] + g_ref[...] * acc


def _out_project(x, gate, ys, ws, bias=None, in_place=True):
    bsz, seq, d = x.shape
    tm = min(ROW_TILE, seq)
    in_specs = [pl.BlockSpec((None, tm, y.shape[-1]), lambda b, i: (b, i, 0)) for y in ys]
    in_specs += [pl.BlockSpec(w.shape, lambda b, i: (0, 0)) for w in ws]
    args = list(ys) + list(ws)
    if bias is not None:
        in_specs.append(pl.BlockSpec((1, d), lambda b, i: (0, 0)))
        args.append(bias.reshape(1, d).astype(F32))
    in_specs += [pl.BlockSpec((None, 1, d), lambda b, i: (b, 0, 0)),
                 pl.BlockSpec((None, tm, d), lambda b, i: (b, i, 0))]
    args += [gate, x]
    return pl.pallas_call(
        functools.partial(_outproj_kernel, n_parts=len(ys), has_bias=bias is not None),
        grid=(bsz, seq // tm),
        in_specs=in_specs,
        out_specs=pl.BlockSpec((None, tm, d), lambda b, i: (b, i, 0)),
        out_shape=jax.ShapeDtypeStruct(x.shape, F32),
        input_output_aliases={len(args) - 1: 0} if in_place else {},
        compiler_params=_params(("parallel", "parallel"), "out_projection"),
    )(*args)


def _ffn_kernel(x_ref, sc_ref, sh_ref, g_ref, wg_ref, wu_ref, wo_ref, *rest, final):
    rest = list(rest)
    fw_ref = rest.pop(0) if final else None
    o_ref, h_ref = rest
    k = pl.program_id(2)
    n_h = pl.num_programs(2)
    tm = x_ref.shape[0]
    halves = [slice(r, r + min(ROW_TILE, tm)) for r in range(0, tm, ROW_TILE)]

    def chunk(first, last):
        for rows in halves:
            if first:
                h = _norm_mod(x_ref[rows, :], sc_ref[...], sh_ref[...]).astype(h_ref.dtype)
                h_ref[rows, :] = h
            else:
                h = h_ref[rows, :]
            gt = jnp.dot(h, wg_ref[...], preferred_element_type=F32)
            up = jnp.dot(h, wu_ref[...], preferred_element_type=F32)
            act = ((gt * jax.nn.sigmoid(gt)) * up).astype(h.dtype)
            acc = jnp.dot(act, wo_ref[...], preferred_element_type=F32)
            if not first:
                acc = o_ref[rows, :] + acc
            if last:
                acc = x_ref[rows, :] + g_ref[...] * acc
                if final:
                    ms = jnp.mean(acc * acc, axis=-1, keepdims=True)
                    acc = (acc * lax.rsqrt(ms + RMS_EPS)) * fw_ref[...]
            o_ref[rows, :] = acc

    @pl.when(k == 0)
    def _():
        chunk(True, False)

    @pl.when((k > 0) & (k < n_h - 1))
    def _():
        chunk(False, False)

    @pl.when(k == n_h - 1)
    def _():
        chunk(False, True)


def _ffn(x, sc, sh, gate, w_in, w_out, final_w=None):
    bsz, seq, d = x.shape
    hidden = w_out.shape[0]
    tm = min(FFN_ROW_TILE, seq)
    th = _col_tile(hidden)
    n_h = hidden // th
    assert n_h >= 2
    mod_spec = pl.BlockSpec((None, 1, d), lambda b, i, k: (b, 0, 0))
    in_specs = [pl.BlockSpec((None, tm, d), lambda b, i, k: (b, i, 0)),
                mod_spec, mod_spec, mod_spec,
                pl.BlockSpec((d, th), lambda b, i, k: (0, k)),
                pl.BlockSpec((d, th), lambda b, i, k: (0, n_h + k)),
                pl.BlockSpec((th, d), lambda b, i, k: (k, 0))]
    args = [x, sc, sh, gate, w_in, w_in, w_out]
    if final_w is not None:
        in_specs.append(pl.BlockSpec((1, d), lambda b, i, k: (0, 0)))
        args.append(final_w.reshape(1, d).astype(F32))
    return pl.pallas_call(
        functools.partial(_ffn_kernel, final=final_w is not None),
        grid=(bsz, seq // tm, n_h),
        in_specs=in_specs,
        out_specs=pl.BlockSpec((None, tm, d), lambda b, i, k: (b, i, 0)),
        out_shape=jax.ShapeDtypeStruct(x.shape, F32),
        scratch_shapes=[pltpu.VMEM((tm, d), MXU_DTYPE)],
        input_output_aliases={0: 0},
        compiler_params=_params(("parallel", "parallel", "arbitrary"), "ffn"),
    )(*args)


def _rope_tables(seq, dim, query_scale):
    pos = jnp.arange(seq, dtype=F32)
    inv_freq = ROPE_THETA ** (-jnp.arange(0, dim, 2, dtype=F32) / dim)
    ang = pos[:, None] * inv_freq[None, :]
    cos, sin = jnp.cos(ang), jnp.sin(ang)
    reps = LANES // dim
    cos = jnp.tile(jnp.concatenate([cos, cos], axis=1), (1, reps))
    sin = jnp.tile(jnp.concatenate([-sin, sin], axis=1), (1, reps))
    return jnp.stack([cos * query_scale, cos]), jnp.stack([sin * query_scale, sin])


def _lambda_init(layer):
    return 0.8 - 0.6 * math.exp(-0.3 * layer)


def _dup_heads(w, n_heads, head_dim):
    lead = w.shape[:-1]
    w = w.reshape(lead + (n_heads, 1, head_dim))
    return jnp.broadcast_to(w, lead + (n_heads, 2, head_dim)).reshape(lead + (2 * n_heads * head_dim,))


def kernel(x, c, ada_w, ada_b, ab_w_in, ab_w_out, diff_lambda, diff_subln, swa_w_in, swa_b_in,
           swa_w_out, swa_b_out, swa_sinks, ffn_w_in, ffn_w_out, final_norm):
    bsz, seq, d = x.shape
    depth = ada_w.shape[0]
    a_heads, b_heads = d // 256, d // 512
    a_width, b_width = a_heads * A_HEAD_DIM, 2 * b_heads * B_HEAD_DIM
    assert a_width == b_width
    c_q_heads = d // C_HEAD_DIM
    c_kv_heads = c_q_heads // C_GROUP
    c_q_width, c_kv_width = c_q_heads * C_HEAD_DIM, c_kv_heads * C_HEAD_DIM

    rope_ab = _rope_tables(seq, A_HEAD_DIM, A_HEAD_DIM ** -0.5 * LOG2E)
    rope_c = _rope_tables(seq, C_HEAD_DIM, C_HEAD_DIM ** -0.5 * LOG2E)
    mod = _modulation(c, ada_w, ada_b)

    for layer in range(depth):
        sh1, sc1, g1, sh2, sc2, g2 = [
            mod[layer, :, None, m * d:(m + 1) * d] for m in range(N_MOD)]
        li = layer // 2
        if layer % 2 == 0:
            w_in = ab_w_in[li].astype(MXU_DTYPE)
            aq, ak, av, bq, bk, bv = jnp.split(
                w_in, [a_width, 2 * a_width, 3 * a_width,
                       3 * a_width + b_width, 3 * a_width + 2 * b_width], axis=1)
            qk = _project(x, sc1, sh1, jnp.concatenate([aq, ak, bq, bk], axis=1),
                          rope=rope_ab, rope_dim=A_HEAD_DIM,
                          is_query_col=lambda col: (col // a_width) % 2 == 0)
            vt = _project(x, sc1, sh1, jnp.concatenate([av, bv], axis=1), transpose_out=True)
            ya = _moba_attention(qk, vt, a_heads, q_col=0, k_col=a_heads, v_row=0)
            yb = _diff_attention(qk, vt, diff_lambda[li], diff_subln[li], _lambda_init(layer),
                                 b_heads, q_col=2 * a_heads, k_col=2 * a_heads + 2 * b_heads,
                                 v_row=a_width // (2 * B_HEAD_DIM))
            w_out = ab_w_out[li].astype(MXU_DTYPE)
            x = _out_project(x, g1, [ya, yb], [w_out[:a_width], w_out[a_width:]],
                             in_place=layer > 0)
        else:
            w_in, b_in = swa_w_in[li], swa_b_in[li]
            wq, wk, wv = jnp.split(w_in, [c_q_width, c_q_width + c_kv_width], axis=1)
            bq_, bk_, bv_ = jnp.split(b_in, [c_q_width, c_q_width + c_kv_width])
            w_qk = jnp.concatenate([wq, _dup_heads(wk, c_kv_heads, C_HEAD_DIM)], axis=1)
            b_qk = jnp.concatenate([bq_, _dup_heads(bk_, c_kv_heads, C_HEAD_DIM)])
            qk = _project(x, sc1, sh1, w_qk.astype(MXU_DTYPE), bias=b_qk,
                          rope=rope_c, rope_dim=C_HEAD_DIM,
                          is_query_col=lambda col: col < c_q_width)
            vt = _project(x, sc1, sh1, wv.astype(MXU_DTYPE), bias=bv_, transpose_out=True)
            y = _swa_attention(qk, vt, swa_sinks[li], c_kv_heads)
            x = _out_project(x, g1, [y], [swa_w_out[li].astype(MXU_DTYPE)], bias=swa_b_out[li])
        x = _ffn(x, sc2, sh2, g2, ffn_w_in[layer].astype(MXU_DTYPE),
                 ffn_w_out[layer].astype(MXU_DTYPE),
                 final_w=final_norm if layer == depth - 1 else None)
    return x
```

```python
import functools
import math

import jax
import jax.numpy as jnp
from jax import lax
from jax.experimental import pallas as pl
from jax.experimental.pallas import tpu as pltpu

F32 = jnp.float32
MXU_DTYPE = jnp.bfloat16

ROPE_THETA = 10000.0
RMS_EPS = 1e-6
DIFF_SUBLN_EPS = 1e-5
A_HEAD_DIM = 128
MOBA_BLOCK = 256
MOBA_BLOCK_LOG2 = 8
MOBA_TOPK = 3
B_HEAD_DIM = 128
C_HEAD_DIM = 64
C_GROUP = 8
C_WINDOW = 128
N_MOD = 6

LANES = 128
MASKED = -1e30
LOG2E = 1.4426950408889634

ATTN_TILE = 512
ROW_TILE = 512
FFN_ROW_TILE = 1024
MOD_COL_TILE = 1024
MIB = 2 ** 20
VMEM_LIMIT_MIB = dict(modulation=40, projection=58, moba=56, diff=58, swa=32,
                      out_projection=48, ffn=60)

NT_DIMS = (((1,), (1,)), ((), ()))


def _params(semantics, call):
    return pltpu.CompilerParams(dimension_semantics=semantics,
                                vmem_limit_bytes=VMEM_LIMIT_MIB[call] * MIB)


def _col_tile(n):
    for t in (512, 256, 128):
        if n % t == 0:
            return t
    raise ValueError(f"column count {n} is not a multiple of {LANES}")


def _column_tiles(w, tn):
    k, n = w.shape
    return w.reshape(k, n // tn, tn).transpose(1, 0, 2)


def _norm_mod(x, sc, sh):
    ms = jnp.mean(x * x, axis=-1, keepdims=True)
    return (x * lax.rsqrt(ms + RMS_EPS)) * (1.0 + sc) + sh


def _mod_kernel(c_ref, w_ref, b_ref, o_ref):
    c = c_ref[...]
    cond = (c * jax.nn.sigmoid(c)).astype(MXU_DTYPE)
    o_ref[...] = jnp.dot(cond, w_ref[...].astype(MXU_DTYPE),
                         preferred_element_type=F32) + b_ref[...]


def _modulation(c, ada_w, ada_b):
    depth, d, n = ada_w.shape
    bsz = c.shape[0]
    rows = -(-bsz // 8) * 8
    c_pad = jnp.pad(c, ((0, rows - bsz), (0, 0)))
    tn = MOD_COL_TILE if n % MOD_COL_TILE == 0 else _col_tile(n)
    out = pl.pallas_call(
        _mod_kernel,
        grid=(depth, n // tn),
        in_specs=[pl.BlockSpec((rows, d), lambda l, j: (0, 0)),
                  pl.BlockSpec((None, d, tn), lambda l, j: (l, 0, j)),
                  pl.BlockSpec((None, 1, tn), lambda l, j: (l, 0, j))],
        out_specs=pl.BlockSpec((None, rows, tn), lambda l, j: (l, 0, j)),
        out_shape=jax.ShapeDtypeStruct((depth, rows, n), F32),
        compiler_params=_params(("parallel", "parallel"), "modulation"),
    )(c_pad, ada_w, ada_b.reshape(depth, 1, n))
    return out[:, :bsz]


def _rope_partner(a, rope_dim):
    if rope_dim == LANES:
        return pltpu.roll(a, LANES // 2, 1)
    lane = lax.broadcasted_iota(jnp.int32, a.shape, 1)
    half = rope_dim // 2
    first_half = (lane & (rope_dim - 1)) < half
    return jnp.where(first_half, pltpu.roll(a, LANES - half, 1), pltpu.roll(a, half, 1))


def _proj_kernel(x_ref, sc_ref, sh_ref, wqk_ref, wv_ref, *rest, rope_dim, has_bias, query_tiles):
    rest = list(rest)
    bqk_ref, bv_ref = (rest.pop(0), rest.pop(0)) if has_bias else (None, None)
    cos_ref, sin_ref, o_ref, ot_ref = rest
    h = _norm_mod(x_ref[...], sc_ref[...], sh_ref[...]).astype(wqk_ref.dtype)
    n_t, _, tn = wqk_ref.shape
    for j in range(n_t):
        acc = jnp.dot(h, wqk_ref[j], preferred_element_type=F32)
        if has_bias:
            acc = acc + bqk_ref[:, j * tn:(j + 1) * tn]
        table = 0 if query_tiles[j] else 1
        cos, sin = cos_ref[table], sin_ref[table]
        for t in range(tn // LANES):
            a = acc[:, t * LANES:(t + 1) * LANES]
            o_ref[:, j * tn + t * LANES:j * tn + (t + 1) * LANES] = (
                a * cos + _rope_partner(a, rope_dim) * sin).astype(o_ref.dtype)
    n_tv, _, tnv = wv_ref.shape
    for j in range(n_tv):
        cols = slice(j * tnv, (j + 1) * tnv)
        acc = jnp.dot(h, wv_ref[j], preferred_element_type=F32)
        if has_bias:
            acc = acc + bv_ref[:, cols]
        ot_ref[cols, :] = acc.T.astype(ot_ref.dtype)


def _project(x, sc, sh, w_qk, w_v, rope, rope_dim, is_query_col, bias_qk=None, bias_v=None):
    bsz, seq, d = x.shape
    n, nv = w_qk.shape[1], w_v.shape[1]
    tm = min(ROW_TILE, seq)
    tn, tnv = _col_tile(n), _col_tile(nv)
    resident = lambda w, t: pl.BlockSpec((w.shape[1] // t, d, t), lambda b, i: (0, 0, 0),
                                         pipeline_mode=pl.Buffered(1))
    in_specs = [pl.BlockSpec((None, tm, d), lambda b, i: (b, i, 0)),
                pl.BlockSpec((None, 1, d), lambda b, i: (b, 0, 0)),
                pl.BlockSpec((None, 1, d), lambda b, i: (b, 0, 0)),
                resident(w_qk, tn), resident(w_v, tnv)]
    args = [x, sc, sh, _column_tiles(w_qk, tn), _column_tiles(w_v, tnv)]
    if bias_qk is not None:
        in_specs += [pl.BlockSpec((1, n), lambda b, i: (0, 0)),
                     pl.BlockSpec((1, nv), lambda b, i: (0, 0))]
        args += [bias_qk.reshape(1, n).astype(F32), bias_v.reshape(1, nv).astype(F32)]
    query_tiles = tuple(bool(is_query_col(j * tn)) for j in range(n // tn))
    in_specs += [pl.BlockSpec((2, tm, LANES), lambda b, i: (0, i, 0))] * 2
    args += list(rope)
    return pl.pallas_call(
        functools.partial(_proj_kernel, rope_dim=rope_dim, has_bias=bias_qk is not None,
                          query_tiles=query_tiles),
        grid=(bsz, seq // tm),
        in_specs=in_specs,
        out_specs=[pl.BlockSpec((None, tm, n), lambda b, i: (b, i, 0)),
                   pl.BlockSpec((None, None, nv, tm), lambda b, i: (b, i, 0, 0))],
        out_shape=[jax.ShapeDtypeStruct((bsz, seq, n), MXU_DTYPE),
                   jax.ShapeDtypeStruct((bsz, seq // tm, nv, tm), MXU_DTYPE)],
        compiler_params=_params(("parallel", "parallel"), "projection"),
    )(*args)


N_FLASH_REFS = 10
SLAB_ROWS = 64
ONES_ROWS = 16
FLASH_GROUP = 2


def _flash_scratch(t, v_dim):
    stat = pltpu.VMEM((1, t), F32)
    scores = pltpu.VMEM((t, t), F32)
    probs = pltpu.VMEM((t, t), MXU_DTYPE)
    return [stat, pltpu.VMEM((v_dim + ONES_ROWS, t), F32),
            scores, scores, stat, stat, probs, probs, stat, stat]


class _FlashStream:
    def __init__(self, refs):
        self.m, self.acc = refs[:2]
        self.s, self.s_max, self.p, self.alpha = refs[2:4], refs[4:6], refs[6:8], refs[8:10]

    def result(self):
        v_dim = self.acc.shape[0] - ONES_ROWS
        return self.acc[:v_dim, :] / self.acc[v_dim:v_dim + 1, :]


def _col_max(x):
    slab = x[:SLAB_ROWS]
    for r in range(SLAB_ROWS, x.shape[0], SLAB_ROWS):
        slab = jnp.maximum(slab, x[r:r + SLAB_ROWS])
    return jnp.max(slab, axis=0, keepdims=True)


def _causal_flash(i, streams, scores_of, values_of):
    for st in streams:
        st.m[...] = jnp.full_like(st.m, MASKED)
        st.acc[...] = jnp.zeros_like(st.acc)
        st.alpha[1][...] = jnp.ones_like(st.alpha[1])
        st.p[1][...] = jnp.zeros_like(st.p[1])

    def put_scores(st, slot, s):
        st.s[slot][...] = s
        st.s_max[slot][...] = _col_max(s)

    for idx, st in enumerate(streams):
        s = scores_of(idx, i)
        krow = lax.broadcasted_iota(jnp.int32, s.shape, 0)
        qcol = lax.broadcasted_iota(jnp.int32, s.shape, 1)
        put_scores(st, 0, jnp.where(krow <= qcol, s, MASKED))
    n_steps = i + 1

    def values_step(k, slot):
        tile = jnp.where(k <= 0, i, k - 1)
        for idx, st in enumerate(streams):
            vt = values_of(idx, tile)
            vt = jnp.concatenate([vt, jnp.ones((ONES_ROWS, vt.shape[1]), vt.dtype)], axis=0)
            st.acc[...] = st.alpha[slot][...] * st.acc[...] + jnp.dot(
                vt, st.p[slot][...], preferred_element_type=F32)

    def step(k, cur, is_last):
        nxt = 1 - cur
        ahead = k
        tile = jnp.where(k <= 1, i, k - 2)
        for g0 in range(0, len(streams), FLASH_GROUP):
            group = list(enumerate(streams))[g0:g0 + FLASH_GROUP]
            for idx, st in group:
                m_prev = st.m[...]
                m_new = jnp.maximum(m_prev, st.s_max[cur][...])
                for r in range(0, st.s[cur].shape[0], SLAB_ROWS):
                    p = jnp.exp2(st.s[cur][r:r + SLAB_ROWS, :] - m_new)
                    st.p[cur][r:r + SLAB_ROWS, :] = p.astype(st.p[cur].dtype)
                st.m[...] = m_new
                st.alpha[cur][...] = jnp.exp2(m_prev - m_new)
            if not is_last:
                for idx, st in group:
                    put_scores(st, nxt, scores_of(idx, ahead))
            for idx, st in group:
                vt = values_of(idx, tile)
                vt = jnp.concatenate([vt, jnp.ones((ONES_ROWS, vt.shape[1]), vt.dtype)], axis=0)
                st.acc[...] = st.alpha[nxt][...] * st.acc[...] + jnp.dot(
                    vt, st.p[nxt][...], preferred_element_type=F32)

    def pair_body(kk, carry):
        step(2 * kk, 0, False)
        step(2 * kk + 1, 1, False)
        return carry

    n_pairs = (n_steps - 1) // 2
    lax.fori_loop(0, n_pairs, pair_body, 0)
    done = 2 * n_pairs

    @pl.when(n_steps - done == 1)
    def _():
        step(done, 0, True)
        values_step(done, 0)

    @pl.when(n_steps - done == 2)
    def _():
        step(done, 0, False)
        step(done + 1, 1, True)
        values_step(done + 1, 1)


MOBA_HEADS_PER_STEP = 4


def _moba_kernel(q_ref, k_ref, vt_ref, o_ref, kmean_ref, qaug_ref, *state, n_blocks):
    i = pl.program_id(2)
    tq = q_ref.shape[0]
    tk = tq
    hd = A_HEAD_DIM
    heads = [slice(e * hd, (e + 1) * hd) for e in range(MOBA_HEADS_PER_STEP)]

    @pl.when(i == 0)
    def _():
        kmean_ref[...] = jnp.zeros_like(kmean_ref)

        def mean_body(n, carry):
            kb = k_ref[pl.ds(pl.multiple_of(n * MOBA_BLOCK, MOBA_BLOCK), MOBA_BLOCK), :]
            mean = jnp.sum(kb.astype(F32), axis=0, keepdims=True) * (1.0 / MOBA_BLOCK)
            for e, cols in enumerate(heads):
                kmean_ref[e, pl.ds(n, 1), :] = mean[:, cols]
            return carry

        lax.fori_loop(0, n_blocks, mean_body, 0)

    rows = min(LANES, -(-n_blocks // 16) * 16)
    blk = lax.broadcasted_iota(jnp.int32, (rows, tq), 0)
    qpos = lax.broadcasted_iota(jnp.int32, (rows, tq), 1)
    own = jnp.right_shift(i * tq + qpos, MOBA_BLOCK_LOG2)
    past = blk < own
    blkf = blk.astype(F32)
    for e, cols in enumerate(heads):
        q_t = q_ref[:, cols].astype(F32).T.astype(MXU_DTYPE)
        km = kmean_ref[e, :rows, :]
        k_hi = km.astype(MXU_DTYPE)
        r1 = km - k_hi.astype(F32)
        k_mid = r1.astype(MXU_DTYPE)
        k_lo = (r1 - k_mid.astype(F32)).astype(MXU_DTYPE)
        gate = (jnp.dot(k_lo, q_t, preferred_element_type=F32)
                + jnp.dot(k_mid, q_t, preferred_element_type=F32)
                + jnp.dot(k_hi, q_t, preferred_element_type=F32))
        g = jnp.where(past, gate, -jnp.inf)
        chosen = jnp.zeros(gate.shape, jnp.bool_)
        for _ in range(min(MOBA_TOPK, n_blocks)):
            best = jnp.max(g, axis=0, keepdims=True)
            first_best = jnp.min(jnp.where(g == best, blkf, float(LANES)), axis=0, keepdims=True)
            pick = blkf == first_best
            chosen = jnp.logical_or(chosen, pick)
            g = jnp.where(pick, -jnp.inf, g)
        visible = jnp.logical_or(jnp.logical_and(chosen, past), blk == own)
        qaug_ref[e, :hd, :] = q_t
        qaug_ref[e, hd:hd + rows, :] = jnp.where(visible, 0.0, MASKED).astype(qaug_ref.dtype)
        if rows < LANES:
            qaug_ref[e, hd + rows:, :] = jnp.zeros((LANES - rows, tq), qaug_ref.dtype)

    def scores_of(e, j):
        kt = k_ref[pl.ds(pl.multiple_of(j * tk, tk), tk), heads[e]]
        lane = lax.broadcasted_iota(jnp.int32, (1, LANES), 1)
        blocks_per_tile = tk // MOBA_BLOCK
        onehot = jnp.concatenate(
            [jnp.broadcast_to(jnp.where(lane == j * blocks_per_tile + b, 1.0, 0.0),
                              (MOBA_BLOCK, LANES)) for b in range(blocks_per_tile)], axis=0)
        k_aug = jnp.concatenate([kt, onehot.astype(kt.dtype)], axis=1)
        return jnp.dot(k_aug, qaug_ref[e], preferred_element_type=F32)

    streams = [_FlashStream(state[e * N_FLASH_REFS:(e + 1) * N_FLASH_REFS])
               for e in range(MOBA_HEADS_PER_STEP)]
    _causal_flash(i, streams, scores_of, lambda e, j: vt_ref[j, heads[e], :])
    for e, st in enumerate(streams):
        o_ref[:, heads[e]] = st.result().T.astype(o_ref.dtype)


def _moba_attention(qk, vt, n_heads, q_col, k_col, v_row):
    bsz, seq, _ = qk.shape
    t = min(ATTN_TILE, seq)
    n_tiles = seq // t
    n_blocks = seq // MOBA_BLOCK
    per = MOBA_HEADS_PER_STEP
    assert seq % t == 0 and t % MOBA_BLOCK == 0 and n_blocks <= LANES
    assert n_heads % per == 0 and q_col % per == 0 and k_col % per == 0 and v_row % per == 0
    hd = A_HEAD_DIM
    scratch = [pltpu.VMEM((per, LANES, hd), F32), pltpu.VMEM((per, 2 * hd, t), MXU_DTYPE)]
    for _ in range(per):
        scratch += _flash_scratch(t, hd)
    return pl.pallas_call(
        functools.partial(_moba_kernel, n_blocks=n_blocks),
        grid=(bsz, n_heads // per, n_tiles),
        in_specs=[pl.BlockSpec((None, t, per * hd), lambda b, h, i: (b, i, q_col // per + h)),
                  pl.BlockSpec((None, seq, per * hd), lambda b, h, i: (b, 0, k_col // per + h),
                               pipeline_mode=pl.Buffered(1)),
                  pl.BlockSpec((None, n_tiles, per * hd, t),
                               lambda b, h, i: (b, 0, v_row // per + h, 0),
                               pipeline_mode=pl.Buffered(1))],
        out_specs=pl.BlockSpec((None, t, per * hd), lambda b, h, i: (b, i, h)),
        out_shape=jax.ShapeDtypeStruct((bsz, seq, n_heads * hd), MXU_DTYPE),
        scratch_shapes=scratch,
        compiler_params=_params(("parallel", "parallel", "arbitrary"), "moba"),
    )(qk, qk, vt)


DIFF_HEADS_PER_STEP = 2


def _diff_kernel(q_ref, k_ref, vt_ref, lam_ref, w_ref, o_ref, *state, lambda_init):
    i = pl.program_id(2)
    tk = q_ref.shape[0]
    hd = B_HEAD_DIM
    n_streams = 2 * DIFF_HEADS_PER_STEP
    streams = [_FlashStream(state[s * N_FLASH_REFS:(s + 1) * N_FLASH_REFS])
               for s in range(n_streams)]

    def scores_of(s, j):
        cols = slice(s * hd, (s + 1) * hd)
        rows = pl.ds(pl.multiple_of(j * tk, tk), tk)
        return lax.dot_general(k_ref[rows, cols], q_ref[:, cols], NT_DIMS,
                               preferred_element_type=F32)

    def values_of(s, j):
        head = s // 2
        return vt_ref[j, head * 2 * hd:(head + 1) * 2 * hd, :]

    _causal_flash(i, streams, scores_of, values_of)

    lv = lam_ref[...]
    lam = (jnp.exp(jnp.sum(lv[0:1] * lv[1:2], axis=1, keepdims=True))
           - jnp.exp(jnp.sum(lv[2:3] * lv[3:4], axis=1, keepdims=True)) + lambda_init)
    for head in range(DIFF_HEADS_PER_STEP):
        y = streams[2 * head].result() - lam * streams[2 * head + 1].result()
        ms = jnp.mean(y * y, axis=0, keepdims=True)
        y = (y * lax.rsqrt(ms + DIFF_SUBLN_EPS)) * w_ref[...]
        o_ref[:, head * 2 * hd:(head + 1) * 2 * hd] = (
            (y * (1.0 - lambda_init)).T.astype(o_ref.dtype))


def _diff_attention(qk, vt, lam_vecs, subln_w, lambda_init, n_heads, q_col, k_col, v_row):
    bsz, seq, _ = qk.shape
    t = min(ATTN_TILE, seq)
    n_tiles = seq // t
    hd = B_HEAD_DIM
    per = DIFF_HEADS_PER_STEP
    assert n_heads % per == 0 and q_col % (2 * per) == 0 and k_col % (2 * per) == 0
    assert v_row % per == 0
    width = 2 * per * hd
    scratch = []
    for _ in range(2 * per):
        scratch += _flash_scratch(t, 2 * hd)
    return pl.pallas_call(
        functools.partial(_diff_kernel, lambda_init=lambda_init),
        grid=(bsz, n_heads // per, n_tiles),
        in_specs=[pl.BlockSpec((None, t, width), lambda b, h, i: (b, i, q_col // (2 * per) + h)),
                  pl.BlockSpec((None, seq, width), lambda b, h, i: (b, 0, k_col // (2 * per) + h),
                               pipeline_mode=pl.Buffered(1)),
                  pl.BlockSpec((None, n_tiles, width, t),
                               lambda b, h, i: (b, 0, v_row // per + h, 0),
                               pipeline_mode=pl.Buffered(1)),
                  pl.BlockSpec((4, hd), lambda b, h, i: (0, 0)),
                  pl.BlockSpec((2 * hd, 1), lambda b, h, i: (0, 0))],
        out_specs=pl.BlockSpec((None, t, width), lambda b, h, i: (b, i, h)),
        out_shape=jax.ShapeDtypeStruct((bsz, seq, n_heads * 2 * hd), MXU_DTYPE),
        scratch_shapes=scratch,
        compiler_params=_params(("parallel", "parallel", "arbitrary"), "diff"),
    )(qk, qk, vt, lam_vecs.astype(F32), subln_w.astype(F32).reshape(2 * hd, 1))


def _swa_kernel(q_ref, kc_ref, kp_ref, vtc_ref, vtp_ref, sink_ref, o_ref):
    i = pl.program_id(1)
    w = C_WINDOW
    hd = C_HEAD_DIM
    gw = C_GROUP * hd
    lane = lax.broadcasted_iota(jnp.int32, (w, LANES), 1)
    lo = jnp.where(lane < hd, 1.0, 0.0).astype(q_ref.dtype)
    hi = jnp.where(lane >= hd, 1.0, 0.0).astype(q_ref.dtype)
    krow = lax.broadcasted_iota(jnp.int32, (2 * w, C_GROUP * w), 0)
    qcol = lax.broadcasted_iota(jnp.int32, (2 * w, C_GROUP * w), 1) & (w - 1)
    rel = w + qcol - krow
    valid = (rel >= 0) & (rel < w) & ((i - 1) * w + krow >= 0)
    ones = jnp.ones((ONES_ROWS, 2 * w), vtc_ref.dtype)
    groups = range(q_ref.shape[1] // gw)
    scores = []
    for g in groups:
        k_cols = slice(g * LANES, (g + 1) * LANES)
        kk = jnp.concatenate([kp_ref[:, k_cols], kc_ref[:, k_cols]], axis=0)
        q_heads = []
        for p in range(C_GROUP // 2):
            q2 = q_ref[:, g * gw + p * LANES:g * gw + (p + 1) * LANES]
            q_heads += [q2 * lo, q2 * hi]
        q_all = jnp.concatenate(q_heads, axis=0)
        s = lax.dot_general(kk, q_all, NT_DIMS, preferred_element_type=F32)
        scores.append(jnp.where(valid, s, MASKED))
    sinks = [sink_ref[g] * LOG2E for g in groups]
    maxes = [jnp.maximum(_col_max(s), sink) for s, sink in zip(scores, sinks)]
    probs = [jnp.concatenate([jnp.exp2(s[r:r + SLAB_ROWS] - m).astype(vtc_ref.dtype)
                              for r in range(0, 2 * w, SLAB_ROWS)], axis=0)
             for s, m in zip(scores, maxes)]
    for g in groups:
        v_rows = slice(g * hd, (g + 1) * hd)
        vt = jnp.concatenate([vtp_ref[v_rows, :], vtc_ref[v_rows, :]], axis=1)
        o = jnp.dot(jnp.concatenate([vt, ones], axis=0), probs[g], preferred_element_type=F32)
        o = o[:hd] / (o[hd:hd + 1] + jnp.exp2(sinks[g] - maxes[g]))
        o = jnp.concatenate([o[:, h * w:(h + 1) * w] for h in range(C_GROUP)], axis=0)
        o_ref[:, g * gw:(g + 1) * gw] = o.T.astype(o_ref.dtype)


def _swa_attention(qk, vt, sinks, n_kv):
    bsz, seq, _ = qk.shape
    w = C_WINDOW
    q_width = n_kv * C_GROUP * C_HEAD_DIM
    k_width = n_kv * LANES
    assert q_width % k_width == 0
    per_tile = vt.shape[-1] // w
    sink_cols = jnp.repeat(sinks.astype(F32).reshape(n_kv, 1, C_GROUP), w, axis=-1)
    prev = lambda i: jnp.maximum(i - 1, 0)
    vt_spec = lambda at: pl.BlockSpec(
        (None, None, n_kv * C_HEAD_DIM, w),
        lambda b, i: (b, at(i) // per_tile, 0, at(i) % per_tile))
    return pl.pallas_call(
        _swa_kernel,
        grid=(bsz, seq // w),
        in_specs=[pl.BlockSpec((None, w, q_width), lambda b, i: (b, i, 0)),
                  pl.BlockSpec((None, w, k_width), lambda b, i: (b, i, q_width // k_width)),
                  pl.BlockSpec((None, w, k_width), lambda b, i: (b, prev(i), q_width // k_width)),
                  vt_spec(lambda i: i),
                  vt_spec(prev),
                  pl.BlockSpec((n_kv, 1, C_GROUP * w), lambda b, i: (0, 0, 0))],
        out_specs=pl.BlockSpec((None, w, q_width), lambda b, i: (b, i, 0)),
        out_shape=jax.ShapeDtypeStruct((bsz, seq, q_width), MXU_DTYPE),
        compiler_params=_params(("parallel", "parallel"), "swa"),
    )(qk, qk, qk, vt, vt, sink_cols)


def _outproj_kernel(*refs, n_parts, has_bias):
    y_refs, w_refs = refs[:n_parts], refs[n_parts:2 * n_parts]
    rest = list(refs[2 * n_parts:])
    b_ref = rest.pop(0) if has_bias else None
    g_ref, x_ref, o_ref = rest
    acc = jnp.dot(y_refs[0][...], w_refs[0][...], preferred_element_type=F32)
    for y_ref, w_ref in zip(y_refs[1:], w_refs[1:]):
        acc = acc + jnp.dot(y_ref[...], w_ref[...], preferred_element_type=F32)
    if has_bias:
        acc = acc + b_ref[...]
    o_ref[...] = x_ref[...] + g_ref[...] * acc


def _out_project(x, gate, ys, ws, bias=None, in_place=True):
    bsz, seq, d = x.shape
    tm = min(ROW_TILE, seq)
    in_specs = [pl.BlockSpec((None, tm, y.shape[-1]), lambda b, i: (b, i, 0)) for y in ys]
    in_specs += [pl.BlockSpec(w.shape, lambda b, i: (0, 0)) for w in ws]
    args = list(ys) + list(ws)
    if bias is not None:
        in_specs.append(pl.BlockSpec((1, d), lambda b, i: (0, 0)))
        args.append(bias.reshape(1, d).astype(F32))
    in_specs += [pl.BlockSpec((None, 1, d), lambda b, i: (b, 0, 0)),
                 pl.BlockSpec((None, tm, d), lambda b, i: (b, i, 0))]
    args += [gate, x]
    return pl.pallas_call(
        functools.partial(_outproj_kernel, n_parts=len(ys), has_bias=bias is not None),
        grid=(bsz, seq // tm),
        in_specs=in_specs,
        out_specs=pl.BlockSpec((None, tm, d), lambda b, i: (b, i, 0)),
        out_shape=jax.ShapeDtypeStruct(x.shape, F32),
        input_output_aliases={len(args) - 1: 0} if in_place else {},
        compiler_params=_params(("parallel", "parallel"), "out_projection"),
    )(*args)


def _ffn_kernel(x_ref, sc_ref, sh_ref, g_ref, wg_ref, wu_ref, wo_ref, *rest, final):
    rest = list(rest)
    fw_ref = rest.pop(0) if final else None
    o_ref, h_ref = rest
    k = pl.program_id(2)
    n_h = pl.num_programs(2)
    tm = x_ref.shape[0]
    halves = [slice(r, r + min(ROW_TILE, tm)) for r in range(0, tm, ROW_TILE)]

    def chunk(first, last):
        for rows in halves:
            if first:
                h = _norm_mod(x_ref[rows, :], sc_ref[...], sh_ref[...]).astype(h_ref.dtype)
                h_ref[rows, :] = h
            else:
                h = h_ref[rows, :]
            gt = jnp.dot(h, wg_ref[...], preferred_element_type=F32)
            up = jnp.dot(h, wu_ref[...], preferred_element_type=F32)
            act = ((gt * jax.nn.sigmoid(gt)) * up).astype(h.dtype)
            acc = jnp.dot(act, wo_ref[...], preferred_element_type=F32)
            if not first:
                acc = o_ref[rows, :] + acc
            if last:
                acc = x_ref[rows, :] + g_ref[...] * acc
                if final:
                    ms = jnp.mean(acc * acc, axis=-1, keepdims=True)
                    acc = (acc * lax.rsqrt(ms + RMS_EPS)) * fw_ref[...]
            o_ref[rows, :] = acc

    @pl.when(k == 0)
    def _():
        chunk(True, False)

    @pl.when((k > 0) & (k < n_h - 1))
    def _():
        chunk(False, False)

    @pl.when(k == n_h - 1)
    def _():
        chunk(False, True)


def _ffn(x, sc, sh, gate, w_in, w_out, final_w=None):
    bsz, seq, d = x.shape
    hidden = w_out.shape[0]
    tm = min(FFN_ROW_TILE, seq)
    th = _col_tile(hidden)
    n_h = hidden // th
    assert n_h >= 2
    mod_spec = pl.BlockSpec((None, 1, d), lambda b, i, k: (b, 0, 0))
    in_specs = [pl.BlockSpec((None, tm, d), lambda b, i, k: (b, i, 0)),
                mod_spec, mod_spec, mod_spec,
                pl.BlockSpec((d, th), lambda b, i, k: (0, k)),
                pl.BlockSpec((d, th), lambda b, i, k: (0, n_h + k)),
                pl.BlockSpec((th, d), lambda b, i, k: (k, 0))]
    args = [x, sc, sh, gate, w_in, w_in, w_out]
    if final_w is not None:
        in_specs.append(pl.BlockSpec((1, d), lambda b, i, k: (0, 0)))
        args.append(final_w.reshape(1, d).astype(F32))
    return pl.pallas_call(
        functools.partial(_ffn_kernel, final=final_w is not None),
        grid=(bsz, seq // tm, n_h),
        in_specs=in_specs,
        out_specs=pl.BlockSpec((None, tm, d), lambda b, i, k: (b, i, 0)),
        out_shape=jax.ShapeDtypeStruct(x.shape, F32),
        scratch_shapes=[pltpu.VMEM((tm, d), MXU_DTYPE)],
        input_output_aliases={0: 0},
        compiler_params=_params(("parallel", "parallel", "arbitrary"), "ffn"),
    )(*args)


def _rope_tables(seq, dim, query_scale):
    pos = jnp.arange(seq, dtype=F32)
    inv_freq = ROPE_THETA ** (-jnp.arange(0, dim, 2, dtype=F32) / dim)
    ang = pos[:, None] * inv_freq[None, :]
    cos, sin = jnp.cos(ang), jnp.sin(ang)
    reps = LANES // dim
    cos = jnp.tile(jnp.concatenate([cos, cos], axis=1), (1, reps))
    sin = jnp.tile(jnp.concatenate([-sin, sin], axis=1), (1, reps))
    return jnp.stack([cos * query_scale, cos]), jnp.stack([sin * query_scale, sin])


def _lambda_init(layer):
    return 0.8 - 0.6 * math.exp(-0.3 * layer)


def _dup_heads(w, n_heads, head_dim):
    lead = w.shape[:-1]
    w = w.reshape(lead + (n_heads, 1, head_dim))
    return jnp.broadcast_to(w, lead + (n_heads, 2, head_dim)).reshape(lead + (2 * n_heads * head_dim,))


def kernel(x, c, ada_w, ada_b, ab_w_in, ab_w_out, diff_lambda, diff_subln, swa_w_in, swa_b_in,
           swa_w_out, swa_b_out, swa_sinks, ffn_w_in, ffn_w_out, final_norm):
    bsz, seq, d = x.shape
    depth = ada_w.shape[0]
    a_heads, b_heads = d // 256, d // 512
    a_width, b_width = a_heads * A_HEAD_DIM, 2 * b_heads * B_HEAD_DIM
    assert a_width == b_width
    c_q_heads = d // C_HEAD_DIM
    c_kv_heads = c_q_heads // C_GROUP
    c_q_width, c_kv_width = c_q_heads * C_HEAD_DIM, c_kv_heads * C_HEAD_DIM

    rope_ab = _rope_tables(seq, A_HEAD_DIM, A_HEAD_DIM ** -0.5 * LOG2E)
    rope_c = _rope_tables(seq, C_HEAD_DIM, C_HEAD_DIM ** -0.5 * LOG2E)
    mod = _modulation(c, ada_w, ada_b)

    for layer in range(depth):
        sh1, sc1, g1, sh2, sc2, g2 = [
            mod[layer, :, None, m * d:(m + 1) * d] for m in range(N_MOD)]
        li = layer // 2
        if layer % 2 == 0:
            w_in = ab_w_in[li].astype(MXU_DTYPE)
            aq, ak, av, bq, bk, bv = jnp.split(
                w_in, [a_width, 2 * a_width, 3 * a_width,
                       3 * a_width + b_width, 3 * a_width + 2 * b_width], axis=1)
            qk, vt = _project(x, sc1, sh1, jnp.concatenate([aq, ak, bq, bk], axis=1),
                              jnp.concatenate([av, bv], axis=1), rope_ab, A_HEAD_DIM,
                              is_query_col=lambda col: (col // a_width) % 2 == 0)
            ya = _moba_attention(qk, vt, a_heads, q_col=0, k_col=a_heads, v_row=0)
            yb = _diff_attention(qk, vt, diff_lambda[li], diff_subln[li], _lambda_init(layer),
                                 b_heads, q_col=2 * a_heads, k_col=2 * a_heads + 2 * b_heads,
                                 v_row=a_width // (2 * B_HEAD_DIM))
            w_out = ab_w_out[li].astype(MXU_DTYPE)
            x = _out_project(x, g1, [ya, yb], [w_out[:a_width], w_out[a_width:]],
                             in_place=layer > 0)
        else:
            w_in, b_in = swa_w_in[li], swa_b_in[li]
            wq, wk, wv = jnp.split(w_in, [c_q_width, c_q_width + c_kv_width], axis=1)
            bq_, bk_, bv_ = jnp.split(b_in, [c_q_width, c_q_width + c_kv_width])
            w_qk = jnp.concatenate([wq, _dup_heads(wk, c_kv_heads, C_HEAD_DIM)], axis=1)
            b_qk = jnp.concatenate([bq_, _dup_heads(bk_, c_kv_heads, C_HEAD_DIM)])
            qk, vt = _project(x, sc1, sh1, w_qk.astype(MXU_DTYPE), wv.astype(MXU_DTYPE),
                              rope_c, C_HEAD_DIM, is_query_col=lambda col: col < c_q_width,
                              bias_qk=b_qk, bias_v=bv_)
            y = _swa_attention(qk, vt, swa_sinks[li], c_kv_heads)
            x = _out_project(x, g1, [y], [swa_w_out[li].astype(MXU_DTYPE)], bias=swa_b_out[li])
        x = _ffn(x, sc2, sh2, g2, ffn_w_in[layer].astype(MXU_DTYPE),
                 ffn_w_out[layer].astype(MXU_DTYPE),
                 final_w=final_norm if layer == depth - 1 else None)
    return x
```
